```python
import math
import jax, jax.numpy as jnp
from jax import lax
import numpy as np

D_MODEL = 1024
BATCH = 8
SEQ = 2048
DEPTH = 4
DEC_BATCH = 128
DEC_SEQ = 4
PAST_LEN = 16384
PAGE_SIZE = 128

MIX_WIDTH = D_MODEL
RWKV_WIDTH = MIX_WIDTH // 2
RWKV_HEAD = 64
RWKV_HEADS = RWKV_WIDTH // RWKV_HEAD
LORA_W = 64
LORA_A = 64
LORA_V = 32
S5_WIDTH = MIX_WIDTH - RWKV_WIDTH
S5_GROUP = 16
S5_GROUPS = S5_WIDTH // S5_GROUP
S5_STATE = 64
MEM_LEN = 256
X_HEADS = 4
X_HEAD_DIM = D_MODEL // X_HEADS
SHIFT_COLS = 3 * RWKV_WIDTH + LORA_W + LORA_A
IN_COLS = SHIFT_COLS + RWKV_WIDTH + 2 * S5_WIDTH
NORM_EPS = 1e-6
GN_EPS = 64e-5

kernel_name = 'hymba_rwkv7_s5_xmem_step'

F32 = jnp.float32


def rms_norm(x, g):
    xf = x.astype(F32)
    y = xf * lax.rsqrt(jnp.mean(xf * xf, axis=-1, keepdims=True) + NORM_EPS)
    return (y * g.astype(F32)).astype(x.dtype)


def token_shift(p, prev, mu):
    p_prev = jnp.concatenate([prev[:, None, :].astype(p.dtype), p[:, :-1, :]], axis=1)
    return p + (p_prev - p) * mu


def wkv_scan(r, w, k, v, kk, a, S0):
    def step(S, inp):
        r_t, w_t, k_t, v_t, kk_t, a_t = inp
        sk = jnp.einsum('bhij,bhj->bhi', S, kk_t)
        S = (S * w_t[:, :, None, :]
             - sk[..., None] * (kk_t * a_t)[:, :, None, :]
             + v_t[..., None] * k_t[:, :, None, :])
        y = jnp.einsum('bhij,bhj->bhi', S, r_t)
        return S, y
    xs = tuple(jnp.moveaxis(t, 1, 0) for t in (r, w, k, v, kk, a))
    S, ys = lax.scan(step, S0.astype(F32), xs)
    return jnp.moveaxis(ys, 0, 1), S


def rwkv7_mix(ps, gate, v_first, S0, w0, w2, a0, a2, k_k, k_a, r_k, gn_w, gn_b, vres):
    Bsz, T, _ = ps.shape
    r, k, v, wl, al = jnp.split(
        ps, [RWKV_WIDTH, 2 * RWKV_WIDTH, 3 * RWKV_WIDTH, 3 * RWKV_WIDTH + LORA_W], axis=-1)
    wlog = -jax.nn.softplus(-(w0.astype(F32) + jnp.tanh(wl) @ w2.astype(F32))) - 0.5
    decay = jnp.exp(-jnp.exp(wlog))
    a = jax.nn.sigmoid(a0.astype(F32) + al @ a2.astype(F32))
    if vres is None:
        v_first = v
    else:
        v0, v1, v2 = vres
        v = v + (v_first - v) * jax.nn.sigmoid(
            v0.astype(F32) + (v @ v1.astype(F32)) @ v2.astype(F32))
    heads = lambda t: t.reshape(Bsz, T, RWKV_HEADS, RWKV_HEAD)
    kk = heads(k * k_k.astype(F32))
    kk = kk * lax.rsqrt(jnp.maximum(jnp.sum(kk * kk, axis=-1, keepdims=True), 1e-24))
    k = k * (1.0 + (a - 1.0) * k_a.astype(F32))
    rh, kh, vh, ah = heads(r), heads(k), heads(v), heads(a)
    y, S = wkv_scan(rh, heads(decay), kh, vh, kk, ah, S0)
    mu = jnp.mean(y, axis=-1, keepdims=True)
    var = jnp.mean(jnp.square(y - mu), axis=-1, keepdims=True)
    y = (y - mu) * lax.rsqrt(var + GN_EPS)
    y = (y * gn_w.astype(F32).reshape(RWKV_HEADS, RWKV_HEAD)
         + gn_b.astype(F32).reshape(RWKV_HEADS, RWKV_HEAD))
    y = y + jnp.sum(rh * kh * r_k.astype(F32), axis=-1, keepdims=True) * vh
    out = y.reshape(Bsz, T, RWKV_WIDTH) * jax.nn.silu(gate.astype(F32))
    return out, v_first, S


def cmul(ar, ai, br, bi):
    return ar * br - ai * bi, ar * bi + ai * br


def s5_mix(u, gate, h0_re, h0_im, lam_re, lam_im, log_dt, b_re, b_im, c_re, c_im, d_skip, w_glu):
    Bsz, T, _ = u.shape
    uf = u.astype(F32)
    ug = uf.reshape(Bsz, T, S5_GROUPS, S5_GROUP)
    dt = jnp.exp(log_dt.astype(F32))[:, None]
    lr, li = lam_re.astype(F32), lam_im.astype(F32)
    mag = jnp.exp(lr * dt)
    lb_re, lb_im = mag * jnp.cos(li * dt), mag * jnp.sin(li * dt)
    q_re, q_im = lb_re - 1.0, lb_im
    den = lr * lr + li * li
    f_re = (q_re * lr + q_im * li) / den
    f_im = (q_im * lr - q_re * li) / den
    bb_re, bb_im = cmul(f_re[..., None], f_im[..., None], b_re.astype(F32), b_im.astype(F32))
    bu_re = jnp.einsum('gnc,btgc->btgn', bb_re, ug)
    bu_im = jnp.einsum('gnc,btgc->btgn', bb_im, ug)
    a_re = jnp.broadcast_to(lb_re, bu_re.shape)
    a_im = jnp.broadcast_to(lb_im, bu_im.shape)

    def combine(e1, e2):
        a1r, a1i, b1r, b1i = e1
        a2r, a2i, b2r, b2i = e2
        ar, ai = cmul(a1r, a1i, a2r, a2i)
        br, bi = cmul(a2r, a2i, b1r, b1i)
        return ar, ai, br + b2r, bi + b2i

    Ar, Ai, Hr, Hi = lax.associative_scan(combine, (a_re, a_im, bu_re, bu_im), axis=1)
    cr, ci = cmul(Ar, Ai, h0_re.astype(F32)[:, None], h0_im.astype(F32)[:, None])
    Hr = Hr + cr
    Hi = Hi + ci
    y = (jnp.einsum('gcn,btgn->btgc', c_re.astype(F32), Hr)
         - jnp.einsum('gcn,btgn->btgc', c_im.astype(F32), Hi))
    y = y.reshape(Bsz, T, S5_WIDTH) + d_skip.astype(F32) * uf
    y = jax.nn.gelu(y, approximate=False)
    y = y * jax.nn.sigmoid(y @ w_glu.astype(F32))
    out = y * jax.nn.silu(gate.astype(F32))
    return out, Hr[:, -1], Hi[:, -1]


def cross_attend(xn, mk, mv, wq, wo):
    Bsz, T, _ = xn.shape
    q = (xn @ wq).reshape(Bsz, T, X_HEADS, X_HEAD_DIM)
    s = jnp.einsum('bthd,bmhd->bhtm', q, mk).astype(F32) / math.sqrt(X_HEAD_DIM)
    p = jax.nn.softmax(s, axis=-1).astype(mv.dtype)
    o = jnp.einsum('bhtm,bmhd->bthd', p, mv).reshape(Bsz, T, D_MODEL)
    return o @ wo


def run_trunk(x, shift0, wkv0, s5re0, s5im0, mem_k, mem_v, P):
    new_shift, new_wkv, new_re, new_im = [], [], [], []
    v_first = None
    for l in range(DEPTH):
        xn = rms_norm(x, P['norm_mix'][l])
        proj = xn @ P['w_in'][l]
        p_sh = proj[..., :SHIFT_COLS]
        new_shift.append(p_sh[:, -1])
        ps = token_shift(p_sh, shift0[l], P['mu_shift'][l]).astype(F32)
        g_rwkv = proj[..., SHIFT_COLS:SHIFT_COLS + RWKV_WIDTH]
        u_s5 = proj[..., SHIFT_COLS + RWKV_WIDTH:SHIFT_COLS + RWKV_WIDTH + S5_WIDTH]
        g_s5 = proj[..., SHIFT_COLS + RWKV_WIDTH + S5_WIDTH:]
        vres = None if l == 0 else (P['v0'][l - 1], P['v1'][l - 1], P['v2'][l - 1])
        o_rwkv, v_first, S = rwkv7_mix(
            ps, g_rwkv, v_first, wkv0[l], P['w0'][l], P['w2'][l], P['a0'][l], P['a2'][l],
            P['k_k'][l], P['k_a'][l], P['r_k'][l], P['gn_w'][l], P['gn_b'][l], vres)
        o_s5, hr, hi = s5_mix(
            u_s5, g_s5, s5re0[l], s5im0[l], P['lam_re'][l], P['lam_im'][l], P['log_dt'][l],
            P['b_re'][l], P['b_im'][l], P['c_re'][l], P['c_im'][l], P['d_skip'][l], P['w_glu'][l])
        new_wkv.append(S)
        new_re.append(hr)
        new_im.append(hi)
        mix = jnp.concatenate([o_rwkv, o_s5], axis=-1).astype(x.dtype)
        x = x + mix @ P['w_out'][l]
        xc = rms_norm(x, P['norm_x'][l])
        x = x + cross_attend(xc, mem_k[l], mem_v[l], P['wq'][l], P['wo'][l])
    y = rms_norm(x, P['norm_f'])
    return y, jnp.stack(new_shift), jnp.stack(new_wkv), jnp.stack(new_re), jnp.stack(new_im)


def setup_inputs(seed: int = 0) -> dict:
    key = jax.random.key(seed)
    ks = iter(jax.random.split(key, 64))
    nrm = lambda shape, scale: jax.random.normal(next(ks), shape, F32) * scale
    L, D, R, G, N, C = DEPTH, D_MODEL, RWKV_WIDTH, S5_GROUPS, S5_STATE, S5_GROUP
    return {
        'x_prompt': nrm((BATCH, SEQ, D), 1.0),
        'x_sample': nrm((DEC_BATCH, DEC_SEQ, D), 1.0),
        'state_shift': nrm((L, DEC_BATCH, SHIFT_COLS), 1.0),
        'state_wkv': nrm((L, DEC_BATCH, RWKV_HEADS, RWKV_HEAD, RWKV_HEAD), 0.5),
        'state_s5_re': nrm((L, DEC_BATCH, G, N), 1.0),
        'state_s5_im': nrm((L, DEC_BATCH, G, N), 1.0),
        'cache_mem_k': nrm((L, DEC_BATCH, MEM_LEN, X_HEADS, X_HEAD_DIM), 1.0),
        'cache_mem_v': nrm((L, DEC_BATCH, MEM_LEN, X_HEADS, X_HEAD_DIM), 1.0),
        'mem_prompt': nrm((BATCH, MEM_LEN, D), 1.0),
        'norm_mix': 1.0 + nrm((L, D), 0.02),
        'w_in': nrm((L, D, IN_COLS), D ** -0.5),
        'mu_shift': jax.random.uniform(next(ks), (L, SHIFT_COLS), F32),
        'w0': nrm((L, R), 0.5),
        'w2': nrm((L, LORA_W, R), 0.1 * LORA_W ** -0.5),
        'a0': nrm((L, R), 0.1),
        'a2': nrm((L, LORA_A, R), 0.1 * LORA_A ** -0.5),
        'v0': nrm((L - 1, R), 0.1),
        'v1': nrm((L - 1, R, LORA_V), R ** -0.5),
        'v2': nrm((L - 1, LORA_V, R), 0.1 * LORA_V ** -0.5),
        'k_k': 0.85 + nrm((L, R), 0.02),
        'k_a': 1.0 + nrm((L, R), 0.02),
        'r_k': nrm((L, RWKV_HEADS, RWKV_HEAD), 0.1),
        'gn_w': 1.0 + nrm((L, R), 0.02),
        'gn_b': nrm((L, R), 0.02),
        'lam_re': -0.5 + nrm((L, G, N), 0.01),
        'lam_im': jnp.pi * jnp.arange(N, dtype=F32)[None, None, :] + nrm((L, G, N), 0.01),
        'log_dt': jax.random.uniform(next(ks), (L, G), F32, math.log(0.001), math.log(0.1)),
        'b_re': nrm((L, G, N, C), (2 * C) ** -0.5),
        'b_im': nrm((L, G, N, C), (2 * C) ** -0.5),
        'c_re': nrm((L, G, C, N), (2 * N) ** -0.5),
        'c_im': nrm((L, G, C, N), (2 * N) ** -0.5),
        'd_skip': nrm((L, S5_WIDTH), 1.0),
        'w_glu': nrm((L, S5_WIDTH, S5_WIDTH), S5_WIDTH ** -0.5),
        'w_out': nrm((L, MIX_WIDTH, D), MIX_WIDTH ** -0.5),
        'norm_x': 1.0 + nrm((L, D), 0.02),
        'norm_mem': 1.0 + nrm((L, D), 0.02),
        'wq': nrm((L, D, D), D ** -0.5),
        'wk': nrm((L, D, D), D ** -0.5),
        'wv': nrm((L, D, D), D ** -0.5),
        'wo': nrm((L, D, D), D ** -0.5),
        'norm_f': 1.0 + nrm((D,), 0.02),
    }


def reference(x_prompt, x_sample, state_shift, state_wkv, state_s5_re, state_s5_im,
              cache_mem_k, cache_mem_v, mem_prompt, norm_mix, w_in, mu_shift, w0, w2, a0, a2,
              v0, v1, v2, k_k, k_a, r_k, gn_w, gn_b, lam_re, lam_im, log_dt, b_re, b_im,
              c_re, c_im, d_skip, w_glu, w_out, norm_x, norm_mem, wq, wk, wv, wo, norm_f):
    P = dict(norm_mix=norm_mix, w_in=w_in, mu_shift=mu_shift, w0=w0, w2=w2, a0=a0, a2=a2,
             v0=v0, v1=v1, v2=v2, k_k=k_k, k_a=k_a, r_k=r_k, gn_w=gn_w, gn_b=gn_b,
             lam_re=lam_re, lam_im=lam_im, log_dt=log_dt, b_re=b_re, b_im=b_im,
             c_re=c_re, c_im=c_im, d_skip=d_skip, w_glu=w_glu, w_out=w_out,
             norm_x=norm_x, wq=wq, wo=wo, norm_f=norm_f)
    Bp, Mp = mem_prompt.shape[0], mem_prompt.shape[1]
    mks, mvs = [], []
    for l in range(DEPTH):
        mn = rms_norm(mem_prompt, norm_mem[l])
        mks.append((mn @ wk[l]).reshape(Bp, Mp, X_HEADS, X_HEAD_DIM))
        mvs.append((mn @ wv[l]).reshape(Bp, Mp, X_HEADS, X_HEAD_DIM))
    p_mem_k = jnp.stack(mks)
    p_mem_v = jnp.stack(mvs)
    z_shift = jnp.zeros((DEPTH, Bp, SHIFT_COLS), x_prompt.dtype)
    z_wkv = jnp.zeros((DEPTH, Bp, RWKV_HEADS, RWKV_HEAD, RWKV_HEAD), F32)
    z_s5 = jnp.zeros((DEPTH, Bp, S5_GROUPS, S5_STATE), F32)
    y_prompt, p_shift, p_wkv, p_s5_re, p_s5_im = run_trunk(
        x_prompt, z_shift, z_wkv, z_s5, z_s5, p_mem_k, p_mem_v, P)
    y_sample, s_shift, s_wkv, s_s5_re, s_s5_im = run_trunk(
        x_sample, state_shift, state_wkv, state_s5_re, state_s5_im, cache_mem_k, cache_mem_v, P)
    return (y_prompt, y_sample, p_shift, p_wkv, p_s5_re, p_s5_im, p_mem_k, p_mem_v,
            s_shift, s_wkv, s_s5_re, s_s5_im)
```

```python
import functools
import math

import jax
import jax.numpy as jnp
from jax import lax
from jax.experimental import pallas as pl
from jax.experimental.pallas import tpu as pltpu

F32 = jnp.float32
BF16 = jnp.bfloat16

D_MODEL = 1024
DEPTH = 4
RWKV_WIDTH = 512
RWKV_HEAD = 64
RWKV_HEADS = 8
LORA_W = 64
LORA_A = 64
LORA_V = 32
S5_WIDTH = 512
S5_GROUP = 16
S5_GROUPS = 32
S5_STATE = 64
S5_LANES = S5_GROUPS * S5_STATE
MEM_LEN = 256
X_HEADS = 4
X_HEAD_DIM = 256
SHIFT_COLS = 3 * RWKV_WIDTH + LORA_W + LORA_A
IN_COLS = SHIFT_COLS + RWKV_WIDTH + 2 * S5_WIDTH
NORM_EPS = 1e-6
GN_EPS = 64e-5

LANE = 128
WKV_CHUNK = 64
S5_TILES = S5_WIDTH // LANE
GROUPS_PER_TILE = LANE // S5_GROUP
TILE_STATES = GROUPS_PER_TILE * S5_STATE
S5_SCAN_ROWS = 8
VMEM_LIMIT = 56 * 1024 * 1024


def _cparams(sem):
    return pltpu.CompilerParams(dimension_semantics=sem, vmem_limit_bytes=VMEM_LIMIT)


def _mm(a, b):
    return jnp.dot(a.astype(BF16), b.astype(BF16), preferred_element_type=F32)


def _mm_nt(a, b):
    return lax.dot_general(a.astype(BF16), b.astype(BF16), (((1,), (1,)), ((), ())),
                           preferred_element_type=F32)


def _mm_tn(a, b):
    return lax.dot_general(a.astype(BF16), b.astype(BF16), (((0,), (0,)), ((), ())),
                           preferred_element_type=F32)


def _mm_f32(a, b):
    return jnp.dot(a, b, preferred_element_type=F32, precision=lax.Precision.HIGHEST)


def _mm_nt_f32(a, b):
    return lax.dot_general(a, b, (((1,), (1,)), ((), ())), preferred_element_type=F32,
                           precision=lax.Precision.HIGHEST)


def _rms(x, g):
    return x * lax.rsqrt(jnp.mean(x * x, axis=-1, keepdims=True) + NORM_EPS) * g


def _sigmoid(x):
    return 1.0 / (1.0 + jnp.exp(-x))


def _silu(x):
    return x * _sigmoid(x)


def _iota(shape, dim):
    return lax.broadcasted_iota(jnp.int32, shape, dim)


def _memkv_kernel(m_ref, g_ref, wk_ref, wv_ref, k_ref, v_ref):
    mn = _rms(m_ref[...], g_ref[0]).astype(BF16)
    k_ref[0] = jnp.dot(mn, wk_ref[0], preferred_element_type=F32)
    v_ref[0] = jnp.dot(mn, wv_ref[0], preferred_element_type=F32)


def _memkv(mem2d, norm_mem, wk_b, wv_b):
    rows = mem2d.shape[0]
    tm = 512
    out = jax.ShapeDtypeStruct((DEPTH, rows, D_MODEL), F32)
    return pl.pallas_call(
        _memkv_kernel,
        grid=(DEPTH, rows // tm),
        in_specs=[
            pl.BlockSpec((tm, D_MODEL), lambda l, i: (i, 0)),
            pl.BlockSpec((1, 1, D_MODEL), lambda l, i: (l, 0, 0)),
            pl.BlockSpec((1, D_MODEL, D_MODEL), lambda l, i: (l, 0, 0)),
            pl.BlockSpec((1, D_MODEL, D_MODEL), lambda l, i: (l, 0, 0)),
        ],
        out_specs=[
            pl.BlockSpec((1, tm, D_MODEL), lambda l, i: (l, i, 0)),
            pl.BlockSpec((1, tm, D_MODEL), lambda l, i: (l, i, 0)),
        ],
        out_shape=[out, out],
        compiler_params=_cparams(("arbitrary", "arbitrary")),
        name="memkv",
    )(mem2d, norm_mem.reshape(DEPTH, 1, D_MODEL), wk_b, wv_b)


_IN_SPLITS = (0, SHIFT_COLS, SHIFT_COLS + RWKV_WIDTH, SHIFT_COLS + RWKV_WIDTH + S5_WIDTH, IN_COLS)


def _inproj_kernel(x_ref, g_ref, w_ref, psh_ref, grw_ref, us5_ref, gs5_ref):
    xn = _rms(x_ref[...], g_ref[0]).astype(BF16)
    proj = lambda lo, hi: jnp.dot(xn, w_ref[0, :, lo:hi], preferred_element_type=F32)
    psh_ref[...] = proj(_IN_SPLITS[0], _IN_SPLITS[1])
    grw_ref[...] = proj(_IN_SPLITS[1], _IN_SPLITS[2])
    for o_ref, lo in ((us5_ref, _IN_SPLITS[2]), (gs5_ref, _IN_SPLITS[3])):
        for j in range(S5_TILES):
            o_ref[j] = proj(lo + j * LANE, lo + (j + 1) * LANE)


def _inproj(x2d, norm_mix3, w_in_b, l):
    rows = x2d.shape[0]
    tm = 256
    s5_spec = pl.BlockSpec((S5_TILES, tm, LANE), lambda i: (0, i, 0))
    s5_shape = jax.ShapeDtypeStruct((S5_TILES, rows, LANE), F32)
    return pl.pallas_call(
        _inproj_kernel,
        grid=(rows // tm,),
        in_specs=[
            pl.BlockSpec((tm, D_MODEL), lambda i: (i, 0)),
            pl.BlockSpec((1, 1, D_MODEL), lambda i: (l, 0, 0)),
            pl.BlockSpec((1, D_MODEL, IN_COLS), lambda i: (l, 0, 0)),
        ],
        out_specs=[pl.BlockSpec((tm, SHIFT_COLS), lambda i: (i, 0)),
                   pl.BlockSpec((tm, RWKV_WIDTH), lambda i: (i, 0)), s5_spec, s5_spec],
        out_shape=[jax.ShapeDtypeStruct((rows, SHIFT_COLS), F32),
                   jax.ShapeDtypeStruct((rows, RWKV_WIDTH), F32), s5_shape, s5_shape],
        compiler_params=_cparams(("arbitrary",)),
        name="inproj",
    )(x2d, norm_mix3, w_in_b)


_V_W0, _V_A0, _V_KK, _V_KA, _V_GNW, _V_GNB, _V_RK, _V_V0 = range(8)


def _rwkv_prep(ps, vfirst, vecs, wla, v1, v2):
    r = ps[:, 0:RWKV_WIDTH]
    k = ps[:, RWKV_WIDTH:2 * RWKV_WIDTH]
    v = ps[:, 2 * RWKV_WIDTH:3 * RWKV_WIDTH]
    lora_in = ps[:, 3 * RWKV_WIDTH:SHIFT_COLS]
    lane = _iota(lora_in.shape, 1)
    lora_in = jnp.where(lane < LORA_W, jnp.tanh(lora_in), lora_in)
    lora = _mm(lora_in, wla)
    wpre = vecs[_V_W0:_V_W0 + 1] + lora[:, 0:RWKV_WIDTH]
    z = -wpre
    softplus = jnp.maximum(z, 0.0) + jnp.log1p(jnp.exp(-jnp.abs(z)))
    logdecay = -jnp.exp(-softplus - 0.5)
    a = _sigmoid(vecs[_V_A0:_V_A0 + 1] + lora[:, RWKV_WIDTH:2 * RWKV_WIDTH])
    if vfirst is not None:
        gate = _sigmoid(vecs[_V_V0:_V_V0 + 1] + _mm(_mm(v, v1), v2))
        v = v + (vfirst - v) * gate
    kku = k * vecs[_V_KK:_V_KK + 1]
    kmod = k * (1.0 + (a - 1.0) * vecs[_V_KA:_V_KA + 1])
    return r, kku, kmod, v, a, logdecay


def _wkv_chunk(lw, load_head, states, seg_len, vecs):
    C = WKV_CHUNK
    nseg = C // seg_len
    shift = int(math.log2(seg_len))
    row = _iota((C, C), 0)
    col = _iota((C, C), 1)
    same = (row >> shift) == (col >> shift)
    incl = jnp.logical_and(col <= row, same)
    strict = jnp.logical_and(col < row, same)
    cum = _mm_f32(incl.astype(F32), lw)
    tot = _mm_f32(same.astype(F32), lw)
    g_inc_all = jnp.exp(cum)
    g_exc_all = jnp.exp(cum - lw)
    g_inv_all = jnp.exp(-cum)
    g_end_all = jnp.exp(tot - cum)
    g_tot_all = jnp.exp(tot)
    eye = (row == col).astype(F32)
    nsq = max(shift - 1, 0)
    rowv = _iota((C, 1), 0)
    row2 = _iota((2 * C, 1), 0)

    ys = []
    new_states = []
    for h in range(RWKV_HEADS):
        sl = slice(h * RWKV_HEAD, (h + 1) * RWKV_HEAD)
        r, kku, kmod, v, a = load_head(h)
        ss = jnp.sum(kku * kku, axis=-1, keepdims=True)
        kkn = kku * lax.rsqrt(jnp.maximum(ss, 1e-24))
        b = kkn * a
        g_inv = g_inv_all[:, sl]
        g_end = g_end_all[:, sl]
        qk = kkn * g_exc_all[:, sl]
        qr = r * g_inc_all[:, sl]
        q2 = jnp.concatenate([qk, qr], axis=0).astype(BF16)
        ak = _mm_nt(q2, kmod * g_inv)
        ab = _mm_nt(q2, b * g_inv)
        akk = jnp.where(strict, ak[:C], 0.0)
        ark = jnp.where(incl, ak[C:], 0.0)
        abk = jnp.where(strict, ab[:C], 0.0)
        arb = jnp.where(incl, ab[C:], 0.0)
        p = -abk
        tinv = eye + p
        for _ in range(nsq):
            p = _mm(p, p)
            tinv = tinv + _mm(tinv, p)
        if nseg == 1:
            qs = _mm_nt(q2, states[h][0])
        else:
            qs = jnp.zeros((2 * C, RWKV_HEAD), F32)
            for s in range(nseg):
                m2 = ((row2 & (C - 1)) >> shift) == s
                qs = qs + _mm_nt(jnp.where(m2, q2, jnp.zeros_like(q2)), states[h][s])
        u = _mm(tinv, qs[:C] + _mm(akk, v))
        y = qs[C:] + _mm(ark, v) - _mm(arb, u)
        kc = kmod * g_end
        bc = b * g_end
        hs = []
        for s in range(nseg):
            if nseg == 1:
                vm, um = v, u
            else:
                m = (rowv >> shift) == s
                vm = jnp.where(m, v, 0.0)
                um = jnp.where(m, u, 0.0)
            gt = g_tot_all[s * seg_len:s * seg_len + 1, sl]
            hs.append(states[h][s] * gt + _mm_tn(vm, kc) - _mm_tn(um, bc))
        new_states.append(hs)
        mu = jnp.mean(y, axis=-1, keepdims=True)
        yc = y - mu
        var = jnp.mean(yc * yc, axis=-1, keepdims=True)
        yn = yc * lax.rsqrt(var + GN_EPS) * vecs[_V_GNW:_V_GNW + 1, sl] + vecs[_V_GNB:_V_GNB + 1, sl]
        bonus = jnp.sum(r * kmod * vecs[_V_RK:_V_RK + 1, sl], axis=-1, keepdims=True)
        ys.append(yn + bonus * v)
    return jnp.concatenate(ys, axis=-1), new_states


def _rwkv_prompt_kernel(*refs, first_layer, tt):
    if first_layer:
        (psh_ref, grw_ref, mu_ref, vecs_ref, wla_ref, v1_ref, v2_ref,
         o_ref, vf_out_ref, shift_out_ref, wkv_out_ref,
         carry_ref, s_ref, r_s, kku_s, kmod_s, v_s, a_s, lw_s, y_s) = refs
        vf_ref = None
    else:
        (psh_ref, grw_ref, vf_ref, mu_ref, vecs_ref, wla_ref, v1_ref, v2_ref,
         o_ref, shift_out_ref, wkv_out_ref,
         carry_ref, s_ref, r_s, kku_s, kmod_s, v_s, a_s, lw_s, y_s) = refs
        vf_out_ref = None
    t = pl.program_id(1)

    @pl.when(t == 0)
    def _():
        carry_ref[...] = jnp.zeros_like(carry_ref)
        s_ref[...] = jnp.zeros_like(s_ref)

    p = psh_ref[0]
    rowi = _iota(p.shape, 0)
    prev = jnp.where(rowi == 0, carry_ref[...], pltpu.roll(p, 1, 0))
    ps = p + (prev - p) * mu_ref[0]
    last = p[tt - 1:tt, :]
    carry_ref[...] = last
    shift_out_ref[0] = last

    vecs = vecs_ref[0]
    vfirst = None if first_layer else vf_ref[0]
    r, kku, kmod, v, a, lw = _rwkv_prep(ps, vfirst, vecs, wla_ref[0], v1_ref[0], v2_ref[0])
    if first_layer:
        vf_out_ref[0] = v
    r_s[...] = r
    kku_s[...] = kku
    kmod_s[...] = kmod
    v_s[...] = v
    a_s[...] = a
    lw_s[...] = lw

    def chunk(c, carry):
        rows = pl.ds(pl.multiple_of(c * WKV_CHUNK, WKV_CHUNK), WKV_CHUNK)

        def load_head(h):
            sl = slice(h * RWKV_HEAD, (h + 1) * RWKV_HEAD)
            return (r_s[rows, sl], kku_s[rows, sl], kmod_s[rows, sl], v_s[rows, sl], a_s[rows, sl])

        states = [[s_ref[h]] for h in range(RWKV_HEADS)]
        y, new_states = _wkv_chunk(lw_s[rows, :], load_head, states, WKV_CHUNK, vecs)
        for h in range(RWKV_HEADS):
            s_ref[h] = new_states[h][0]
        y_s[rows, :] = y
        return carry

    lax.fori_loop(0, tt // WKV_CHUNK, chunk, 0)
    o_ref[0] = y_s[...] * _silu(grw_ref[0])
    wkv_out_ref[0] = s_ref[...]


def _rwkv_prompt(psh, grw, vfirst, mu3, vecs, wla, v1, v2, l, batch, seq):
    tt = 512
    first = vfirst is None
    lv = max(l - 1, 0)
    tok = lambda w: pl.BlockSpec((1, tt, w), lambda b, t: (b, t, 0))
    in_specs = [tok(SHIFT_COLS), tok(RWKV_WIDTH)]
    args = [psh.reshape(batch, seq, SHIFT_COLS), grw.reshape(batch, seq, RWKV_WIDTH)]
    if not first:
        in_specs.append(tok(RWKV_WIDTH))
        args.append(vfirst.reshape(batch, seq, RWKV_WIDTH))
    in_specs += [
        pl.BlockSpec((1, 1, SHIFT_COLS), lambda b, t: (l, 0, 0)),
        pl.BlockSpec((1, 8, RWKV_WIDTH), lambda b, t: (l, 0, 0)),
        pl.BlockSpec((1, LORA_W + LORA_A, 2 * RWKV_WIDTH), lambda b, t: (l, 0, 0)),
        pl.BlockSpec((1, RWKV_WIDTH, LORA_V), lambda b, t: (lv, 0, 0)),
        pl.BlockSpec((1, LORA_V, RWKV_WIDTH), lambda b, t: (lv, 0, 0)),
    ]
    args += [mu3, vecs, wla, v1, v2]
    out_specs = [tok(RWKV_WIDTH)]
    out_shape = [jax.ShapeDtypeStruct((batch, seq, RWKV_WIDTH), F32)]
    if first:
        out_specs.append(tok(RWKV_WIDTH))
        out_shape.append(jax.ShapeDtypeStruct((batch, seq, RWKV_WIDTH), F32))
    out_specs += [
        pl.BlockSpec((1, 1, SHIFT_COLS), lambda b, t: (b, 0, 0)),
        pl.BlockSpec((1, RWKV_HEADS, RWKV_HEAD, RWKV_HEAD), lambda b, t: (b, 0, 0, 0)),
    ]
    out_shape += [
        jax.ShapeDtypeStruct((batch, 1, SHIFT_COLS), F32),
        jax.ShapeDtypeStruct((batch, RWKV_HEADS, RWKV_HEAD, RWKV_HEAD), F32),
    ]
    scratch = [
        pltpu.VMEM((1, SHIFT_COLS), F32),
        pltpu.VMEM((RWKV_HEADS, RWKV_HEAD, RWKV_HEAD), F32),
    ] + [pltpu.VMEM((tt, RWKV_WIDTH), F32) for _ in range(7)]
    outs = pl.pallas_call(
        functools.partial(_rwkv_prompt_kernel, first_layer=first, tt=tt),
        grid=(batch, seq // tt),
        in_specs=in_specs,
        out_specs=out_specs,
        out_shape=out_shape,
        scratch_shapes=scratch,
        compiler_params=_cparams(("arbitrary", "arbitrary")),
        name="rwkv_prompt",
    )(*args)
    if first:
        o, vf, sh, wkv = outs
    else:
        (o, sh, wkv), vf = outs, vfirst
    return o.reshape(batch * seq, RWKV_WIDTH), vf.reshape(batch * seq, RWKV_WIDTH), sh[:, 0], wkv


def _rwkv_sample_kernel(*refs, first_layer, seq):
    if first_layer:
        (psh_ref, grw_ref, prev_ref, wkv_ref, mu_ref, vecs_ref, wla_ref, v1_ref, v2_ref,
         o_ref, vf_out_ref, wkv_out_ref) = refs
        vf_ref = None
    else:
        (psh_ref, grw_ref, vf_ref, prev_ref, wkv_ref, mu_ref, vecs_ref, wla_ref, v1_ref, v2_ref,
         o_ref, wkv_out_ref) = refs
        vf_out_ref = None
    nseq = WKV_CHUNK // seq
    p = psh_ref[...]
    rowi = _iota(p.shape, 0)
    prev = jnp.where((rowi & (seq - 1)) == 0, prev_ref[...], pltpu.roll(p, 1, 0))
    ps = p + (prev - p) * mu_ref[0]

    vecs = vecs_ref[0]
    vfirst = None if first_layer else vf_ref[...]
    r, kku, kmod, v, a, lw = _rwkv_prep(ps, vfirst, vecs, wla_ref[0], v1_ref[0], v2_ref[0])
    if first_layer:
        vf_out_ref[...] = v

    def load_head(h):
        sl = slice(h * RWKV_HEAD, (h + 1) * RWKV_HEAD)
        return (r[:, sl], kku[:, sl], kmod[:, sl], v[:, sl], a[:, sl])

    states = [[wkv_ref[0, s, h] for s in range(nseq)] for h in range(RWKV_HEADS)]
    y, new_states = _wkv_chunk(lw, load_head, states, seq, vecs)
    for h in range(RWKV_HEADS):
        for s in range(nseq):
            wkv_out_ref[s, h] = new_states[h][s]
    o_ref[...] = y * _silu(grw_ref[...])


def _rwkv_sample(psh, grw, vfirst, prev_rows, wkv0, mu3, vecs, wla, v1, v2, l, seq):
    rows = psh.shape[0]
    nseq = WKV_CHUNK // seq
    first = vfirst is None
    lv = max(l - 1, 0)
    tok = lambda w: pl.BlockSpec((WKV_CHUNK, w), lambda i: (i, 0))
    st_in = pl.BlockSpec((1, nseq, RWKV_HEADS, RWKV_HEAD, RWKV_HEAD), lambda i: (l, i, 0, 0, 0))
    st = pl.BlockSpec((nseq, RWKV_HEADS, RWKV_HEAD, RWKV_HEAD), lambda i: (i, 0, 0, 0))
    in_specs = [tok(SHIFT_COLS), tok(RWKV_WIDTH)]
    args = [psh, grw]
    if not first:
        in_specs.append(tok(RWKV_WIDTH))
        args.append(vfirst)
    in_specs += [
        tok(SHIFT_COLS), st_in,
        pl.BlockSpec((1, 1, SHIFT_COLS), lambda i: (l, 0, 0)),
        pl.BlockSpec((1, 8, RWKV_WIDTH), lambda i: (l, 0, 0)),
        pl.BlockSpec((1, LORA_W + LORA_A, 2 * RWKV_WIDTH), lambda i: (l, 0, 0)),
        pl.BlockSpec((1, RWKV_WIDTH, LORA_V), lambda i: (lv, 0, 0)),
        pl.BlockSpec((1, LORA_V, RWKV_WIDTH), lambda i: (lv, 0, 0)),
    ]
    args += [prev_rows, wkv0, mu3, vecs, wla, v1, v2]
    out_specs = [tok(RWKV_WIDTH)]
    out_shape = [jax.ShapeDtypeStruct((rows, RWKV_WIDTH), F32)]
    if first:
        out_specs.append(tok(RWKV_WIDTH))
        out_shape.append(jax.ShapeDtypeStruct((rows, RWKV_WIDTH), F32))
    out_specs.append(st)
    out_shape.append(jax.ShapeDtypeStruct(wkv0.shape[1:], F32))
    outs = pl.pallas_call(
        functools.partial(_rwkv_sample_kernel, first_layer=first, seq=seq),
        grid=(rows // WKV_CHUNK,),
        in_specs=in_specs,
        out_specs=out_specs,
        out_shape=out_shape,
        compiler_params=_cparams(("arbitrary",)),
        name="rwkv_sample",
    )(*args)
    if first:
        o, vf, wkv = outs
    else:
        (o, wkv), vf = outs, vfirst
    sh = psh.reshape(rows // seq, seq, SHIFT_COLS)[:, seq - 1]
    return o, vf, sh, wkv


def _s5_prep_kernel(lam_re_ref, lam_im_ref, ldt_ref, bt_re_ref, bt_im_ref, ct_re_ref, ct_im_ref,
                    mt_ref, wz_re_ref, wz_im_ref, wc_re_ref, wc_im_ref, ap_re_ref, ap_im_ref, *, cs):
    n = TILE_STATES
    row_g = _iota((LANE, n), 0) >> 4
    lane_g = _iota((LANE, n), 1) >> 6
    emb_mask = row_g == lane_g
    bd_mask = (_iota((LANE, LANE), 0) >> 4) == (_iota((LANE, LANE), 1) >> 4)

    def rep(x):
        return jnp.concatenate([x] * GROUPS_PER_TILE, axis=0)

    def embed(x):
        return jnp.where(emb_mask, rep(x), 0.0)

    for j in range(S5_TILES):
        ls = slice(j * n, (j + 1) * n)
        lr = lam_re_ref[0, :, ls]
        li = lam_im_ref[0, :, ls]
        dt = jnp.exp(ldt_ref[0, :, ls])
        mag = jnp.exp(lr * dt)
        a_re = mag * jnp.cos(li * dt)
        a_im = mag * jnp.sin(li * dt)
        q_re = a_re - 1.0
        q_im = a_im
        den = lr * lr + li * li
        f_re = (q_re * lr + q_im * li) / den
        f_im = (q_im * lr - q_re * li) / den
        b_re = bt_re_ref[0, :, ls]
        b_im = bt_im_ref[0, :, ls]
        bb_re = f_re * b_re - f_im * b_im
        bb_im = f_re * b_im + f_im * b_re
        c_re = ct_re_ref[0, :, ls]
        c_im = ct_im_ref[0, :, ls]
        pw = [(jnp.ones_like(a_re), jnp.zeros_like(a_re))]
        for _ in range(cs):
            pr, pi = pw[-1]
            pw.append((pr * a_re - pi * a_im, pr * a_im + pi * a_re))
        e = [(pr * bb_re - pi * bb_im, pr * bb_im + pi * bb_re) for pr, pi in pw[:cs]]
        wz_re_ref[0, j] = jnp.concatenate([embed(e[cs - 1 - s][0]) for s in range(cs)], axis=0).astype(BF16)
        wz_im_ref[0, j] = jnp.concatenate([embed(e[cs - 1 - s][1]) for s in range(cs)], axis=0).astype(BF16)
        wc_re_ref[0, j] = jnp.concatenate(
            [embed(c_re * pw[s + 1][0] - c_im * pw[s + 1][1]) for s in range(cs)], axis=0).astype(BF16)
        wc_im_ref[0, j] = jnp.concatenate(
            [embed(-(c_re * pw[s + 1][1] + c_im * pw[s + 1][0])) for s in range(cs)], axis=0).astype(BF16)
        ce_re = embed(c_re)
        ce_im = embed(c_im)
        d = [jnp.where(bd_mask, _mm_nt_f32(ce_re, rep(er)) - _mm_nt_f32(ce_im, rep(ei)), 0.0) for er, ei in e]
        zero = jnp.zeros((LANE, LANE), F32)
        mt_ref[0, j] = jnp.concatenate(
            [jnp.concatenate([d[s - sp] if sp <= s else zero for sp in range(cs)], axis=1) for s in range(cs)],
            axis=0).astype(BF16)
        pr, pi = pw[cs]
        for k in range(S5_SCAN_ROWS):
            ap_re_ref[0, k:k + 1, ls] = pr
            ap_im_ref[0, k:k + 1, ls] = pi
            pr, pi = pr * pr - pi * pi, 2.0 * pr * pi


def _s5_prep(lam_re, lam_im, ldt_rep, bt_re, bt_im, ct_re, ct_im, cs):
    vec = pl.BlockSpec((1, 1, S5_LANES), lambda l: (l, 0, 0))
    mat = pl.BlockSpec((1, S5_GROUP, S5_LANES), lambda l: (l, 0, 0))
    k = cs * LANE
    w_spec = pl.BlockSpec((1, S5_TILES, k, TILE_STATES), lambda l: (l, 0, 0, 0))
    w_shape = jax.ShapeDtypeStruct((DEPTH, S5_TILES, k, TILE_STATES), BF16)
    ap_spec = pl.BlockSpec((1, S5_SCAN_ROWS, S5_LANES), lambda l: (l, 0, 0))
    ap_shape = jax.ShapeDtypeStruct((DEPTH, S5_SCAN_ROWS, S5_LANES), F32)
    return pl.pallas_call(
        functools.partial(_s5_prep_kernel, cs=cs),
        grid=(DEPTH,),
        in_specs=[vec, vec, vec, mat, mat, mat, mat],
        out_specs=[pl.BlockSpec((1, S5_TILES, k, k), lambda l: (l, 0, 0, 0)),
                   w_spec, w_spec, w_spec, w_spec, ap_spec, ap_spec],
        out_shape=[jax.ShapeDtypeStruct((DEPTH, S5_TILES, k, k), BF16),
                   w_shape, w_shape, w_shape, w_shape, ap_shape, ap_shape],
        compiler_params=_cparams(("arbitrary",)),
        name="s5_prep",
    )(lam_re, lam_im, ldt_rep, bt_re, bt_im, ct_re, ct_im)


def _gelu(x):
    return 0.5 * x * (1.0 + lax.erf(x * (1.0 / math.sqrt(2.0))))


def _s5_kernel(*refs, cs, rows, chain):
    if chain:
        (u_ref, g_ref, mt_ref, wz_re_ref, wz_im_ref, wc_re_ref, wc_im_ref, ap_re_ref, ap_im_ref,
         dsk_ref, wglu_ref, o_ref, hr_out_ref, hi_out_ref, hc_r_ref, hc_i_ref) = refs

        @pl.when(pl.program_id(1) == 0)
        def _():
            hc_r_ref[...] = jnp.zeros_like(hc_r_ref)
            hc_i_ref[...] = jnp.zeros_like(hc_i_ref)
    else:
        (u_ref, g_ref, h0r_ref, h0i_ref, mt_ref, wz_re_ref, wz_im_ref, wc_re_ref, wc_im_ref,
         ap_re_ref, ap_im_ref, dsk_ref, wglu_ref, o_ref, hr_out_ref, hi_out_ref) = refs
    n = TILE_STATES
    us = [[u_ref[j, pl.ds(s, rows, stride=cs), :] for s in range(cs)] for j in range(S5_TILES)]
    x8 = [jnp.concatenate(us[j], axis=1).astype(BF16) for j in range(S5_TILES)]
    zr = jnp.concatenate([jnp.dot(x8[j], wz_re_ref[0, j], preferred_element_type=F32)
                          for j in range(S5_TILES)], axis=1)
    zi = jnp.concatenate([jnp.dot(x8[j], wz_im_ref[0, j], preferred_element_type=F32)
                          for j in range(S5_TILES)], axis=1)
    if chain:
        rowi = _iota(zr.shape, 0)
        cr = hc_r_ref[...]
        ci = hc_i_ref[...]
        a1r = ap_re_ref[0, 0:1, :]
        a1i = ap_im_ref[0, 0:1, :]
        hr = zr + jnp.where(rowi == 0, a1r * cr - a1i * ci, 0.0)
        hi = zi + jnp.where(rowi == 0, a1r * ci + a1i * cr, 0.0)
        d = 1
        k = 0
        while d < rows:
            ar = ap_re_ref[0, k:k + 1, :]
            ai = ap_im_ref[0, k:k + 1, :]
            sr = jnp.where(rowi >= d, pltpu.roll(hr, d, 0), 0.0)
            si = jnp.where(rowi >= d, pltpu.roll(hi, d, 0), 0.0)
            hr, hi = hr + ar * sr - ai * si, hi + ar * si + ai * sr
            d *= 2
            k += 1
        hin_r = jnp.where(rowi >= 1, pltpu.roll(hr, 1, 0), cr)
        hin_i = jnp.where(rowi >= 1, pltpu.roll(hi, 1, 0), ci)
        hc_r_ref[...] = hr[rows - 1:rows, :]
        hc_i_ref[...] = hi[rows - 1:rows, :]
        hr_out_ref[0] = hr[rows - 1:rows, :]
        hi_out_ref[0] = hi[rows - 1:rows, :]
    else:
        hin_r = h0r_ref[...]
        hin_i = h0i_ref[...]
        ar = ap_re_ref[0, 0:1, :]
        ai = ap_im_ref[0, 0:1, :]
        hr_out_ref[...] = ar * hin_r - ai * hin_i + zr
        hi_out_ref[...] = ar * hin_i + ai * hin_r + zi
    hin_r = hin_r.astype(BF16)
    hin_i = hin_i.astype(BF16)
    y8 = []
    for j in range(S5_TILES):
        ls = slice(j * n, (j + 1) * n)
        y8.append(_mm_nt(x8[j], mt_ref[0, j]) + _mm_nt(hin_r[:, ls], wc_re_ref[0, j])
                  + _mm_nt(hin_i[:, ls], wc_im_ref[0, j]))
    dsk = dsk_ref[0]
    for s in range(cs):
        y = jnp.concatenate([y8[j][:, s * LANE:(s + 1) * LANE] for j in range(S5_TILES)], axis=1)
        u_s = jnp.concatenate([us[j][s] for j in range(S5_TILES)], axis=1)
        y = _gelu(y + dsk * u_s)
        y = y * _sigmoid(jnp.dot(y.astype(BF16), wglu_ref[0], preferred_element_type=F32))
        gate = jnp.concatenate([g_ref[j, pl.ds(s, rows, stride=cs), :] for j in range(S5_TILES)], axis=1)
        out = y * _silu(gate)
        for j in range(S5_TILES):
            o_ref[j, pl.ds(s, rows, stride=cs), :] = out[:, j * LANE:(j + 1) * LANE]


def _s5_mix(u, g, h0, ops, dsk3, wglu_b, l, cs, nb):
    seq = u.shape[1] // nb
    chain = h0 is None
    ntok = min(seq, 1024) if chain else seq
    tiles = seq // ntok
    rows = ntok // cs
    mt, wz_re, wz_im, wc_re, wc_im, ap_re, ap_im = ops
    k = cs * LANE
    tok = pl.BlockSpec((S5_TILES, ntok, LANE), lambda b, t: (0, b * tiles + t, 0))
    w_spec = pl.BlockSpec((1, S5_TILES, k, TILE_STATES), lambda b, t: (l, 0, 0, 0))
    ap_spec = pl.BlockSpec((1, S5_SCAN_ROWS, S5_LANES), lambda b, t: (l, 0, 0))
    in_specs = [tok, tok]
    args = [u, g]
    scratch = []
    if not chain:
        hs = pl.BlockSpec((rows, S5_LANES), lambda b, t: (0, 0))
        in_specs += [hs, hs]
        args += [h0[0], h0[1]]
        st_spec = hs
        st_shape = jax.ShapeDtypeStruct((rows, S5_LANES), F32)
    else:
        st_spec = pl.BlockSpec((1, 1, S5_LANES), lambda b, t: (b, 0, 0))
        st_shape = jax.ShapeDtypeStruct((nb, 1, S5_LANES), F32)
        scratch = [pltpu.VMEM((1, S5_LANES), F32), pltpu.VMEM((1, S5_LANES), F32)]
    in_specs += [
        pl.BlockSpec((1, S5_TILES, k, k), lambda b, t: (l, 0, 0, 0)),
        w_spec, w_spec, w_spec, w_spec, ap_spec, ap_spec,
        pl.BlockSpec((1, 1, S5_WIDTH), lambda b, t: (l, 0, 0)),
        pl.BlockSpec((1, S5_WIDTH, S5_WIDTH), lambda b, t: (l, 0, 0)),
    ]
    args += [mt, wz_re, wz_im, wc_re, wc_im, ap_re, ap_im, dsk3, wglu_b]
    o, hr, hi = pl.pallas_call(
        functools.partial(_s5_kernel, cs=cs, rows=rows, chain=chain),
        grid=(nb, tiles),
        in_specs=in_specs,
        out_specs=[tok, st_spec, st_spec],
        out_shape=[jax.ShapeDtypeStruct(u.shape, F32), st_shape, st_shape],
        scratch_shapes=scratch,
        compiler_params=_cparams(("arbitrary", "arbitrary")),
        name="s5_mix",
    )(*args)
    return o, hr, hi


def _attend(q, get_k, get_v, nseg, row_seg):
    outs = []
    for h in range(X_HEADS):
        hs = slice(h * X_HEAD_DIM, (h + 1) * X_HEAD_DIM)
        qh = q[:, hs].astype(BF16)
        s = None
        for i in range(nseg):
            si = _mm_nt(qh, get_k(i, hs))
            s = si if s is None else jnp.where(row_seg == i, si, s)
        s = s - jnp.max(s, axis=-1, keepdims=True)
        e = jnp.exp(s)
        p = (e / jnp.sum(e, axis=-1, keepdims=True)).astype(BF16)
        o = None
        for i in range(nseg):
            oi = _mm(p, get_v(i, hs))
            o = oi if o is None else jnp.where(row_seg == i, oi, o)
        outs.append(o)
    return jnp.concatenate(outs, axis=-1)


def _post_kernel(x_ref, orw_ref, os5_ref, wout_ref, nx_ref, wq_ref, wo_ref, mk_ref, mv_ref, nf_ref,
                 o_ref, *, nseg, seg_rows, final):
    x = x_ref[...]
    os5 = jnp.concatenate([os5_ref[j] for j in range(S5_TILES)], axis=1)
    x1 = (x + jnp.dot(orw_ref[...].astype(BF16), wout_ref[0, 0:RWKV_WIDTH, :], preferred_element_type=F32)
          + jnp.dot(os5.astype(BF16), wout_ref[0, RWKV_WIDTH:, :], preferred_element_type=F32))
    xc = _rms(x1, nx_ref[0]).astype(BF16)
    q = jnp.dot(xc, wq_ref[0], preferred_element_type=F32) * (1.0 / math.sqrt(X_HEAD_DIM))
    row_seg = _iota((q.shape[0], 1), 0) >> int(math.log2(seg_rows))
    att = _attend(q, lambda i, hs: mk_ref[0, i, :, hs], lambda i, hs: mv_ref[0, i, :, hs], nseg, row_seg)
    x2 = x1 + jnp.dot(att.astype(BF16), wo_ref[0], preferred_element_type=F32)
    if final:
        x2 = _rms(x2, nf_ref[...])
    o_ref[...] = x2


def _post(x2d, orw, os5, wout_b, nx3, wq_b, wo_b, mk, mv, nf2, l, nb, nseg, seg_rows, tq, final):
    tiles = x2d.shape[0] // nb // tq
    tok = lambda w: pl.BlockSpec((tq, w), lambda b, t: (b * tiles + t, 0))
    tok_s5 = pl.BlockSpec((S5_TILES, tq, LANE), lambda b, t: (0, b * tiles + t, 0))
    if nseg == 1:
        mem = pl.BlockSpec((1, 1, MEM_LEN, D_MODEL), lambda b, t: (l, b, 0, 0))
    else:
        mem = pl.BlockSpec((1, nseg, MEM_LEN, D_MODEL), lambda b, t: (l, t, 0, 0))
    wsp = pl.BlockSpec((1, D_MODEL, D_MODEL), lambda b, t: (l, 0, 0))
    return pl.pallas_call(
        functools.partial(_post_kernel, nseg=nseg, seg_rows=seg_rows, final=final),
        grid=(nb, tiles),
        in_specs=[tok(D_MODEL), tok(RWKV_WIDTH), tok_s5, wsp,
                  pl.BlockSpec((1, 1, D_MODEL), lambda b, t: (l, 0, 0)), wsp, wsp, mem, mem,
                  pl.BlockSpec((1, D_MODEL), lambda b, t: (0, 0))],
        out_specs=tok(D_MODEL),
        out_shape=jax.ShapeDtypeStruct(x2d.shape, F32),
        compiler_params=_cparams(("arbitrary", "arbitrary")),
        name="post",
    )(x2d, orw, os5, wout_b, nx3, wq_b, wo_b, mk, mv, nf2)


def _trunk(x, states, mem_k, mem_v, W):
    batch, seq, _ = x.shape
    prompt = states is None
    x2d = x.reshape(batch * seq, D_MODEL)
    new_shift, new_wkv, new_re, new_im = [], [], [], []
    vfirst = None
    for l in range(DEPTH):
        psh, grw, us5, gs5 = _inproj(x2d, W['norm_mix'], W['w_in'], l)
        rw_args = (W['mu'], W['vecs'], W['wla'], W['v1'], W['v2'], l)
        if prompt:
            orw, vfirst, sh, wkv = _rwkv_prompt(psh, grw, vfirst, *rw_args, batch, seq)
            os5, hr, hi = _s5_mix(us5, gs5, None, W['s5_prompt'], W['d_skip'], W['w_glu'], l, 8, batch)
            hr = hr.reshape(batch, S5_GROUPS, S5_STATE)
            hi = hi.reshape(batch, S5_GROUPS, S5_STATE)
            x2d = _post(x2d, orw, os5, W['w_out'], W['norm_x'], W['wq'], W['wo'], mem_k, mem_v,
                        W['norm_f'], l, batch, 1, seq, 512, l == DEPTH - 1)
        else:
            shift0, wkv0, re0, im0 = states
            prev_rows = jnp.repeat(shift0[l], seq, axis=0)
            orw, vfirst, sh, wkv = _rwkv_sample(psh, grw, vfirst, prev_rows, wkv0, *rw_args, seq)
            h0 = (re0[l].reshape(batch, S5_LANES), im0[l].reshape(batch, S5_LANES))
            os5, hr, hi = _s5_mix(us5, gs5, h0, W['s5_sample'], W['d_skip'], W['w_glu'], l, seq, 1)
            hr = hr.reshape(batch, S5_GROUPS, S5_STATE)
            hi = hi.reshape(batch, S5_GROUPS, S5_STATE)
            nseg = 16 // seq
            x2d = _post(x2d, orw, os5, W['w_out'], W['norm_x'], W['wq'], W['wo'], mem_k, mem_v,
                        W['norm_f'], l, 1, nseg, seq, nseg * seq, l == DEPTH - 1)
        new_shift.append(sh)
        new_wkv.append(wkv)
        new_re.append(hr)
        new_im.append(hi)
    y = x2d.reshape(batch, seq, D_MODEL)
    return y, jnp.stack(new_shift), jnp.stack(new_wkv), jnp.stack(new_re), jnp.stack(new_im)


def kernel(x_prompt, x_sample, state_shift, state_wkv, state_s5_re, state_s5_im, cache_mem_k, cache_mem_v, mem_prompt, norm_mix, w_in, mu_shift, w0, w2, a0, a2, v0, v1, v2, k_k, k_a, r_k, gn_w, gn_b, lam_re, lam_im, log_dt, b_re, b_im, c_re, c_im, d_skip, w_glu, w_out, norm_x, norm_mem, wq, wk, wv, wo, norm_f):
    L = DEPTH
    bp, mp = mem_prompt.shape[0], mem_prompt.shape[1]
    zpad = jnp.zeros((1, RWKV_WIDTH), F32)
    vecs = jnp.stack([w0, a0, k_k, k_a, gn_w, gn_b, r_k.reshape(L, RWKV_WIDTH),
                      jnp.concatenate([zpad, v0], axis=0)], axis=1)
    zl = jnp.zeros((L, LORA_W, RWKV_WIDTH), F32)
    wla = jnp.concatenate([jnp.concatenate([w2, zl], axis=2),
                           jnp.concatenate([zl, a2], axis=2)], axis=1).astype(BF16)
    tr = lambda t: jnp.transpose(t, (0, 3, 1, 2)).reshape(L, S5_GROUP, S5_LANES)
    s5_in = (lam_re.reshape(L, 1, S5_LANES), lam_im.reshape(L, 1, S5_LANES),
             jnp.repeat(log_dt, S5_STATE, axis=1).reshape(L, 1, S5_LANES),
             tr(b_re), tr(b_im),
             jnp.transpose(c_re, (0, 2, 1, 3)).reshape(L, S5_GROUP, S5_LANES),
             jnp.transpose(c_im, (0, 2, 1, 3)).reshape(L, S5_GROUP, S5_LANES))
    W = dict(
        norm_mix=norm_mix.reshape(L, 1, D_MODEL), w_in=w_in.astype(BF16),
        mu=mu_shift.reshape(L, 1, SHIFT_COLS), vecs=vecs, wla=wla,
        v1=v1.astype(BF16), v2=v2.astype(BF16),
        s5_prompt=_s5_prep(*s5_in, 8), s5_sample=_s5_prep(*s5_in, x_sample.shape[1]),
        d_skip=d_skip.reshape(L, 1, S5_WIDTH), w_glu=w_glu.astype(BF16),
        w_out=w_out.astype(BF16), norm_x=norm_x.reshape(L, 1, D_MODEL),
        wq=wq.astype(BF16), wo=wo.astype(BF16), norm_f=norm_f.reshape(1, D_MODEL),
    )
    pk, pv = _memkv(mem_prompt.reshape(bp * mp, D_MODEL), norm_mem, wk.astype(BF16), wv.astype(BF16))
    pk = pk.reshape(L, bp, mp, D_MODEL)
    pv = pv.reshape(L, bp, mp, D_MODEL)
    y_prompt, p_shift, p_wkv, p_re, p_im = _trunk(x_prompt, None, pk, pv, W)
    db = x_sample.shape[0]
    ck = cache_mem_k.reshape(L, db, MEM_LEN, D_MODEL)
    cv = cache_mem_v.reshape(L, db, MEM_LEN, D_MODEL)
    y_sample, s_shift, s_wkv, s_re, s_im = _trunk(
        x_sample, (state_shift, state_wkv, state_s5_re, state_s5_im), ck, cv, W)
    return (y_prompt, y_sample, p_shift, p_wkv, p_re, p_im,
            pk.reshape(L, bp, mp, X_HEADS, X_HEAD_DIM), pv.reshape(L, bp, mp, X_HEADS, X_HEAD_DIM),
            s_shift, s_wkv, s_re, s_im)
```

```python
import functools
import math

import jax
import jax.numpy as jnp
from jax import lax
from jax.experimental import pallas as pl
from jax.experimental.pallas import tpu as pltpu

F32 = jnp.float32
BF16 = jnp.bfloat16

D_MODEL = 1024
DEPTH = 4
RWKV_WIDTH = 512
RWKV_HEAD = 64
RWKV_HEADS = 8
LORA_W = 64
LORA_A = 64
LORA_V = 32
S5_WIDTH = 512
S5_GROUP = 16
S5_GROUPS = 32
S5_STATE = 64
S5_LANES = S5_GROUPS * S5_STATE
MEM_LEN = 256
X_HEADS = 4
X_HEAD_DIM = 256
SHIFT_COLS = 3 * RWKV_WIDTH + LORA_W + LORA_A
IN_COLS = SHIFT_COLS + RWKV_WIDTH + 2 * S5_WIDTH
NORM_EPS = 1e-6
GN_EPS = 64e-5

LANE = 128
WKV_CHUNK = 64
HALF_HEADS = 4
WKV_UNROLL = 4
S5_TILES = S5_WIDTH // LANE
GROUPS_PER_TILE = LANE // S5_GROUP
TILE_STATES = GROUPS_PER_TILE * S5_STATE
S5_SCAN_ROWS = 8
VMEM_LIMIT = 56 * 1024 * 1024


def _cparams(sem):
    return pltpu.CompilerParams(dimension_semantics=sem, vmem_limit_bytes=VMEM_LIMIT)


def _mm(a, b):
    return jnp.dot(a.astype(BF16), b.astype(BF16), preferred_element_type=F32)


def _mm_nt(a, b):
    return lax.dot_general(a.astype(BF16), b.astype(BF16), (((1,), (1,)), ((), ())),
                           preferred_element_type=F32)


def _mm_tn(a, b):
    return lax.dot_general(a.astype(BF16), b.astype(BF16), (((0,), (0,)), ((), ())),
                           preferred_element_type=F32)


def _mm_f32(a, b):
    return jnp.dot(a, b, preferred_element_type=F32, precision=lax.Precision.HIGHEST)


def _mm_nt_f32(a, b):
    return lax.dot_general(a, b, (((1,), (1,)), ((), ())), preferred_element_type=F32,
                           precision=lax.Precision.HIGHEST)


def _rms(x, g):
    return x * lax.rsqrt(jnp.mean(x * x, axis=-1, keepdims=True) + NORM_EPS) * g


def _sigmoid(x):
    return 1.0 / (1.0 + jnp.exp(-x))


def _silu(x):
    return x * _sigmoid(x)


def _iota(shape, dim):
    return lax.broadcasted_iota(jnp.int32, shape, dim)


def _memkv_kernel(m_ref, g_ref, wk_ref, wv_ref, k_ref, v_ref):
    mn = _rms(m_ref[...], g_ref[0]).astype(BF16)
    k_ref[0] = jnp.dot(mn, wk_ref[0], preferred_element_type=F32)
    v_ref[0] = jnp.dot(mn, wv_ref[0], preferred_element_type=F32)


def _memkv(mem2d, norm_mem, wk_b, wv_b):
    rows = mem2d.shape[0]
    tm = 512
    out = jax.ShapeDtypeStruct((DEPTH, rows, D_MODEL), F32)
    return pl.pallas_call(
        _memkv_kernel,
        grid=(DEPTH, rows // tm),
        in_specs=[
            pl.BlockSpec((tm, D_MODEL), lambda l, i: (i, 0)),
            pl.BlockSpec((1, 1, D_MODEL), lambda l, i: (l, 0, 0)),
            pl.BlockSpec((1, D_MODEL, D_MODEL), lambda l, i: (l, 0, 0)),
            pl.BlockSpec((1, D_MODEL, D_MODEL), lambda l, i: (l, 0, 0)),
        ],
        out_specs=[
            pl.BlockSpec((1, tm, D_MODEL), lambda l, i: (l, i, 0)),
            pl.BlockSpec((1, tm, D_MODEL), lambda l, i: (l, i, 0)),
        ],
        out_shape=[out, out],
        compiler_params=_cparams(("arbitrary", "arbitrary")),
        name="memkv",
    )(mem2d, norm_mem.reshape(DEPTH, 1, D_MODEL), wk_b, wv_b)


_IN_SPLITS = (0, SHIFT_COLS, SHIFT_COLS + RWKV_WIDTH, SHIFT_COLS + RWKV_WIDTH + S5_WIDTH, IN_COLS)


def _inproj_kernel(x_ref, g_ref, w_ref, psh_ref, grw_ref, us5_ref, gs5_ref):
    xn = _rms(x_ref[...], g_ref[0]).astype(BF16)
    proj = lambda lo, hi: jnp.dot(xn, w_ref[0, :, lo:hi], preferred_element_type=F32)
    psh_ref[...] = proj(_IN_SPLITS[0], _IN_SPLITS[1])
    grw_ref[...] = proj(_IN_SPLITS[1], _IN_SPLITS[2])
    for o_ref, lo in ((us5_ref, _IN_SPLITS[2]), (gs5_ref, _IN_SPLITS[3])):
        for j in range(S5_TILES):
            o_ref[j] = proj(lo + j * LANE, lo + (j + 1) * LANE)


def _inproj(x2d, norm_mix3, w_in_b, l):
    rows = x2d.shape[0]
    tm = 256
    s5_spec = pl.BlockSpec((S5_TILES, tm, LANE), lambda i: (0, i, 0))
    s5_shape = jax.ShapeDtypeStruct((S5_TILES, rows, LANE), F32)
    return pl.pallas_call(
        _inproj_kernel,
        grid=(rows // tm,),
        in_specs=[
            pl.BlockSpec((tm, D_MODEL), lambda i: (i, 0)),
            pl.BlockSpec((1, 1, D_MODEL), lambda i: (l, 0, 0)),
            pl.BlockSpec((1, D_MODEL, IN_COLS), lambda i: (l, 0, 0)),
        ],
        out_specs=[pl.BlockSpec((tm, SHIFT_COLS), lambda i: (i, 0)),
                   pl.BlockSpec((tm, RWKV_WIDTH), lambda i: (i, 0)), s5_spec, s5_spec],
        out_shape=[jax.ShapeDtypeStruct((rows, SHIFT_COLS), F32),
                   jax.ShapeDtypeStruct((rows, RWKV_WIDTH), F32), s5_shape, s5_shape],
        compiler_params=_cparams(("arbitrary",)),
        name="inproj",
    )(x2d, norm_mix3, w_in_b)


_V_W0, _V_A0, _V_KK, _V_KA, _V_GNW, _V_GNB, _V_RK, _V_V0 = range(8)


def _rwkv_prep(ps, vfirst, vecs, wla, v1, v2):
    r = ps[:, 0:RWKV_WIDTH]
    k = ps[:, RWKV_WIDTH:2 * RWKV_WIDTH]
    v = ps[:, 2 * RWKV_WIDTH:3 * RWKV_WIDTH]
    lora_in = ps[:, 3 * RWKV_WIDTH:SHIFT_COLS]
    lane = _iota(lora_in.shape, 1)
    lora_in = jnp.where(lane < LORA_W, jnp.tanh(lora_in), lora_in)
    lora = _mm(lora_in, wla)
    wpre = vecs[_V_W0:_V_W0 + 1] + lora[:, 0:RWKV_WIDTH]
    z = -wpre
    softplus = jnp.maximum(z, 0.0) + jnp.log1p(jnp.exp(-jnp.abs(z)))
    logdecay = -jnp.exp(-softplus - 0.5)
    a = _sigmoid(vecs[_V_A0:_V_A0 + 1] + lora[:, RWKV_WIDTH:2 * RWKV_WIDTH])
    if vfirst is not None:
        gate = _sigmoid(vecs[_V_V0:_V_V0 + 1] + _mm(_mm(v, v1), v2))
        v = v + (vfirst - v) * gate
    kku = k * vecs[_V_KK:_V_KK + 1]
    kmod = k * (1.0 + (a - 1.0) * vecs[_V_KA:_V_KA + 1])
    return r, kku, kmod, v, a, logdecay


def _wkv_chunk(lw, load_head, states, seg_len, vecs):
    C = WKV_CHUNK
    nseg = C // seg_len
    shift = int(math.log2(seg_len))
    row = _iota((C, C), 0)
    col = _iota((C, C), 1)
    same = (row >> shift) == (col >> shift)
    incl = jnp.logical_and(col <= row, same)
    strict = jnp.logical_and(col < row, same)
    cum = _mm_f32(incl.astype(F32), lw)
    tot = _mm_f32(same.astype(F32), lw)
    g_inc_all = jnp.exp(cum)
    g_exc_all = jnp.exp(cum - lw)
    g_inv_all = jnp.exp(-cum)
    g_end_all = jnp.exp(tot - cum)
    g_tot_all = jnp.exp(tot)
    eye = (row == col).astype(F32)
    nsq = max(shift - 1, 0)
    rowv = _iota((C, 1), 0)
    row2 = _iota((2 * C, 1), 0)

    ys = []
    new_states = []
    for h in range(RWKV_HEADS):
        sl = slice(h * RWKV_HEAD, (h + 1) * RWKV_HEAD)
        r, kku, kmod, v, a = load_head(h)
        ss = jnp.sum(kku * kku, axis=-1, keepdims=True)
        kkn = kku * lax.rsqrt(jnp.maximum(ss, 1e-24))
        b = kkn * a
        g_inv = g_inv_all[:, sl]
        g_end = g_end_all[:, sl]
        qk = kkn * g_exc_all[:, sl]
        qr = r * g_inc_all[:, sl]
        q2 = jnp.concatenate([qk, qr], axis=0).astype(BF16)
        ak = _mm_nt(q2, kmod * g_inv)
        ab = _mm_nt(q2, b * g_inv)
        akk = jnp.where(strict, ak[:C], 0.0)
        ark = jnp.where(incl, ak[C:], 0.0)
        abk = jnp.where(strict, ab[:C], 0.0)
        arb = jnp.where(incl, ab[C:], 0.0)
        p = -abk
        tinv = eye + p
        for _ in range(nsq):
            p = _mm(p, p)
            tinv = tinv + _mm(tinv, p)
        if nseg == 1:
            qs = _mm_nt(q2, states[h][0])
        else:
            qs = jnp.zeros((2 * C, RWKV_HEAD), F32)
            for s in range(nseg):
                m2 = ((row2 & (C - 1)) >> shift) == s
                qs = qs + _mm_nt(jnp.where(m2, q2, jnp.zeros_like(q2)), states[h][s])
        u = _mm(tinv, qs[:C] + _mm(akk, v))
        y = qs[C:] + _mm(ark, v) - _mm(arb, u)
        kc = kmod * g_end
        bc = b * g_end
        hs = []
        for s in range(nseg):
            if nseg == 1:
                vm, um = v, u
            else:
                m = (rowv >> shift) == s
                vm = jnp.where(m, v, 0.0)
                um = jnp.where(m, u, 0.0)
            gt = g_tot_all[s * seg_len:s * seg_len + 1, sl]
            hs.append(states[h][s] * gt + _mm_tn(vm, kc) - _mm_tn(um, bc))
        new_states.append(hs)
        mu = jnp.mean(y, axis=-1, keepdims=True)
        yc = y - mu
        var = jnp.mean(yc * yc, axis=-1, keepdims=True)
        yn = yc * lax.rsqrt(var + GN_EPS) * vecs[_V_GNW:_V_GNW + 1, sl] + vecs[_V_GNB:_V_GNB + 1, sl]
        bonus = jnp.sum(r * kmod * vecs[_V_RK:_V_RK + 1, sl], axis=-1, keepdims=True)
        ys.append(yn + bonus * v)
    return jnp.concatenate(ys, axis=-1), new_states


def _rwkv_prompt_kernel(*refs, first_layer, tt):
    if first_layer:
        (psh_ref, grw_ref, mu_ref, vecs_ref, wla_ref, v1_ref, v2_ref,
         o_ref, vf_out_ref, shift_out_ref, wkv_out_ref,
         carry_ref, s_ref, r_s, kkn_s, kmod_s, v_s, b_s, lw_s, y_s) = refs
        vf_ref = None
    else:
        (psh_ref, grw_ref, vf_ref, mu_ref, vecs_ref, wla_ref, v1_ref, v2_ref,
         o_ref, shift_out_ref, wkv_out_ref,
         carry_ref, s_ref, r_s, kkn_s, kmod_s, v_s, b_s, lw_s, y_s) = refs
        vf_out_ref = None
    t = pl.program_id(1)

    @pl.when(t == 0)
    def _():
        carry_ref[...] = jnp.zeros_like(carry_ref)
        s_ref[...] = jnp.zeros_like(s_ref)

    p = psh_ref[0]
    rowi = _iota(p.shape, 0)
    prev = jnp.where(rowi == 0, carry_ref[...], pltpu.roll(p, 1, 0))
    ps = p + (prev - p) * mu_ref[0]
    last = p[tt - 1:tt, :]
    carry_ref[...] = last
    shift_out_ref[0] = last

    vecs = vecs_ref[0]
    vfirst = None if first_layer else vf_ref[0]
    r, kku, kmod, v, a, lw = _rwkv_prep(ps, vfirst, vecs, wla_ref[0], v1_ref[0], v2_ref[0])
    if first_layer:
        vf_out_ref[0] = v

    hw = HALF_HEADS * RWKV_HEAD
    bd_bool = (_iota((hw, hw), 0) >> 6) == (_iota((hw, hw), 1) >> 6)
    bd_ones = bd_bool.astype(F32).astype(BF16)

    def head_sums(x):
        n = x.shape[0]
        hi = x.astype(BF16)
        lo = (x - hi.astype(F32)).astype(BF16)
        st = jnp.concatenate([hi, lo], axis=0)
        outs = []
        for hf in range(2):
            s2 = jnp.dot(st[:, hf * hw:(hf + 1) * hw], bd_ones, preferred_element_type=F32)
            outs.append(s2[:n] + s2[n:])
        return jnp.concatenate(outs, axis=1)

    kkn = kku * lax.rsqrt(jnp.maximum(head_sums(kku * kku), 1e-24))
    r_s[...] = r
    kkn_s[...] = kkn
    kmod_s[...] = kmod
    v_s[...] = v
    b_s[...] = kkn * a
    lw_s[...] = lw

    C = WKV_CHUNK
    trow = _iota((C, hw), 0)
    tcol = _iota((C, hw), 1) & (RWKV_HEAD - 1)
    strict = tcol < trow
    incl = tcol <= trow
    eye = (tcol == trow).astype(F32)
    tri = (_iota((C, C), 1) <= _iota((C, C), 0)).astype(F32)

    def bd(x):
        return jnp.concatenate([x.astype(BF16)] * HALF_HEADS, axis=0) * bd_ones

    def group(g, carry):
        chains = []
        for k in range(WKV_UNROLL):
            rows = pl.ds(pl.multiple_of((g * WKV_UNROLL + k) * C, C), C)
            lwc = lw_s[rows, :]
            cum = _mm_f32(tri, lwc)
            tot = cum[C - 1:C, :]
            g_inc = jnp.exp(cum)
            g_exc = jnp.exp(cum - lwc)
            g_inv = jnp.exp(-cum)
            g_end = jnp.exp(tot - cum)
            g_tot = jnp.exp(tot)
            rr, kk, km, bb, vv = (r_s[rows, :], kkn_s[rows, :], kmod_s[rows, :], b_s[rows, :],
                                  v_s[rows, :])
            qk_a, qr_a = kk * g_exc, rr * g_inc
            kd_a, bd_a = km * g_inv, bb * g_inv
            kc_a, bc_a = km * g_end, bb * g_end
            for hf in range(2):
                ls = slice(hf * hw, (hf + 1) * hw)
                chains.append(dict(k=k, hf=hf, rows=rows, qk=qk_a[:, ls], qr=qr_a[:, ls], kd=kd_a[:, ls],
                                   bd=bd_a[:, ls], kc=kc_a[:, ls], bc=bc_a[:, ls], v=vv[:, ls],
                                   g_tot=g_tot[:, ls]))
        for ch in chains:
            ch['q2'] = jnp.concatenate([ch['qk'], ch['qr']], axis=0).astype(BF16)
            ch['ak'] = _mm_nt(ch['q2'], bd(ch['kd']))
        for ch in chains:
            ab = _mm_nt(ch['q2'], bd(ch['bd']))
            ch['akk'] = jnp.where(strict, ch['ak'][:C], 0.0)
            ch['ark'] = jnp.where(incl, ch['ak'][C:], 0.0)
            ch['arb'] = jnp.where(incl, ab[C:], 0.0)
            ch['n'] = -jnp.where(strict, ab[:C], 0.0)
            ch['tinv'] = eye + ch['n']
        for ch in chains:
            ch['npow'] = _mm(ch['n'], bd(ch['n']))
        for _ in range(4):
            for ch in chains:
                st = _mm(jnp.concatenate([ch['tinv'], ch['npow']], axis=0), bd(ch['npow']))
                ch['tinv'] = ch['tinv'] + st[:C]
                ch['npow'] = st[C:]
        for ch in chains:
            ch['av'] = _mm(jnp.concatenate([ch['akk'], ch['ark']], axis=0), bd(ch['v']))
        for ch in chains:
            ch['tinv'] = ch['tinv'] + _mm(ch['tinv'], bd(ch['npow']))
        for ch in chains:
            ch['u_ind'] = _mm(ch['tinv'], bd(ch['av'][:C]))
            ch['tq'] = _mm(ch['tinv'], bd(ch['qk']))
        for ch in chains:
            ch['gc'] = jnp.where(bd_bool, _mm_tn(ch['tq'], ch['bc']), 0.0).astype(BF16)
            ch['nc'] = jnp.where(bd_bool, _mm_tn(jnp.concatenate([ch['v'], ch['u_ind']], axis=0),
                                                 jnp.concatenate([ch['kc'], -ch['bc']], axis=0)), 0.0)
        state = [s_ref[0], s_ref[1]]
        for ch in chains:
            s_old = state[ch['hf']]
            ch['s_b'] = s_old.astype(BF16)
            state[ch['hf']] = s_old * ch['g_tot'] - _mm(ch['s_b'], ch['gc']) + ch['nc']
        s_ref[0] = state[0]
        s_ref[1] = state[1]
        for ch in chains:
            ch['qs'] = _mm_nt(jnp.concatenate([ch['tq'], ch['qr']], axis=0), ch['s_b'])
        for ch in chains:
            u = ch['u_ind'] + ch['qs'][:C]
            ch['y'] = ch['qs'][C:] + ch['av'][C:] - _mm(ch['arb'], bd(u))
        for k in range(WKV_UNROLL):
            y_s[chains[2 * k]['rows'], :] = jnp.concatenate([chains[2 * k]['y'], chains[2 * k + 1]['y']], axis=1)
        return carry

    lax.fori_loop(0, tt // (C * WKV_UNROLL), group, 0)

    y = y_s[...]
    inv_n = 1.0 / RWKV_HEAD
    mu = head_sums(y) * inv_n
    yc = y - mu
    var = head_sums(yc * yc) * inv_n
    yn = yc * lax.rsqrt(var + GN_EPS) * vecs[_V_GNW:_V_GNW + 1] + vecs[_V_GNB:_V_GNB + 1]
    rr, km, vv = r_s[...], kmod_s[...], v_s[...]
    bonus = head_sums(rr * km * vecs[_V_RK:_V_RK + 1])
    o_ref[0] = (yn + bonus * vv) * _silu(grw_ref[0])

    @pl.when(t == pl.num_programs(1) - 1)
    def _():
        for h in range(RWKV_HEADS):
            o = (h % HALF_HEADS) * RWKV_HEAD
            wkv_out_ref[0, h] = s_ref[h // HALF_HEADS, o:o + RWKV_HEAD, o:o + RWKV_HEAD]


def _rwkv_prompt(psh, grw, vfirst, mu3, vecs, wla, v1, v2, l, batch, seq):
    tt = 512
    first = vfirst is None
    lv = max(l - 1, 0)
    tok = lambda w: pl.BlockSpec((1, tt, w), lambda b, t: (b, t, 0))
    in_specs = [tok(SHIFT_COLS), tok(RWKV_WIDTH)]
    args = [psh.reshape(batch, seq, SHIFT_COLS), grw.reshape(batch, seq, RWKV_WIDTH)]
    if not first:
        in_specs.append(tok(RWKV_WIDTH))
        args.append(vfirst.reshape(batch, seq, RWKV_WIDTH))
    in_specs += [
        pl.BlockSpec((1, 1, SHIFT_COLS), lambda b, t: (l, 0, 0)),
        pl.BlockSpec((1, 8, RWKV_WIDTH), lambda b, t: (l, 0, 0)),
        pl.BlockSpec((1, LORA_W + LORA_A, 2 * RWKV_WIDTH), lambda b, t: (l, 0, 0)),
        pl.BlockSpec((1, RWKV_WIDTH, LORA_V), lambda b, t: (lv, 0, 0)),
        pl.BlockSpec((1, LORA_V, RWKV_WIDTH), lambda b, t: (lv, 0, 0)),
    ]
    args += [mu3, vecs, wla, v1, v2]
    out_specs = [tok(RWKV_WIDTH)]
    out_shape = [jax.ShapeDtypeStruct((batch, seq, RWKV_WIDTH), F32)]
    if first:
        out_specs.append(tok(RWKV_WIDTH))
        out_shape.append(jax.ShapeDtypeStruct((batch, seq, RWKV_WIDTH), F32))
    out_specs += [
        pl.BlockSpec((1, 1, SHIFT_COLS), lambda b, t: (b, 0, 0)),
        pl.BlockSpec((1, RWKV_HEADS, RWKV_HEAD, RWKV_HEAD), lambda b, t: (b, 0, 0, 0)),
    ]
    out_shape += [
        jax.ShapeDtypeStruct((batch, 1, SHIFT_COLS), F32),
        jax.ShapeDtypeStruct((batch, RWKV_HEADS, RWKV_HEAD, RWKV_HEAD), F32),
    ]
    scratch = [
        pltpu.VMEM((1, SHIFT_COLS), F32),
        pltpu.VMEM((2, HALF_HEADS * RWKV_HEAD, HALF_HEADS * RWKV_HEAD), F32),
    ] + [pltpu.VMEM((tt, RWKV_WIDTH), F32) for _ in range(7)]
    outs = pl.pallas_call(
        functools.partial(_rwkv_prompt_kernel, first_layer=first, tt=tt),
        grid=(batch, seq // tt),
        in_specs=in_specs,
        out_specs=out_specs,
        out_shape=out_shape,
        scratch_shapes=scratch,
        compiler_params=_cparams(("arbitrary", "arbitrary")),
        name="rwkv_prompt",
    )(*args)
    if first:
        o, vf, sh, wkv = outs
    else:
        (o, sh, wkv), vf = outs, vfirst
    return o.reshape(batch * seq, RWKV_WIDTH), vf.reshape(batch * seq, RWKV_WIDTH), sh[:, 0], wkv


def _rwkv_sample_kernel(*refs, first_layer, seq):
    if first_layer:
        (psh_ref, grw_ref, prev_ref, wkv_ref, mu_ref, vecs_ref, wla_ref, v1_ref, v2_ref,
         o_ref, vf_out_ref, wkv_out_ref) = refs
        vf_ref = None
    else:
        (psh_ref, grw_ref, vf_ref, prev_ref, wkv_ref, mu_ref, vecs_ref, wla_ref, v1_ref, v2_ref,
         o_ref, wkv_out_ref) = refs
        vf_out_ref = None
    nseq = WKV_CHUNK // seq
    p = psh_ref[...]
    rowi = _iota(p.shape, 0)
    prev = jnp.where((rowi & (seq - 1)) == 0, prev_ref[...], pltpu.roll(p, 1, 0))
    ps = p + (prev - p) * mu_ref[0]

    vecs = vecs_ref[0]
    vfirst = None if first_layer else vf_ref[...]
    r, kku, kmod, v, a, lw = _rwkv_prep(ps, vfirst, vecs, wla_ref[0], v1_ref[0], v2_ref[0])
    if first_layer:
        vf_out_ref[...] = v

    def load_head(h):
        sl = slice(h * RWKV_HEAD, (h + 1) * RWKV_HEAD)
        return (r[:, sl], kku[:, sl], kmod[:, sl], v[:, sl], a[:, sl])

    states = [[wkv_ref[0, s, h] for s in range(nseq)] for h in range(RWKV_HEADS)]
    y, new_states = _wkv_chunk(lw, load_head, states, seq, vecs)
    for h in range(RWKV_HEADS):
        for s in range(nseq):
            wkv_out_ref[s, h] = new_states[h][s]
    o_ref[...] = y * _silu(grw_ref[...])


def _rwkv_sample(psh, grw, vfirst, prev_rows, wkv0, mu3, vecs, wla, v1, v2, l, seq):
    rows = psh.shape[0]
    nseq = WKV_CHUNK // seq
    first = vfirst is None
    lv = max(l - 1, 0)
    tok = lambda w: pl.BlockSpec((WKV_CHUNK, w), lambda i: (i, 0))
    st_in = pl.BlockSpec((1, nseq, RWKV_HEADS, RWKV_HEAD, RWKV_HEAD), lambda i: (l, i, 0, 0, 0))
    st = pl.BlockSpec((nseq, RWKV_HEADS, RWKV_HEAD, RWKV_HEAD), lambda i: (i, 0, 0, 0))
    in_specs = [tok(SHIFT_COLS), tok(RWKV_WIDTH)]
    args = [psh, grw]
    if not first:
        in_specs.append(tok(RWKV_WIDTH))
        args.append(vfirst)
    in_specs += [
        tok(SHIFT_COLS), st_in,
        pl.BlockSpec((1, 1, SHIFT_COLS), lambda i: (l, 0, 0)),
        pl.BlockSpec((1, 8, RWKV_WIDTH), lambda i: (l, 0, 0)),
        pl.BlockSpec((1, LORA_W + LORA_A, 2 * RWKV_WIDTH), lambda i: (l, 0, 0)),
        pl.BlockSpec((1, RWKV_WIDTH, LORA_V), lambda i: (lv, 0, 0)),
        pl.BlockSpec((1, LORA_V, RWKV_WIDTH), lambda i: (lv, 0, 0)),
    ]
    args += [prev_rows, wkv0, mu3, vecs, wla, v1, v2]
    out_specs = [tok(RWKV_WIDTH)]
    out_shape = [jax.ShapeDtypeStruct((rows, RWKV_WIDTH), F32)]
    if first:
        out_specs.append(tok(RWKV_WIDTH))
        out_shape.append(jax.ShapeDtypeStruct((rows, RWKV_WIDTH), F32))
    out_specs.append(st)
    out_shape.append(jax.ShapeDtypeStruct(wkv0.shape[1:], F32))
    outs = pl.pallas_call(
        functools.partial(_rwkv_sample_kernel, first_layer=first, seq=seq),
        grid=(rows // WKV_CHUNK,),
        in_specs=in_specs,
        out_specs=out_specs,
        out_shape=out_shape,
        compiler_params=_cparams(("arbitrary",)),
        name="rwkv_sample",
    )(*args)
    if first:
        o, vf, wkv = outs
    else:
        (o, wkv), vf = outs, vfirst
    sh = psh.reshape(rows // seq, seq, SHIFT_COLS)[:, seq - 1]
    return o, vf, sh, wkv


def _s5_prep_kernel(lam_re_ref, lam_im_ref, ldt_ref, bt_re_ref, bt_im_ref, ct_re_ref, ct_im_ref,
                    mt_ref, wz_re_ref, wz_im_ref, wc_re_ref, wc_im_ref, ap_re_ref, ap_im_ref, *, cs):
    n = TILE_STATES
    row_g = _iota((LANE, n), 0) >> 4
    lane_g = _iota((LANE, n), 1) >> 6
    emb_mask = row_g == lane_g
    bd_mask = (_iota((LANE, LANE), 0) >> 4) == (_iota((LANE, LANE), 1) >> 4)

    def rep(x):
        return jnp.concatenate([x] * GROUPS_PER_TILE, axis=0)

    def embed(x):
        return jnp.where(emb_mask, rep(x), 0.0)

    for j in range(S5_TILES):
        ls = slice(j * n, (j + 1) * n)
        lr = lam_re_ref[0, :, ls]
        li = lam_im_ref[0, :, ls]
        dt = jnp.exp(ldt_ref[0, :, ls])
        mag = jnp.exp(lr * dt)
        a_re = mag * jnp.cos(li * dt)
        a_im = mag * jnp.sin(li * dt)
        q_re = a_re - 1.0
        q_im = a_im
        den = lr * lr + li * li
        f_re = (q_re * lr + q_im * li) / den
        f_im = (q_im * lr - q_re * li) / den
        b_re = bt_re_ref[0, :, ls]
        b_im = bt_im_ref[0, :, ls]
        bb_re = f_re * b_re - f_im * b_im
        bb_im = f_re * b_im + f_im * b_re
        c_re = ct_re_ref[0, :, ls]
        c_im = ct_im_ref[0, :, ls]
        pw = [(jnp.ones_like(a_re), jnp.zeros_like(a_re))]
        for _ in range(cs):
            pr, pi = pw[-1]
            pw.append((pr * a_re - pi * a_im, pr * a_im + pi * a_re))
        e = [(pr * bb_re - pi * bb_im, pr * bb_im + pi * bb_re) for pr, pi in pw[:cs]]
        wz_re_ref[0, j] = jnp.concatenate([embed(e[cs - 1 - s][0]) for s in range(cs)], axis=0).astype(BF16)
        wz_im_ref[0, j] = jnp.concatenate([embed(e[cs - 1 - s][1]) for s in range(cs)], axis=0).astype(BF16)
        wc_re_ref[0, j] = jnp.concatenate(
            [embed(c_re * pw[s + 1][0] - c_im * pw[s + 1][1]) for s in range(cs)], axis=0).astype(BF16)
        wc_im_ref[0, j] = jnp.concatenate(
            [embed(-(c_re * pw[s + 1][1] + c_im * pw[s + 1][0])) for s in range(cs)], axis=0).astype(BF16)
        ce_re = embed(c_re)
        ce_im = embed(c_im)
        d = [jnp.where(bd_mask, _mm_nt_f32(ce_re, rep(er)) - _mm_nt_f32(ce_im, rep(ei)), 0.0) for er, ei in e]
        zero = jnp.zeros((LANE, LANE), F32)
        mt_ref[0, j] = jnp.concatenate(
            [jnp.concatenate([d[s - sp] if sp <= s else zero for sp in range(cs)], axis=1) for s in range(cs)],
            axis=0).astype(BF16)
        pr, pi = pw[cs]
        for k in range(S5_SCAN_ROWS):
            ap_re_ref[0, k:k + 1, ls] = pr
            ap_im_ref[0, k:k + 1, ls] = pi
            pr, pi = pr * pr - pi * pi, 2.0 * pr * pi


def _s5_prep(lam_re, lam_im, ldt_rep, bt_re, bt_im, ct_re, ct_im, cs):
    vec = pl.BlockSpec((1, 1, S5_LANES), lambda l: (l, 0, 0))
    mat = pl.BlockSpec((1, S5_GROUP, S5_LANES), lambda l: (l, 0, 0))
    k = cs * LANE
    w_spec = pl.BlockSpec((1, S5_TILES, k, TILE_STATES), lambda l: (l, 0, 0, 0))
    w_shape = jax.ShapeDtypeStruct((DEPTH, S5_TILES, k, TILE_STATES), BF16)
    ap_spec = pl.BlockSpec((1, S5_SCAN_ROWS, S5_LANES), lambda l: (l, 0, 0))
    ap_shape = jax.ShapeDtypeStruct((DEPTH, S5_SCAN_ROWS, S5_LANES), F32)
    return pl.pallas_call(
        functools.partial(_s5_prep_kernel, cs=cs),
        grid=(DEPTH,),
        in_specs=[vec, vec, vec, mat, mat, mat, mat],
        out_specs=[pl.BlockSpec((1, S5_TILES, k, k), lambda l: (l, 0, 0, 0)),
                   w_spec, w_spec, w_spec, w_spec, ap_spec, ap_spec],
        out_shape=[jax.ShapeDtypeStruct((DEPTH, S5_TILES, k, k), BF16),
                   w_shape, w_shape, w_shape, w_shape, ap_shape, ap_shape],
        compiler_params=_cparams(("arbitrary",)),
        name="s5_prep",
    )(lam_re, lam_im, ldt_rep, bt_re, bt_im, ct_re, ct_im)


def _gelu(x):
    return 0.5 * x * (1.0 + lax.erf(x * (1.0 / math.sqrt(2.0))))


def _s5_kernel(*refs, cs, rows, chain):
    if chain:
        (u_ref, g_ref, mt_ref, wz_re_ref, wz_im_ref, wc_re_ref, wc_im_ref, ap_re_ref, ap_im_ref,
         dsk_ref, wglu_ref, o_ref, hr_out_ref, hi_out_ref, hc_r_ref, hc_i_ref) = refs

        @pl.when(pl.program_id(1) == 0)
        def _():
            hc_r_ref[...] = jnp.zeros_like(hc_r_ref)
            hc_i_ref[...] = jnp.zeros_like(hc_i_ref)
    else:
        (u_ref, g_ref, h0r_ref, h0i_ref, mt_ref, wz_re_ref, wz_im_ref, wc_re_ref, wc_im_ref,
         ap_re_ref, ap_im_ref, dsk_ref, wglu_ref, o_ref, hr_out_ref, hi_out_ref) = refs
    n = TILE_STATES
    us = [[u_ref[j, pl.ds(s, rows, stride=cs), :] for s in range(cs)] for j in range(S5_TILES)]
    x8 = [jnp.concatenate(us[j], axis=1).astype(BF16) for j in range(S5_TILES)]
    zr = jnp.concatenate([jnp.dot(x8[j], wz_re_ref[0, j], preferred_element_type=F32)
                          for j in range(S5_TILES)], axis=1)
    zi = jnp.concatenate([jnp.dot(x8[j], wz_im_ref[0, j], preferred_element_type=F32)
                          for j in range(S5_TILES)], axis=1)
    if chain:
        rowi = _iota(zr.shape, 0)
        cr = hc_r_ref[...]
        ci = hc_i_ref[...]
        a1r = ap_re_ref[0, 0:1, :]
        a1i = ap_im_ref[0, 0:1, :]
        hr = zr + jnp.where(rowi == 0, a1r * cr - a1i * ci, 0.0)
        hi = zi + jnp.where(rowi == 0, a1r * ci + a1i * cr, 0.0)
        d = 1
        k = 0
        while d < rows:
            ar = ap_re_ref[0, k:k + 1, :]
            ai = ap_im_ref[0, k:k + 1, :]
            sr = jnp.where(rowi >= d, pltpu.roll(hr, d, 0), 0.0)
            si = jnp.where(rowi >= d, pltpu.roll(hi, d, 0), 0.0)
            hr, hi = hr + ar * sr - ai * si, hi + ar * si + ai * sr
            d *= 2
            k += 1
        hin_r = jnp.where(rowi >= 1, pltpu.roll(hr, 1, 0), cr)
        hin_i = jnp.where(rowi >= 1, pltpu.roll(hi, 1, 0), ci)
        hc_r_ref[...] = hr[rows - 1:rows, :]
        hc_i_ref[...] = hi[rows - 1:rows, :]
        hr_out_ref[0] = hr[rows - 1:rows, :]
        hi_out_ref[0] = hi[rows - 1:rows, :]
    else:
        hin_r = h0r_ref[...]
        hin_i = h0i_ref[...]
        ar = ap_re_ref[0, 0:1, :]
        ai = ap_im_ref[0, 0:1, :]
        hr_out_ref[...] = ar * hin_r - ai * hin_i + zr
        hi_out_ref[...] = ar * hin_i + ai * hin_r + zi
    hin_r = hin_r.astype(BF16)
    hin_i = hin_i.astype(BF16)
    y8 = []
    for j in range(S5_TILES):
        ls = slice(j * n, (j + 1) * n)
        y8.append(_mm_nt(x8[j], mt_ref[0, j]) + _mm_nt(hin_r[:, ls], wc_re_ref[0, j])
                  + _mm_nt(hin_i[:, ls], wc_im_ref[0, j]))
    dsk = dsk_ref[0]
    for s in range(cs):
        y = jnp.concatenate([y8[j][:, s * LANE:(s + 1) * LANE] for j in range(S5_TILES)], axis=1)
        u_s = jnp.concatenate([us[j][s] for j in range(S5_TILES)], axis=1)
        y = _gelu(y + dsk * u_s)
        y = y * _sigmoid(jnp.dot(y.astype(BF16), wglu_ref[0], preferred_element_type=F32))
        gate = jnp.concatenate([g_ref[j, pl.ds(s, rows, stride=cs), :] for j in range(S5_TILES)], axis=1)
        out = y * _silu(gate)
        for j in range(S5_TILES):
            o_ref[j, pl.ds(s, rows, stride=cs), :] = out[:, j * LANE:(j + 1) * LANE]


def _s5_mix(u, g, h0, ops, dsk3, wglu_b, l, cs, nb):
    seq = u.shape[1] // nb
    chain = h0 is None
    ntok = min(seq, 1024) if chain else seq
    tiles = seq // ntok
    rows = ntok // cs
    mt, wz_re, wz_im, wc_re, wc_im, ap_re, ap_im = ops
    k = cs * LANE
    tok = pl.BlockSpec((S5_TILES, ntok, LANE), lambda b, t: (0, b * tiles + t, 0))
    w_spec = pl.BlockSpec((1, S5_TILES, k, TILE_STATES), lambda b, t: (l, 0, 0, 0))
    ap_spec = pl.BlockSpec((1, S5_SCAN_ROWS, S5_LANES), lambda b, t: (l, 0, 0))
    in_specs = [tok, tok]
    args = [u, g]
    scratch = []
    if not chain:
        hs = pl.BlockSpec((rows, S5_LANES), lambda b, t: (0, 0))
        in_specs += [hs, hs]
        args += [h0[0], h0[1]]
        st_spec = hs
        st_shape = jax.ShapeDtypeStruct((rows, S5_LANES), F32)
    else:
        st_spec = pl.BlockSpec((1, 1, S5_LANES), lambda b, t: (b, 0, 0))
        st_shape = jax.ShapeDtypeStruct((nb, 1, S5_LANES), F32)
        scratch = [pltpu.VMEM((1, S5_LANES), F32), pltpu.VMEM((1, S5_LANES), F32)]
    in_specs += [
        pl.BlockSpec((1, S5_TILES, k, k), lambda b, t: (l, 0, 0, 0)),
        w_spec, w_spec, w_spec, w_spec, ap_spec, ap_spec,
        pl.BlockSpec((1, 1, S5_WIDTH), lambda b, t: (l, 0, 0)),
        pl.BlockSpec((1, S5_WIDTH, S5_WIDTH), lambda b, t: (l, 0, 0)),
    ]
    args += [mt, wz_re, wz_im, wc_re, wc_im, ap_re, ap_im, dsk3, wglu_b]
    o, hr, hi = pl.pallas_call(
        functools.partial(_s5_kernel, cs=cs, rows=rows, chain=chain),
        grid=(nb, tiles),
        in_specs=in_specs,
        out_specs=[tok, st_spec, st_spec],
        out_shape=[jax.ShapeDtypeStruct(u.shape, F32), st_shape, st_shape],
        scratch_shapes=scratch,
        compiler_params=_cparams(("arbitrary", "arbitrary")),
        name="s5_mix",
    )(*args)
    return o, hr, hi


def _softmax(s):
    e = jnp.exp(s - jnp.max(s, axis=-1, keepdims=True))
    return (e / jnp.sum(e, axis=-1, keepdims=True)).astype(BF16)


def _attend(q, mk_ref, mv_ref):
    heads = [slice(h * X_HEAD_DIM, (h + 1) * X_HEAD_DIM) for h in range(X_HEADS)]
    s = [_mm_nt(q[:, hs], mk_ref[0, 0, :, hs]) for hs in heads]
    p = [_softmax(sh) for sh in s]
    return jnp.concatenate([_mm(ph, mv_ref[0, 0, :, hs]) for ph, hs in zip(p, heads)], axis=-1)


_DT = X_HEAD_DIM // LANE


def _attend_tiled(q, mk_ref, mv_ref, nseg, row_seg):
    def sub(ref, i, h, dt):
        return ref[0, i, pl.ds(dt * X_HEADS + h, MEM_LEN, stride=X_HEADS * _DT), :]

    def pick(vals):
        out = vals[0]
        for i in range(1, nseg):
            out = jnp.where(row_seg == i, vals[i], out)
        return out

    qb = q.astype(BF16)
    qp = lambda h, dt: qb[:, h * X_HEAD_DIM + dt * LANE:h * X_HEAD_DIM + (dt + 1) * LANE]
    s = [[sum(_mm_nt(qp(h, dt), sub(mk_ref, i, h, dt)) for dt in range(_DT)) for i in range(nseg)]
         for h in range(X_HEADS)]
    p = [_softmax(pick(sh)) for sh in s]
    outs = [pick([_mm(p[h], sub(mv_ref, i, h, dt)) for i in range(nseg)])
            for h in range(X_HEADS) for dt in range(_DT)]
    return jnp.concatenate(outs, axis=-1)


def _post_kernel(x_ref, orw_ref, os5_ref, wout_ref, nx_ref, wq_ref, wo_ref, mk_ref, mv_ref, nf_ref,
                 o_ref, *, nseg, seg_rows, final):
    x = x_ref[...]
    os5 = jnp.concatenate([os5_ref[j] for j in range(S5_TILES)], axis=1)
    x1 = (x + jnp.dot(orw_ref[...].astype(BF16), wout_ref[0, 0:RWKV_WIDTH, :], preferred_element_type=F32)
          + jnp.dot(os5.astype(BF16), wout_ref[0, RWKV_WIDTH:, :], preferred_element_type=F32))
    xc = _rms(x1, nx_ref[0]).astype(BF16)
    q = jnp.dot(xc, wq_ref[0], preferred_element_type=F32) * (1.0 / math.sqrt(X_HEAD_DIM))
    if nseg == 1:
        att = _attend(q, mk_ref, mv_ref)
    else:
        row_seg = _iota((q.shape[0], 1), 0) >> int(math.log2(seg_rows))
        att = _attend_tiled(q, mk_ref, mv_ref, nseg, row_seg)
    x2 = x1 + jnp.dot(att.astype(BF16), wo_ref[0], preferred_element_type=F32)
    if final:
        x2 = _rms(x2, nf_ref[...])
    o_ref[...] = x2


def _post(x2d, orw, os5, wout_b, nx3, wq_b, wo_b, mk, mv, nf2, l, nb, nseg, seg_rows, tq, final):
    tiles = x2d.shape[0] // nb // tq
    tok = lambda w: pl.BlockSpec((tq, w), lambda b, t: (b * tiles + t, 0))
    tok_s5 = pl.BlockSpec((S5_TILES, tq, LANE), lambda b, t: (0, b * tiles + t, 0))
    if nseg == 1:
        mem = pl.BlockSpec((1, 1, MEM_LEN, D_MODEL), lambda b, t: (l, b, 0, 0))
    else:
        mem = pl.BlockSpec((1, nseg, MEM_LEN * X_HEADS * _DT, LANE), lambda b, t: (l, t, 0, 0))
    wsp = pl.BlockSpec((1, D_MODEL, D_MODEL), lambda b, t: (l, 0, 0))
    return pl.pallas_call(
        functools.partial(_post_kernel, nseg=nseg, seg_rows=seg_rows, final=final),
        grid=(nb, tiles),
        in_specs=[tok(D_MODEL), tok(RWKV_WIDTH), tok_s5, wsp,
                  pl.BlockSpec((1, 1, D_MODEL), lambda b, t: (l, 0, 0)), wsp, wsp, mem, mem,
                  pl.BlockSpec((1, D_MODEL), lambda b, t: (0, 0))],
        out_specs=tok(D_MODEL),
        out_shape=jax.ShapeDtypeStruct(x2d.shape, F32),
        compiler_params=_cparams(("arbitrary", "arbitrary")),
        name="post",
    )(x2d, orw, os5, wout_b, nx3, wq_b, wo_b, mk, mv, nf2)


def _trunk(x, states, mem_k, mem_v, W):
    batch, seq, _ = x.shape
    prompt = states is None
    x2d = x.reshape(batch * seq, D_MODEL)
    new_shift, new_wkv, new_re, new_im = [], [], [], []
    vfirst = None
    for l in range(DEPTH):
        psh, grw, us5, gs5 = _inproj(x2d, W['norm_mix'], W['w_in'], l)
        rw_args = (W['mu'], W['vecs'], W['wla'], W['v1'], W['v2'], l)
        if prompt:
            orw, vfirst, sh, wkv = _rwkv_prompt(psh, grw, vfirst, *rw_args, batch, seq)
            os5, hr, hi = _s5_mix(us5, gs5, None, W['s5_prompt'], W['d_skip'], W['w_glu'], l, 8, batch)
            hr = hr.reshape(batch, S5_GROUPS, S5_STATE)
            hi = hi.reshape(batch, S5_GROUPS, S5_STATE)
            x2d = _post(x2d, orw, os5, W['w_out'], W['norm_x'], W['wq'], W['wo'], mem_k, mem_v,
                        W['norm_f'], l, batch, 1, seq, 512, l == DEPTH - 1)
        else:
            shift0, wkv0, re0, im0 = states
            prev_rows = jnp.repeat(shift0[l], seq, axis=0)
            orw, vfirst, sh, wkv = _rwkv_sample(psh, grw, vfirst, prev_rows, wkv0, *rw_args, seq)
            h0 = (re0[l].reshape(batch, S5_LANES), im0[l].reshape(batch, S5_LANES))
            os5, hr, hi = _s5_mix(us5, gs5, h0, W['s5_sample'], W['d_skip'], W['w_glu'], l, seq, 1)
            hr = hr.reshape(batch, S5_GROUPS, S5_STATE)
            hi = hi.reshape(batch, S5_GROUPS, S5_STATE)
            nseg = 16 // seq
            x2d = _post(x2d, orw, os5, W['w_out'], W['norm_x'], W['wq'], W['wo'], mem_k, mem_v,
                        W['norm_f'], l, 1, nseg, seq, nseg * seq, l == DEPTH - 1)
        new_shift.append(sh)
        new_wkv.append(wkv)
        new_re.append(hr)
        new_im.append(hi)
    y = x2d.reshape(batch, seq, D_MODEL)
    return y, jnp.stack(new_shift), jnp.stack(new_wkv), jnp.stack(new_re), jnp.stack(new_im)


def kernel(x_prompt, x_sample, state_shift, state_wkv, state_s5_re, state_s5_im, cache_mem_k, cache_mem_v, mem_prompt, norm_mix, w_in, mu_shift, w0, w2, a0, a2, v0, v1, v2, k_k, k_a, r_k, gn_w, gn_b, lam_re, lam_im, log_dt, b_re, b_im, c_re, c_im, d_skip, w_glu, w_out, norm_x, norm_mem, wq, wk, wv, wo, norm_f):
    L = DEPTH
    bp, mp = mem_prompt.shape[0], mem_prompt.shape[1]
    zpad = jnp.zeros((1, RWKV_WIDTH), F32)
    vecs = jnp.stack([w0, a0, k_k, k_a, gn_w, gn_b, r_k.reshape(L, RWKV_WIDTH),
                      jnp.concatenate([zpad, v0], axis=0)], axis=1)
    zl = jnp.zeros((L, LORA_W, RWKV_WIDTH), F32)
    wla = jnp.concatenate([jnp.concatenate([w2, zl], axis=2),
                           jnp.concatenate([zl, a2], axis=2)], axis=1).astype(BF16)
    tr = lambda t: jnp.transpose(t, (0, 3, 1, 2)).reshape(L, S5_GROUP, S5_LANES)
    s5_in = (lam_re.reshape(L, 1, S5_LANES), lam_im.reshape(L, 1, S5_LANES),
             jnp.repeat(log_dt, S5_STATE, axis=1).reshape(L, 1, S5_LANES),
             tr(b_re), tr(b_im),
             jnp.transpose(c_re, (0, 2, 1, 3)).reshape(L, S5_GROUP, S5_LANES),
             jnp.transpose(c_im, (0, 2, 1, 3)).reshape(L, S5_GROUP, S5_LANES))
    W = dict(
        norm_mix=norm_mix.reshape(L, 1, D_MODEL), w_in=w_in.astype(BF16),
        mu=mu_shift.reshape(L, 1, SHIFT_COLS), vecs=vecs, wla=wla,
        v1=v1.astype(BF16), v2=v2.astype(BF16),
        s5_prompt=_s5_prep(*s5_in, 8), s5_sample=_s5_prep(*s5_in, x_sample.shape[1]),
        d_skip=d_skip.reshape(L, 1, S5_WIDTH), w_glu=w_glu.astype(BF16),
        w_out=w_out.astype(BF16), norm_x=norm_x.reshape(L, 1, D_MODEL),
        wq=wq.astype(BF16), wo=wo.astype(BF16), norm_f=norm_f.reshape(1, D_MODEL),
    )
    pk, pv = _memkv(mem_prompt.reshape(bp * mp, D_MODEL), norm_mem, wk.astype(BF16), wv.astype(BF16))
    pk = pk.reshape(L, bp, mp, D_MODEL)
    pv = pv.reshape(L, bp, mp, D_MODEL)
    y_prompt, p_shift, p_wkv, p_re, p_im = _trunk(x_prompt, None, pk, pv, W)
    db = x_sample.shape[0]
    tiled = lambda c: jnp.transpose(c.reshape(L, db, MEM_LEN, X_HEADS, _DT, LANE),
                                    (0, 1, 2, 4, 3, 5)).reshape(L, db, MEM_LEN * _DT * X_HEADS, LANE)
    ck = tiled(cache_mem_k)
    cv = tiled(cache_mem_v)
    y_sample, s_shift, s_wkv, s_re, s_im = _trunk(
        x_sample, (state_shift, state_wkv, state_s5_re, state_s5_im), ck, cv, W)
    return (y_prompt, y_sample, p_shift, p_wkv, p_re, p_im,
            pk.reshape(L, bp, mp, X_HEADS, X_HEAD_DIM), pv.reshape(L, bp, mp, X_HEADS, X_HEAD_DIM),
            s_shift, s_wkv, s_re, s_im)
```

```python
import functools
import math

import jax
import jax.numpy as jnp
from jax import lax
from jax.experimental import pallas as pl
from jax.experimental.pallas import tpu as pltpu

F32 = jnp.float32
BF16 = jnp.bfloat16

D_MODEL = 1024
DEPTH = 4
RWKV_WIDTH = 512
RWKV_HEAD = 64
RWKV_HEADS = 8
LORA_W = 64
LORA_A = 64
LORA_V = 32
S5_WIDTH = 512
S5_GROUP = 16
S5_GROUPS = 32
S5_STATE = 64
S5_LANES = S5_GROUPS * S5_STATE
MEM_LEN = 256
X_HEADS = 4
X_HEAD_DIM = 256
SHIFT_COLS = 3 * RWKV_WIDTH + LORA_W + LORA_A
IN_COLS = SHIFT_COLS + RWKV_WIDTH + 2 * S5_WIDTH
NORM_EPS = 1e-6
GN_EPS = 64e-5

LANE = 128
SUBLANES = 8
WKV_CHUNK = 64
HALF_HEADS = 4
WKV_UNROLL = 4
S5_TILES = S5_WIDTH // LANE
GROUPS_PER_TILE = LANE // S5_GROUP
TILE_STATES = GROUPS_PER_TILE * S5_STATE
S5_SCAN_ROWS = 8
VMEM_LIMIT = 56 * 1024 * 1024


def _cparams(sem):
    return pltpu.CompilerParams(dimension_semantics=sem, vmem_limit_bytes=VMEM_LIMIT)


def _mm(a, b):
    return jnp.dot(a.astype(BF16), b.astype(BF16), preferred_element_type=F32)


def _mm_nt(a, b):
    return lax.dot_general(a.astype(BF16), b.astype(BF16), (((1,), (1,)), ((), ())),
                           preferred_element_type=F32)


def _mm_tn(a, b):
    return lax.dot_general(a.astype(BF16), b.astype(BF16), (((0,), (0,)), ((), ())),
                           preferred_element_type=F32)


def _mm_f32(a, b):
    return jnp.dot(a, b, preferred_element_type=F32, precision=lax.Precision.HIGHEST)


def _mm_nt_f32(a, b):
    return lax.dot_general(a, b, (((1,), (1,)), ((), ())), preferred_element_type=F32,
                           precision=lax.Precision.HIGHEST)


def _rms(x, g):
    return x * lax.rsqrt(jnp.mean(x * x, axis=-1, keepdims=True) + NORM_EPS) * g


def _sigmoid(x):
    return 1.0 / (1.0 + jnp.exp(-x))


def _silu(x):
    return x * _sigmoid(x)


def _iota(shape, dim):
    return lax.broadcasted_iota(jnp.int32, shape, dim)


def _memkv_kernel(m_ref, g_ref, wk_ref, wv_ref, k_ref, v_ref):
    mn = _rms(m_ref[...], g_ref[0]).astype(BF16)
    k_ref[0] = jnp.dot(mn, wk_ref[0], preferred_element_type=F32)
    v_ref[0] = jnp.dot(mn, wv_ref[0], preferred_element_type=F32)


def _memkv(mem2d, norm_mem, wk_b, wv_b):
    rows = mem2d.shape[0]
    tm = 512
    out = jax.ShapeDtypeStruct((DEPTH, rows, D_MODEL), F32)
    return pl.pallas_call(
        _memkv_kernel,
        grid=(DEPTH, rows // tm),
        in_specs=[
            pl.BlockSpec((tm, D_MODEL), lambda l, i: (i, 0)),
            pl.BlockSpec((1, 1, D_MODEL), lambda l, i: (l, 0, 0)),
            pl.BlockSpec((1, D_MODEL, D_MODEL), lambda l, i: (l, 0, 0)),
            pl.BlockSpec((1, D_MODEL, D_MODEL), lambda l, i: (l, 0, 0)),
        ],
        out_specs=[
            pl.BlockSpec((1, tm, D_MODEL), lambda l, i: (l, i, 0)),
            pl.BlockSpec((1, tm, D_MODEL), lambda l, i: (l, i, 0)),
        ],
        out_shape=[out, out],
        compiler_params=_cparams(("arbitrary", "arbitrary")),
        name="memkv",
    )(mem2d, norm_mem.reshape(DEPTH, 1, D_MODEL), wk_b, wv_b)


_IN_SPLITS = (0, SHIFT_COLS, SHIFT_COLS + RWKV_WIDTH, SHIFT_COLS + RWKV_WIDTH + S5_WIDTH, IN_COLS)


def _inproj_kernel(x_ref, g_ref, w_ref, psh_ref, grw_ref, us5_ref, gs5_ref):
    xn = _rms(x_ref[...], g_ref[0]).astype(BF16)
    proj = lambda lo, hi: jnp.dot(xn, w_ref[0, :, lo:hi], preferred_element_type=F32)
    psh_ref[...] = proj(_IN_SPLITS[0], _IN_SPLITS[1])
    grw_ref[...] = proj(_IN_SPLITS[1], _IN_SPLITS[2])
    for o_ref, lo in ((us5_ref, _IN_SPLITS[2]), (gs5_ref, _IN_SPLITS[3])):
        for j in range(S5_TILES):
            o_ref[j] = proj(lo + j * LANE, lo + (j + 1) * LANE)


def _inproj(x2d, norm_mix3, w_in_b, l):
    rows = x2d.shape[0]
    tm = 256
    s5_spec = pl.BlockSpec((S5_TILES, tm, LANE), lambda i: (0, i, 0))
    s5_shape = jax.ShapeDtypeStruct((S5_TILES, rows, LANE), F32)
    return pl.pallas_call(
        _inproj_kernel,
        grid=(rows // tm,),
        in_specs=[
            pl.BlockSpec((tm, D_MODEL), lambda i: (i, 0)),
            pl.BlockSpec((1, 1, D_MODEL), lambda i: (l, 0, 0)),
            pl.BlockSpec((1, D_MODEL, IN_COLS), lambda i: (l, 0, 0)),
        ],
        out_specs=[pl.BlockSpec((tm, SHIFT_COLS), lambda i: (i, 0)),
                   pl.BlockSpec((tm, RWKV_WIDTH), lambda i: (i, 0)), s5_spec, s5_spec],
        out_shape=[jax.ShapeDtypeStruct((rows, SHIFT_COLS), F32),
                   jax.ShapeDtypeStruct((rows, RWKV_WIDTH), F32), s5_shape, s5_shape],
        compiler_params=_cparams(("arbitrary",)),
        name="inproj",
    )(x2d, norm_mix3, w_in_b)


_V_W0, _V_A0, _V_KK, _V_KA, _V_GNW, _V_GNB, _V_RK, _V_V0 = range(8)


def _rwkv_prep(ps, vfirst, vecs, wla, v1, v2):
    r = ps[:, 0:RWKV_WIDTH]
    k = ps[:, RWKV_WIDTH:2 * RWKV_WIDTH]
    v = ps[:, 2 * RWKV_WIDTH:3 * RWKV_WIDTH]
    lora_in = ps[:, 3 * RWKV_WIDTH:SHIFT_COLS]
    lane = _iota(lora_in.shape, 1)
    lora_in = jnp.where(lane < LORA_W, jnp.tanh(lora_in), lora_in)
    lora = _mm(lora_in, wla)
    wpre = vecs[_V_W0:_V_W0 + 1] + lora[:, 0:RWKV_WIDTH]
    z = -wpre
    softplus = jnp.maximum(z, 0.0) + jnp.log(1.0 + jnp.exp(-jnp.abs(z)))
    logdecay = -jnp.exp(-softplus - 0.5)
    a = _sigmoid(vecs[_V_A0:_V_A0 + 1] + lora[:, RWKV_WIDTH:2 * RWKV_WIDTH])
    if vfirst is not None:
        gate = _sigmoid(vecs[_V_V0:_V_V0 + 1] + _mm(_mm(v, v1), v2))
        v = v + (vfirst - v) * gate
    kku = k * vecs[_V_KK:_V_KK + 1]
    kmod = k * (1.0 + (a - 1.0) * vecs[_V_KA:_V_KA + 1])
    return r, kku, kmod, v, a, logdecay


def _wkv_chunk(lw, load_head, states, seg_len, vecs):
    C = WKV_CHUNK
    nseg = C // seg_len
    shift = int(math.log2(seg_len))
    row = _iota((C, C), 0)
    col = _iota((C, C), 1)
    same = (row >> shift) == (col >> shift)
    incl = jnp.logical_and(col <= row, same)
    strict = jnp.logical_and(col < row, same)
    cum = _mm_f32(incl.astype(F32), lw)
    tot = _mm_f32(same.astype(F32), lw)
    g_inc_all = jnp.exp(cum)
    g_exc_all = jnp.exp(cum - lw)
    g_inv_all = jnp.exp(-cum)
    g_end_all = jnp.exp(tot - cum)
    g_tot_all = jnp.exp(tot)
    eye = (row == col).astype(F32)
    nsq = max(shift - 1, 0)
    rowv = _iota((C, 1), 0)
    row2 = _iota((2 * C, 1), 0)

    ys = []
    new_states = []
    for h in range(RWKV_HEADS):
        sl = slice(h * RWKV_HEAD, (h + 1) * RWKV_HEAD)
        r, kku, kmod, v, a = load_head(h)
        ss = jnp.sum(kku * kku, axis=-1, keepdims=True)
        kkn = kku * lax.rsqrt(jnp.maximum(ss, 1e-24))
        b = kkn * a
        g_inv = g_inv_all[:, sl]
        g_end = g_end_all[:, sl]
        qk = kkn * g_exc_all[:, sl]
        qr = r * g_inc_all[:, sl]
        q2 = jnp.concatenate([qk, qr], axis=0).astype(BF16)
        ak = _mm_nt(q2, kmod * g_inv)
        ab = _mm_nt(q2, b * g_inv)
        akk = jnp.where(strict, ak[:C], 0.0)
        ark = jnp.where(incl, ak[C:], 0.0)
        abk = jnp.where(strict, ab[:C], 0.0)
        arb = jnp.where(incl, ab[C:], 0.0)
        p = -abk
        tinv = eye + p
        for _ in range(nsq):
            p = _mm(p, p)
            tinv = tinv + _mm(tinv, p)
        if nseg == 1:
            qs = _mm_nt(q2, states[h][0])
        else:
            qs = jnp.zeros((2 * C, RWKV_HEAD), F32)
            for s in range(nseg):
                m2 = ((row2 & (C - 1)) >> shift) == s
                qs = qs + _mm_nt(jnp.where(m2, q2, jnp.zeros_like(q2)), states[h][s])
        u = _mm(tinv, qs[:C] + _mm(akk, v))
        y = qs[C:] + _mm(ark, v) - _mm(arb, u)
        kc = kmod * g_end
        bc = b * g_end
        hs = []
        for s in range(nseg):
            if nseg == 1:
                vm, um = v, u
            else:
                m = (rowv >> shift) == s
                vm = jnp.where(m, v, 0.0)
                um = jnp.where(m, u, 0.0)
            gt = g_tot_all[s * seg_len:s * seg_len + 1, sl]
            hs.append(states[h][s] * gt + _mm_tn(vm, kc) - _mm_tn(um, bc))
        new_states.append(hs)
        mu = jnp.mean(y, axis=-1, keepdims=True)
        yc = y - mu
        var = jnp.mean(yc * yc, axis=-1, keepdims=True)
        yn = yc * lax.rsqrt(var + GN_EPS) * vecs[_V_GNW:_V_GNW + 1, sl] + vecs[_V_GNB:_V_GNB + 1, sl]
        bonus = jnp.sum(r * kmod * vecs[_V_RK:_V_RK + 1, sl], axis=-1, keepdims=True)
        ys.append(yn + bonus * v)
    return jnp.concatenate(ys, axis=-1), new_states


def _rwkv_prompt_kernel(*refs, first_layer, tt):
    if first_layer:
        (x_ref, nm_ref, win_ref, mu_ref, vecs_ref, wla_ref, v1_ref, v2_ref,
         o_ref, vf_out_ref, us5_ref, gs5_ref, shift_out_ref, wkv_out_ref,
         carry_ref, s_ref, r_s, kkn_s, kmod_s, v_s, b_s, lw_s, y_s) = refs
        vf_ref = None
    else:
        (x_ref, nm_ref, win_ref, vf_ref, mu_ref, vecs_ref, wla_ref, v1_ref, v2_ref,
         o_ref, us5_ref, gs5_ref, shift_out_ref, wkv_out_ref,
         carry_ref, s_ref, r_s, kkn_s, kmod_s, v_s, b_s, lw_s, y_s) = refs
        vf_out_ref = None
    t = pl.program_id(1)

    @pl.when(t == 0)
    def _():
        carry_ref[...] = jnp.zeros_like(carry_ref)
        s_ref[...] = jnp.zeros_like(s_ref)

    xn = _rms(x_ref[0], nm_ref[0]).astype(BF16)
    proj = lambda lo, hi: jnp.dot(xn, win_ref[0, :, lo:hi], preferred_element_type=F32)
    p = proj(_IN_SPLITS[0], _IN_SPLITS[1])
    o_ref[0] = proj(_IN_SPLITS[1], _IN_SPLITS[2])
    for s5_ref, lo in ((us5_ref, _IN_SPLITS[2]), (gs5_ref, _IN_SPLITS[3])):
        for j in range(S5_TILES):
            s5_ref[j] = proj(lo + j * LANE, lo + (j + 1) * LANE)
    rowi = _iota(p.shape, 0)
    prev = jnp.where(rowi == 0, carry_ref[...], pltpu.roll(p, 1, 0))
    ps = p + (prev - p) * mu_ref[0]
    last = p[tt - 1:tt, :]
    carry_ref[...] = last
    shift_out_ref[0] = last

    vecs = vecs_ref[0]
    vfirst = None if first_layer else vf_ref[0]
    r, kku, kmod, v, a, lw = _rwkv_prep(ps, vfirst, vecs, wla_ref[0], v1_ref[0], v2_ref[0])
    if first_layer:
        vf_out_ref[0] = v

    hw = HALF_HEADS * RWKV_HEAD
    bd_bool = (_iota((hw, hw), 0) >> 6) == (_iota((hw, hw), 1) >> 6)
    bd_ones = bd_bool.astype(F32).astype(BF16)

    def head_sums(x):
        n = x.shape[0]
        hi = x.astype(BF16)
        lo = (x - hi.astype(F32)).astype(BF16)
        st = jnp.concatenate([hi, lo], axis=0)
        outs = []
        for hf in range(2):
            s2 = jnp.dot(st[:, hf * hw:(hf + 1) * hw], bd_ones, preferred_element_type=F32)
            outs.append(s2[:n] + s2[n:])
        return jnp.concatenate(outs, axis=1)

    kkn = kku * lax.rsqrt(jnp.maximum(head_sums(kku * kku), 1e-24))
    r_s[...] = r
    kkn_s[...] = kkn
    kmod_s[...] = kmod
    v_s[...] = v
    b_s[...] = kkn * a
    lw_s[...] = lw

    C = WKV_CHUNK
    trow = _iota((C, hw), 0)
    tcol = _iota((C, hw), 1) & (RWKV_HEAD - 1)
    strict = tcol < trow
    incl = tcol <= trow
    eye = (tcol == trow).astype(F32)
    crow = _iota((C, RWKV_WIDTH), 0)

    def bd(x):
        return jnp.concatenate([x.astype(BF16)] * HALF_HEADS, axis=0) * bd_ones

    def group(g, carry):
        chains = []
        for k in range(WKV_UNROLL):
            rows = pl.ds(pl.multiple_of((g * WKV_UNROLL + k) * C, C), C)
            lwc = lw_s[rows, :]
            cum = lwc
            for d in (1, 2, 4):
                cum = cum + jnp.where(crow >= d, pltpu.roll(cum, d, 0), 0.0)
            for d in (8, 16, 32):
                cum = jnp.concatenate([cum[:d], cum[d:] + cum[:C - d]], axis=0)
            tot = cum[C - 1:C, :]
            g_inc = jnp.exp(cum)
            g_exc = jnp.exp(cum - lwc)
            g_inv = jnp.exp(-cum)
            g_end = jnp.exp(tot - cum)
            g_tot = jnp.exp(tot)
            rr, kk, km, bb, vv = (r_s[rows, :], kkn_s[rows, :], kmod_s[rows, :], b_s[rows, :],
                                  v_s[rows, :])
            qk_a, qr_a = kk * g_exc, rr * g_inc
            kd_a, bd_a = km * g_inv, bb * g_inv
            kc_a, bc_a = km * g_end, bb * g_end
            for hf in range(2):
                ls = slice(hf * hw, (hf + 1) * hw)
                chains.append(dict(k=k, hf=hf, rows=rows, qk=qk_a[:, ls], qr=qr_a[:, ls], kd=kd_a[:, ls],
                                   bd=bd_a[:, ls], kc=kc_a[:, ls], bc=bc_a[:, ls], v=vv[:, ls],
                                   g_tot=g_tot[:, ls]))
        for ch in chains:
            ch['q2'] = jnp.concatenate([ch['qk'], ch['qr']], axis=0).astype(BF16)
            ch['ak'] = _mm_nt(ch['q2'], bd(ch['kd']))
        for ch in chains:
            ab = _mm_nt(ch['q2'], bd(ch['bd']))
            ch['akk'] = jnp.where(strict, ch['ak'][:C], 0.0)
            ch['ark'] = jnp.where(incl, ch['ak'][C:], 0.0)
            ch['arb'] = jnp.where(incl, ab[C:], 0.0)
            ch['n'] = -jnp.where(strict, ab[:C], 0.0)
            ch['tinv'] = eye + ch['n']
        for ch in chains:
            ch['npow'] = _mm(ch['n'], bd(ch['n']))
        for _ in range(4):
            for ch in chains:
                st = _mm(jnp.concatenate([ch['tinv'], ch['npow']], axis=0), bd(ch['npow']))
                ch['tinv'] = ch['tinv'] + st[:C]
                ch['npow'] = st[C:]
        for ch in chains:
            ch['av'] = _mm(jnp.concatenate([ch['akk'], ch['ark']], axis=0), bd(ch['v']))
        for ch in chains:
            ch['tinv'] = ch['tinv'] + _mm(ch['tinv'], bd(ch['npow']))
        for ch in chains:
            ch['u_ind'] = _mm(ch['tinv'], bd(ch['av'][:C]))
            ch['tq'] = _mm(ch['tinv'], bd(ch['qk']))
        for ch in chains:
            ch['gc'] = jnp.where(bd_bool, _mm_tn(ch['tq'], ch['bc']), 0.0).astype(BF16)
            ch['nc'] = jnp.where(bd_bool, _mm_tn(jnp.concatenate([ch['v'], ch['u_ind']], axis=0),
                                                 jnp.concatenate([ch['kc'], -ch['bc']], axis=0)), 0.0)
        state = [s_ref[0], s_ref[1]]
        for ch in chains:
            s_old = state[ch['hf']]
            ch['s_b'] = s_old.astype(BF16)
            state[ch['hf']] = s_old * ch['g_tot'] - _mm(ch['s_b'], ch['gc']) + ch['nc']
        s_ref[0] = state[0]
        s_ref[1] = state[1]
        for ch in chains:
            ch['qs'] = _mm_nt(jnp.concatenate([ch['tq'], ch['qr']], axis=0), ch['s_b'])
        for ch in chains:
            u = ch['u_ind'] + ch['qs'][:C]
            ch['y'] = ch['qs'][C:] + ch['av'][C:] - _mm(ch['arb'], bd(u))
        for k in range(WKV_UNROLL):
            y_s[chains[2 * k]['rows'], :] = jnp.concatenate([chains[2 * k]['y'], chains[2 * k + 1]['y']], axis=1)
        return carry

    lax.fori_loop(0, tt // (C * WKV_UNROLL), group, 0)

    y = y_s[...]
    inv_n = 1.0 / RWKV_HEAD
    mu = head_sums(y) * inv_n
    yc = y - mu
    var = head_sums(yc * yc) * inv_n
    yn = yc * lax.rsqrt(var + GN_EPS) * vecs[_V_GNW:_V_GNW + 1] + vecs[_V_GNB:_V_GNB + 1]
    rr, km, vv = r_s[...], kmod_s[...], v_s[...]
    bonus = head_sums(rr * km * vecs[_V_RK:_V_RK + 1])
    o_ref[0] = (yn + bonus * vv) * _silu(o_ref[0])

    @pl.when(t == pl.num_programs(1) - 1)
    def _():
        for h in range(RWKV_HEADS):
            o = (h % HALF_HEADS) * RWKV_HEAD
            wkv_out_ref[0, h] = s_ref[h // HALF_HEADS, o:o + RWKV_HEAD, o:o + RWKV_HEAD]


def _rwkv_prompt(x2d, norm_mix3, w_in_b, vfirst, mu3, vecs, wla, v1, v2, l, batch, seq):
    tt = 512
    tiles = seq // tt
    first = vfirst is None
    lv = max(l - 1, 0)
    tok = lambda w: pl.BlockSpec((1, tt, w), lambda b, t: (b, t, 0))
    s5_spec = pl.BlockSpec((S5_TILES, tt, LANE), lambda b, t: (0, b * tiles + t, 0))
    s5_shape = jax.ShapeDtypeStruct((S5_TILES, batch * seq, LANE), F32)
    in_specs = [tok(D_MODEL),
                pl.BlockSpec((1, 1, D_MODEL), lambda b, t: (l, 0, 0)),
                pl.BlockSpec((1, D_MODEL, IN_COLS), lambda b, t: (l, 0, 0))]
    args = [x2d.reshape(batch, seq, D_MODEL), norm_mix3, w_in_b]
    if not first:
        in_specs.append(tok(RWKV_WIDTH))
        args.append(vfirst.reshape(batch, seq, RWKV_WIDTH))
    in_specs += [
        pl.BlockSpec((1, 1, SHIFT_COLS), lambda b, t: (l, 0, 0)),
        pl.BlockSpec((1, 8, RWKV_WIDTH), lambda b, t: (l, 0, 0)),
        pl.BlockSpec((1, LORA_W + LORA_A, 2 * RWKV_WIDTH), lambda b, t: (l, 0, 0)),
        pl.BlockSpec((1, RWKV_WIDTH, LORA_V), lambda b, t: (lv, 0, 0)),
        pl.BlockSpec((1, LORA_V, RWKV_WIDTH), lambda b, t: (lv, 0, 0)),
    ]
    args += [mu3, vecs, wla, v1, v2]
    out_specs = [tok(RWKV_WIDTH)]
    out_shape = [jax.ShapeDtypeStruct((batch, seq, RWKV_WIDTH), F32)]
    if first:
        out_specs.append(tok(RWKV_WIDTH))
        out_shape.append(jax.ShapeDtypeStruct((batch, seq, RWKV_WIDTH), F32))
    out_specs += [
        s5_spec, s5_spec,
        pl.BlockSpec((1, 1, SHIFT_COLS), lambda b, t: (b, 0, 0)),
        pl.BlockSpec((1, RWKV_HEADS, RWKV_HEAD, RWKV_HEAD), lambda b, t: (b, 0, 0, 0)),
    ]
    out_shape += [
        s5_shape, s5_shape,
        jax.ShapeDtypeStruct((batch, 1, SHIFT_COLS), F32),
        jax.ShapeDtypeStruct((batch, RWKV_HEADS, RWKV_HEAD, RWKV_HEAD), F32),
    ]
    scratch = [
        pltpu.VMEM((1, SHIFT_COLS), F32),
        pltpu.VMEM((2, HALF_HEADS * RWKV_HEAD, HALF_HEADS * RWKV_HEAD), F32),
    ] + [pltpu.VMEM((tt, RWKV_WIDTH), F32) for _ in range(7)]
    outs = pl.pallas_call(
        functools.partial(_rwkv_prompt_kernel, first_layer=first, tt=tt),
        grid=(batch, seq // tt),
        in_specs=in_specs,
        out_specs=out_specs,
        out_shape=out_shape,
        scratch_shapes=scratch,
        compiler_params=_cparams(("arbitrary", "arbitrary")),
        name="rwkv_prompt",
    )(*args)
    if first:
        o, vf, us5, gs5, sh, wkv = outs
    else:
        (o, us5, gs5, sh, wkv), vf = outs, vfirst
    return (o.reshape(batch * seq, RWKV_WIDTH), vf.reshape(batch * seq, RWKV_WIDTH), us5, gs5,
            sh[:, 0], wkv)


def _rwkv_sample_kernel(*refs, first_layer, seq):
    if first_layer:
        (psh_ref, grw_ref, prev_ref, wkv_ref, mu_ref, vecs_ref, wla_ref, v1_ref, v2_ref,
         o_ref, vf_out_ref, wkv_out_ref) = refs
        vf_ref = None
    else:
        (psh_ref, grw_ref, vf_ref, prev_ref, wkv_ref, mu_ref, vecs_ref, wla_ref, v1_ref, v2_ref,
         o_ref, wkv_out_ref) = refs
        vf_out_ref = None
    nseq = WKV_CHUNK // seq
    p = psh_ref[...]
    rowi = _iota(p.shape, 0)
    prev = jnp.where((rowi & (seq - 1)) == 0, prev_ref[...], pltpu.roll(p, 1, 0))
    ps = p + (prev - p) * mu_ref[0]

    vecs = vecs_ref[0]
    vfirst = None if first_layer else vf_ref[...]
    r, kku, kmod, v, a, lw = _rwkv_prep(ps, vfirst, vecs, wla_ref[0], v1_ref[0], v2_ref[0])
    if first_layer:
        vf_out_ref[...] = v

    def load_head(h):
        sl = slice(h * RWKV_HEAD, (h + 1) * RWKV_HEAD)
        return (r[:, sl], kku[:, sl], kmod[:, sl], v[:, sl], a[:, sl])

    states = [[wkv_ref[0, s, h] for s in range(nseq)] for h in range(RWKV_HEADS)]
    y, new_states = _wkv_chunk(lw, load_head, states, seq, vecs)
    for h in range(RWKV_HEADS):
        for s in range(nseq):
            wkv_out_ref[s, h] = new_states[h][s]
    o_ref[...] = y * _silu(grw_ref[...])


def _rwkv_sample(psh, grw, vfirst, prev_rows, wkv0, mu3, vecs, wla, v1, v2, l, seq):
    rows = psh.shape[0]
    nseq = WKV_CHUNK // seq
    first = vfirst is None
    lv = max(l - 1, 0)
    tok = lambda w: pl.BlockSpec((WKV_CHUNK, w), lambda i: (i, 0))
    st_in = pl.BlockSpec((1, nseq, RWKV_HEADS, RWKV_HEAD, RWKV_HEAD), lambda i: (l, i, 0, 0, 0))
    st = pl.BlockSpec((nseq, RWKV_HEADS, RWKV_HEAD, RWKV_HEAD), lambda i: (i, 0, 0, 0))
    in_specs = [tok(SHIFT_COLS), tok(RWKV_WIDTH)]
    args = [psh, grw]
    if not first:
        in_specs.append(tok(RWKV_WIDTH))
        args.append(vfirst)
    in_specs += [
        tok(SHIFT_COLS), st_in,
        pl.BlockSpec((1, 1, SHIFT_COLS), lambda i: (l, 0, 0)),
        pl.BlockSpec((1, 8, RWKV_WIDTH), lambda i: (l, 0, 0)),
        pl.BlockSpec((1, LORA_W + LORA_A, 2 * RWKV_WIDTH), lambda i: (l, 0, 0)),
        pl.BlockSpec((1, RWKV_WIDTH, LORA_V), lambda i: (lv, 0, 0)),
        pl.BlockSpec((1, LORA_V, RWKV_WIDTH), lambda i: (lv, 0, 0)),
    ]
    args += [prev_rows, wkv0, mu3, vecs, wla, v1, v2]
    out_specs = [tok(RWKV_WIDTH)]
    out_shape = [jax.ShapeDtypeStruct((rows, RWKV_WIDTH), F32)]
    if first:
        out_specs.append(tok(RWKV_WIDTH))
        out_shape.append(jax.ShapeDtypeStruct((rows, RWKV_WIDTH), F32))
    out_specs.append(st)
    out_shape.append(jax.ShapeDtypeStruct(wkv0.shape[1:], F32))
    outs = pl.pallas_call(
        functools.partial(_rwkv_sample_kernel, first_layer=first, seq=seq),
        grid=(rows // WKV_CHUNK,),
        in_specs=in_specs,
        out_specs=out_specs,
        out_shape=out_shape,
        compiler_params=_cparams(("arbitrary",)),
        name="rwkv_sample",
    )(*args)
    if first:
        o, vf, wkv = outs
    else:
        (o, wkv), vf = outs, vfirst
    sh = psh.reshape(rows // seq, seq, SHIFT_COLS)[:, seq - 1]
    return o, vf, sh, wkv


def _s5_prep_kernel(lam_re_ref, lam_im_ref, ldt_ref, bt_re_ref, bt_im_ref, ct_re_ref, ct_im_ref,
                    wz_ref, wy_ref, ap_re_ref, ap_im_ref, *, cs):
    n = TILE_STATES
    row_g = _iota((LANE, n), 0) >> 4
    lane_g = _iota((LANE, n), 1) >> 6
    emb_mask = row_g == lane_g
    bd_mask = (_iota((LANE, LANE), 0) >> 4) == (_iota((LANE, LANE), 1) >> 4)

    def rep(x):
        return jnp.concatenate([x] * GROUPS_PER_TILE, axis=0)

    def embed(x):
        return jnp.where(emb_mask, rep(x), 0.0)

    for j in range(S5_TILES):
        ls = slice(j * n, (j + 1) * n)
        lr = lam_re_ref[0, :, ls]
        li = lam_im_ref[0, :, ls]
        dt = jnp.exp(ldt_ref[0, :, ls])
        mag = jnp.exp(lr * dt)
        a_re = mag * jnp.cos(li * dt)
        a_im = mag * jnp.sin(li * dt)
        q_re = a_re - 1.0
        q_im = a_im
        den = lr * lr + li * li
        f_re = (q_re * lr + q_im * li) / den
        f_im = (q_im * lr - q_re * li) / den
        b_re = bt_re_ref[0, :, ls]
        b_im = bt_im_ref[0, :, ls]
        bb_re = f_re * b_re - f_im * b_im
        bb_im = f_re * b_im + f_im * b_re
        c_re = ct_re_ref[0, :, ls]
        c_im = ct_im_ref[0, :, ls]
        pw = [(jnp.ones_like(a_re), jnp.zeros_like(a_re))]
        for _ in range(cs):
            pr, pi = pw[-1]
            pw.append((pr * a_re - pi * a_im, pr * a_im + pi * a_re))
        e = [(pr * bb_re - pi * bb_im, pr * bb_im + pi * bb_re) for pr, pi in pw[:cs]]
        wz_ref[0, j, :, 0:n] = jnp.concatenate(
            [embed(e[cs - 1 - s][0]) for s in range(cs)], axis=0).astype(BF16)
        wz_ref[0, j, :, n:2 * n] = jnp.concatenate(
            [embed(e[cs - 1 - s][1]) for s in range(cs)], axis=0).astype(BF16)
        wc_re = jnp.concatenate(
            [embed(c_re * pw[s + 1][0] - c_im * pw[s + 1][1]) for s in range(cs)], axis=0)
        wc_im = jnp.concatenate(
            [embed(-(c_re * pw[s + 1][1] + c_im * pw[s + 1][0])) for s in range(cs)], axis=0)
        wy_ref[0, j, 0:n, :] = wc_re.T.astype(BF16)
        wy_ref[0, j, n:2 * n, :] = wc_im.T.astype(BF16)
        ce_re = embed(c_re)
        ce_im = embed(c_im)
        dt_blk = [jnp.where(bd_mask, _mm_nt_f32(rep(er), ce_re) - _mm_nt_f32(rep(ei), ce_im), 0.0)
                  for er, ei in e]
        zero = jnp.zeros((LANE, LANE), F32)
        wy_ref[0, j, 2 * n:, :] = jnp.concatenate(
            [jnp.concatenate([dt_blk[s - sp] if sp <= s else zero for s in range(cs)], axis=1)
             for sp in range(cs)], axis=0).astype(BF16)
        pr, pi = pw[cs]
        for k in range(S5_SCAN_ROWS):
            ap_re_ref[0, k:k + 1, ls] = pr
            ap_im_ref[0, k:k + 1, ls] = pi
            pr, pi = pr * pr - pi * pi, 2.0 * pr * pi


def _s5_prep(lam_re, lam_im, ldt_rep, bt_re, bt_im, ct_re, ct_im, cs):
    vec = pl.BlockSpec((1, 1, S5_LANES), lambda l: (l, 0, 0))
    mat = pl.BlockSpec((1, S5_GROUP, S5_LANES), lambda l: (l, 0, 0))
    k = cs * LANE
    wz_dims = (S5_TILES, k, 2 * TILE_STATES)
    wy_dims = (S5_TILES, 2 * TILE_STATES + k, k)
    ap_spec = pl.BlockSpec((1, S5_SCAN_ROWS, S5_LANES), lambda l: (l, 0, 0))
    ap_shape = jax.ShapeDtypeStruct((DEPTH, S5_SCAN_ROWS, S5_LANES), F32)
    return pl.pallas_call(
        functools.partial(_s5_prep_kernel, cs=cs),
        grid=(DEPTH,),
        in_specs=[vec, vec, vec, mat, mat, mat, mat],
        out_specs=[pl.BlockSpec((1,) + wz_dims, lambda l: (l, 0, 0, 0)),
                   pl.BlockSpec((1,) + wy_dims, lambda l: (l, 0, 0, 0)), ap_spec, ap_spec],
        out_shape=[jax.ShapeDtypeStruct((DEPTH,) + wz_dims, BF16),
                   jax.ShapeDtypeStruct((DEPTH,) + wy_dims, BF16), ap_shape, ap_shape],
        compiler_params=_cparams(("arbitrary",)),
        name="s5_prep",
    )(lam_re, lam_im, ldt_rep, bt_re, bt_im, ct_re, ct_im)


def _gelu(x):
    return 0.5 * x * (1.0 + lax.erf(x * (1.0 / math.sqrt(2.0))))


def _s5_kernel(*refs, cs, rows, chain):
    if chain:
        (u_ref, g_ref, wz_ref, wy_ref, ap_re_ref, ap_im_ref,
         dsk_ref, wglu_ref, o_ref, hr_out_ref, hi_out_ref, hc_r_ref, hc_i_ref) = refs

        @pl.when(pl.program_id(1) == 0)
        def _():
            hc_r_ref[...] = jnp.zeros_like(hc_r_ref)
            hc_i_ref[...] = jnp.zeros_like(hc_i_ref)
    else:
        (u_ref, g_ref, h0r_ref, h0i_ref, wz_ref, wy_ref, ap_re_ref, ap_im_ref,
         dsk_ref, wglu_ref, o_ref, hr_out_ref, hi_out_ref) = refs
    n = TILE_STATES
    k = cs * LANE
    def load_rows(ref, j, s):
        return ref[j, pl.ds(s, rows, stride=cs), :]

    def store_rows(ref, j, s, val):
        ref[j, pl.ds(s, rows, stride=cs), :] = val

    us = [[load_rows(u_ref, j, s) for s in range(cs)] for j in range(S5_TILES)]
    x8 = [jnp.concatenate(us[j], axis=1).astype(BF16) for j in range(S5_TILES)]
    z = [jnp.dot(x8[j], wz_ref[0, j], preferred_element_type=F32) for j in range(S5_TILES)]
    rowi = _iota((rows, n), 0)
    row8 = _iota((SUBLANES, n), 0)
    hin = []
    for j in range(S5_TILES):
        ls = slice(j * n, (j + 1) * n)
        zr, zi = z[j][:, :n], z[j][:, n:]
        ar, ai = ap_re_ref[0, 0:1, ls], ap_im_ref[0, 0:1, ls]
        if chain:
            cr, ci = hc_r_ref[:, ls], hc_i_ref[:, ls]
            hr = jnp.concatenate([zr[:SUBLANES] + jnp.where(row8 == 0, ar * cr - ai * ci, 0.0),
                                  zr[SUBLANES:]], axis=0)
            hi = jnp.concatenate([zi[:SUBLANES] + jnp.where(row8 == 0, ar * ci + ai * cr, 0.0),
                                  zi[SUBLANES:]], axis=0)
            d, lvl = 1, 0
            while d < rows:
                pr, pi = ap_re_ref[0, lvl:lvl + 1, ls], ap_im_ref[0, lvl:lvl + 1, ls]
                if d < SUBLANES:
                    sr = jnp.where(rowi >= d, pltpu.roll(hr, d, 0), 0.0)
                    si = jnp.where(rowi >= d, pltpu.roll(hi, d, 0), 0.0)
                    hr, hi = hr + pr * sr - pi * si, hi + pr * si + pi * sr
                else:
                    sr, si = hr[:rows - d], hi[:rows - d]
                    hr, hi = (jnp.concatenate([hr[:d], hr[d:] + pr * sr - pi * si], axis=0),
                              jnp.concatenate([hi[:d], hi[d:] + pr * si + pi * sr], axis=0))
                d *= 2
                lvl += 1
            hin_r = jnp.where(rowi >= 1, pltpu.roll(hr, 1, 0), cr)
            hin_i = jnp.where(rowi >= 1, pltpu.roll(hi, 1, 0), ci)
            hc_r_ref[:, ls] = hr[rows - 1:rows]
            hc_i_ref[:, ls] = hi[rows - 1:rows]
            hr_out_ref[0, :, ls] = hr[rows - 1:rows]
            hi_out_ref[0, :, ls] = hi[rows - 1:rows]
        else:
            hin_r, hin_i = h0r_ref[:, ls], h0i_ref[:, ls]
            hr_out_ref[:, ls] = ar * hin_r - ai * hin_i + zr
            hi_out_ref[:, ls] = ar * hin_i + ai * hin_r + zi
        hin.append((hin_r.astype(BF16), hin_i.astype(BF16)))
    y8 = []
    for j in range(S5_TILES):
        lhs = jnp.concatenate([hin[j][0], hin[j][1], x8[j]], axis=1)
        cols = []
        for b in range(k // 256):
            kk = 2 * n + (b + 1) * 256
            cols.append(jnp.dot(lhs[:, :kk], wy_ref[0, j, 0:kk, b * 256:(b + 1) * 256],
                                preferred_element_type=F32))
        y8.append(jnp.concatenate(cols, axis=1))
    dsk = dsk_ref[0]
    for s in range(cs):
        y = jnp.concatenate([y8[j][:, s * LANE:(s + 1) * LANE] for j in range(S5_TILES)], axis=1)
        u_s = jnp.concatenate([us[j][s] for j in range(S5_TILES)], axis=1)
        y = _gelu(y + dsk * u_s)
        y = y * _sigmoid(jnp.dot(y.astype(BF16), wglu_ref[0], preferred_element_type=F32))
        gate = jnp.concatenate([load_rows(g_ref, j, s) for j in range(S5_TILES)], axis=1)
        out = y * _silu(gate)
        for j in range(S5_TILES):
            store_rows(o_ref, j, s, out[:, j * LANE:(j + 1) * LANE])


def _s5_mix(u, g, h0, ops, dsk3, wglu_b, l, cs, nb):
    seq = u.shape[1] // nb
    chain = h0 is None
    ntok = min(seq, 1024) if chain else seq
    tiles = seq // ntok
    rows = ntok // cs
    wz, wy, ap_re, ap_im = ops
    tok = pl.BlockSpec((S5_TILES, ntok, LANE), lambda b, t: (0, b * tiles + t, 0))
    ap_spec = pl.BlockSpec((1, S5_SCAN_ROWS, S5_LANES), lambda b, t: (l, 0, 0))
    in_specs = [tok, tok]
    args = [u, g]
    scratch = []
    if not chain:
        hs = pl.BlockSpec((rows, S5_LANES), lambda b, t: (0, 0))
        in_specs += [hs, hs]
        args += [h0[0], h0[1]]
        st_spec = hs
        st_shape = jax.ShapeDtypeStruct((rows, S5_LANES), F32)
    else:
        st_spec = pl.BlockSpec((1, 1, S5_LANES), lambda b, t: (b, 0, 0))
        st_shape = jax.ShapeDtypeStruct((nb, 1, S5_LANES), F32)
        scratch = [pltpu.VMEM((1, S5_LANES), F32), pltpu.VMEM((1, S5_LANES), F32)]
    in_specs += [
        pl.BlockSpec((1,) + wz.shape[1:], lambda b, t: (l, 0, 0, 0)),
        pl.BlockSpec((1,) + wy.shape[1:], lambda b, t: (l, 0, 0, 0)),
        ap_spec, ap_spec,
        pl.BlockSpec((1, 1, S5_WIDTH), lambda b, t: (l, 0, 0)),
        pl.BlockSpec((1, S5_WIDTH, S5_WIDTH), lambda b, t: (l, 0, 0)),
    ]
    args += [wz, wy, ap_re, ap_im, dsk3, wglu_b]
    o, hr, hi = pl.pallas_call(
        functools.partial(_s5_kernel, cs=cs, rows=rows, chain=chain),
        grid=(nb, tiles),
        in_specs=in_specs,
        out_specs=[tok, st_spec, st_spec],
        out_shape=[jax.ShapeDtypeStruct(u.shape, F32), st_shape, st_shape],
        scratch_shapes=scratch,
        compiler_params=_cparams(("arbitrary", "arbitrary")),
        name="s5_mix",
    )(*args)
    return o, hr, hi


def _softmax(s):
    e = jnp.exp(s - jnp.max(s, axis=-1, keepdims=True))
    return (e / jnp.sum(e, axis=-1, keepdims=True)).astype(BF16)


def _attend(q, mk_ref, mv_ref):
    heads = [slice(h * X_HEAD_DIM, (h + 1) * X_HEAD_DIM) for h in range(X_HEADS)]
    s = [_mm_nt(q[:, hs], mk_ref[0, 0, :, hs]) for hs in heads]
    p = [_softmax(sh) for sh in s]
    return jnp.concatenate([_mm(ph, mv_ref[0, 0, :, hs]) for ph, hs in zip(p, heads)], axis=-1)


_DT = X_HEAD_DIM // LANE


def _attend_tiled(q, mk_ref, mv_ref, nseg, row_seg):
    def sub(ref, i, h, dt):
        return ref[0, i, pl.ds(dt * X_HEADS + h, MEM_LEN, stride=X_HEADS * _DT), :]

    def pick(vals):
        out = vals[0]
        for i in range(1, nseg):
            out = jnp.where(row_seg == i, vals[i], out)
        return out

    qb = q.astype(BF16)
    qp = lambda h, dt: qb[:, h * X_HEAD_DIM + dt * LANE:h * X_HEAD_DIM + (dt + 1) * LANE]
    s = [[sum(_mm_nt(qp(h, dt), sub(mk_ref, i, h, dt)) for dt in range(_DT)) for i in range(nseg)]
         for h in range(X_HEADS)]
    p = [_softmax(pick(sh)) for sh in s]
    outs = [pick([_mm(p[h], sub(mv_ref, i, h, dt)) for i in range(nseg)])
            for h in range(X_HEADS) for dt in range(_DT)]
    return jnp.concatenate(outs, axis=-1)


def _post_kernel(x_ref, orw_ref, os5_ref, wout_ref, nx_ref, wq_ref, wo_ref, mk_ref, mv_ref, nf_ref,
                 o_ref, *, nseg, seg_rows, final):
    x = x_ref[...]
    os5 = jnp.concatenate([os5_ref[j] for j in range(S5_TILES)], axis=1)
    x1 = (x + jnp.dot(orw_ref[...].astype(BF16), wout_ref[0, 0:RWKV_WIDTH, :], preferred_element_type=F32)
          + jnp.dot(os5.astype(BF16), wout_ref[0, RWKV_WIDTH:, :], preferred_element_type=F32))
    xc = _rms(x1, nx_ref[0]).astype(BF16)
    q = jnp.dot(xc, wq_ref[0], preferred_element_type=F32) * (1.0 / math.sqrt(X_HEAD_DIM))
    if nseg == 1:
        att = _attend(q, mk_ref, mv_ref)
    else:
        row_seg = _iota((q.shape[0], 1), 0) >> int(math.log2(seg_rows))
        att = _attend_tiled(q, mk_ref, mv_ref, nseg, row_seg)
    x2 = x1 + jnp.dot(att.astype(BF16), wo_ref[0], preferred_element_type=F32)
    if final:
        x2 = _rms(x2, nf_ref[...])
    o_ref[...] = x2


def _post(x2d, orw, os5, wout_b, nx3, wq_b, wo_b, mk, mv, nf2, l, nb, nseg, seg_rows, tq, final):
    tiles = x2d.shape[0] // nb // tq
    tok = lambda w: pl.BlockSpec((tq, w), lambda b, t: (b * tiles + t, 0))
    tok_s5 = pl.BlockSpec((S5_TILES, tq, LANE), lambda b, t: (0, b * tiles + t, 0))
    if nseg == 1:
        mem = pl.BlockSpec((1, 1, MEM_LEN, D_MODEL), lambda b, t: (l, b, 0, 0))
    else:
        mem = pl.BlockSpec((1, nseg, MEM_LEN * X_HEADS * _DT, LANE), lambda b, t: (l, t, 0, 0))
    wsp = pl.BlockSpec((1, D_MODEL, D_MODEL), lambda b, t: (l, 0, 0))
    return pl.pallas_call(
        functools.partial(_post_kernel, nseg=nseg, seg_rows=seg_rows, final=final),
        grid=(nb, tiles),
        in_specs=[tok(D_MODEL), tok(RWKV_WIDTH), tok_s5, wsp,
                  pl.BlockSpec((1, 1, D_MODEL), lambda b, t: (l, 0, 0)), wsp, wsp, mem, mem,
                  pl.BlockSpec((1, D_MODEL), lambda b, t: (0, 0))],
        out_specs=tok(D_MODEL),
        out_shape=jax.ShapeDtypeStruct(x2d.shape, F32),
        compiler_params=_cparams(("arbitrary", "arbitrary")),
        name="post",
    )(x2d, orw, os5, wout_b, nx3, wq_b, wo_b, mk, mv, nf2)


def _trunk(x, states, mem_k, mem_v, W):
    batch, seq, _ = x.shape
    prompt = states is None
    x2d = x.reshape(batch * seq, D_MODEL)
    new_shift, new_wkv, new_re, new_im = [], [], [], []
    vfirst = None
    for l in range(DEPTH):
        rw_args = (W['mu'], W['vecs'], W['wla'], W['v1'], W['v2'], l)
        if prompt:
            orw, vfirst, us5, gs5, sh, wkv = _rwkv_prompt(x2d, W['norm_mix'], W['w_in'], vfirst, *rw_args,
                                                          batch, seq)
            os5, hr, hi = _s5_mix(us5, gs5, None, W['s5_prompt'], W['d_skip'], W['w_glu'], l, 8, batch)
            hr = hr.reshape(batch, S5_GROUPS, S5_STATE)
            hi = hi.reshape(batch, S5_GROUPS, S5_STATE)
            x2d = _post(x2d, orw, os5, W['w_out'], W['norm_x'], W['wq'], W['wo'], mem_k, mem_v,
                        W['norm_f'], l, batch, 1, seq, 512, l == DEPTH - 1)
        else:
            shift0, wkv0, re0, im0 = states
            psh, grw, us5, gs5 = _inproj(x2d, W['norm_mix'], W['w_in'], l)
            prev_rows = jnp.repeat(shift0[l], seq, axis=0)
            orw, vfirst, sh, wkv = _rwkv_sample(psh, grw, vfirst, prev_rows, wkv0, *rw_args, seq)
            h0 = (re0[l].reshape(batch, S5_LANES), im0[l].reshape(batch, S5_LANES))
            os5, hr, hi = _s5_mix(us5, gs5, h0, W['s5_sample'], W['d_skip'], W['w_glu'], l, seq, 1)
            hr = hr.reshape(batch, S5_GROUPS, S5_STATE)
            hi = hi.reshape(batch, S5_GROUPS, S5_STATE)
            nseg = 16 // seq
            x2d = _post(x2d, orw, os5, W['w_out'], W['norm_x'], W['wq'], W['wo'], mem_k, mem_v,
                        W['norm_f'], l, 1, nseg, seq, nseg * seq, l == DEPTH - 1)
        new_shift.append(sh)
        new_wkv.append(wkv)
        new_re.append(hr)
        new_im.append(hi)
    y = x2d.reshape(batch, seq, D_MODEL)
    return y, jnp.stack(new_shift), jnp.stack(new_wkv), jnp.stack(new_re), jnp.stack(new_im)


def kernel(x_prompt, x_sample, state_shift, state_wkv, state_s5_re, state_s5_im, cache_mem_k, cache_mem_v, mem_prompt, norm_mix, w_in, mu_shift, w0, w2, a0, a2, v0, v1, v2, k_k, k_a, r_k, gn_w, gn_b, lam_re, lam_im, log_dt, b_re, b_im, c_re, c_im, d_skip, w_glu, w_out, norm_x, norm_mem, wq, wk, wv, wo, norm_f):
    L = DEPTH
    bp, mp = mem_prompt.shape[0], mem_prompt.shape[1]
    zpad = jnp.zeros((1, RWKV_WIDTH), F32)
    vecs = jnp.stack([w0, a0, k_k, k_a, gn_w, gn_b, r_k.reshape(L, RWKV_WIDTH),
                      jnp.concatenate([zpad, v0], axis=0)], axis=1)
    zl = jnp.zeros((L, LORA_W, RWKV_WIDTH), F32)
    wla = jnp.concatenate([jnp.concatenate([w2, zl], axis=2),
                           jnp.concatenate([zl, a2], axis=2)], axis=1).astype(BF16)
    tr = lambda t: jnp.transpose(t, (0, 3, 1, 2)).reshape(L, S5_GROUP, S5_LANES)
    s5_in = (lam_re.reshape(L, 1, S5_LANES), lam_im.reshape(L, 1, S5_LANES),
             jnp.repeat(log_dt, S5_STATE, axis=1).reshape(L, 1, S5_LANES),
             tr(b_re), tr(b_im),
             jnp.transpose(c_re, (0, 2, 1, 3)).reshape(L, S5_GROUP, S5_LANES),
             jnp.transpose(c_im, (0, 2, 1, 3)).reshape(L, S5_GROUP, S5_LANES))
    W = dict(
        norm_mix=norm_mix.reshape(L, 1, D_MODEL), w_in=w_in.astype(BF16),
        mu=mu_shift.reshape(L, 1, SHIFT_COLS), vecs=vecs, wla=wla,
        v1=v1.astype(BF16), v2=v2.astype(BF16),
        s5_prompt=_s5_prep(*s5_in, 8), s5_sample=_s5_prep(*s5_in, x_sample.shape[1]),
        d_skip=d_skip.reshape(L, 1, S5_WIDTH), w_glu=w_glu.astype(BF16),
        w_out=w_out.astype(BF16), norm_x=norm_x.reshape(L, 1, D_MODEL),
        wq=wq.astype(BF16), wo=wo.astype(BF16), norm_f=norm_f.reshape(1, D_MODEL),
    )
    pk, pv = _memkv(mem_prompt.reshape(bp * mp, D_MODEL), norm_mem, wk.astype(BF16), wv.astype(BF16))
    pk = pk.reshape(L, bp, mp, D_MODEL)
    pv = pv.reshape(L, bp, mp, D_MODEL)
    y_prompt, p_shift, p_wkv, p_re, p_im = _trunk(x_prompt, None, pk, pv, W)
    db = x_sample.shape[0]
    tiled = lambda c: jnp.transpose(c.reshape(L, db, MEM_LEN, X_HEADS, _DT, LANE),
                                    (0, 1, 2, 4, 3, 5)).reshape(L, db, MEM_LEN * _DT * X_HEADS, LANE)
    ck = tiled(cache_mem_k)
    cv = tiled(cache_mem_v)
    y_sample, s_shift, s_wkv, s_re, s_im = _trunk(
        x_sample, (state_shift, state_wkv, state_s5_re, state_s5_im), ck, cv, W)
    return (y_prompt, y_sample, p_shift, p_wkv, p_re, p_im,
            pk.reshape(L, bp, mp, X_HEADS, X_HEAD_DIM), pv.reshape(L, bp, mp, X_HEADS, X_HEAD_DIM),
            s_shift, s_wkv, s_re, s_im)
```

```python
import functools
import math

import jax
import jax.numpy as jnp
from jax import lax
from jax.experimental import pallas as pl
from jax.experimental.pallas import tpu as pltpu

F32 = jnp.float32
BF16 = jnp.bfloat16

D_MODEL = 1024
DEPTH = 4
RWKV_WIDTH = 512
RWKV_HEAD = 64
RWKV_HEADS = 8
LORA_W = 64
LORA_A = 64
LORA_V = 32
S5_WIDTH = 512
S5_GROUP = 16
S5_GROUPS = 32
S5_STATE = 64
S5_LANES = S5_GROUPS * S5_STATE
MEM_LEN = 256
X_HEADS = 4
X_HEAD_DIM = 256
SHIFT_COLS = 3 * RWKV_WIDTH + LORA_W + LORA_A
IN_COLS = SHIFT_COLS + RWKV_WIDTH + 2 * S5_WIDTH
NORM_EPS = 1e-6
GN_EPS = 64e-5

LANE = 128
SUBLANES = 8
WKV_CHUNK = 64
HALF_HEADS = 4
WKV_UNROLL = 4
S5_TILES = S5_WIDTH // LANE
GROUPS_PER_TILE = LANE // S5_GROUP
TILE_STATES = GROUPS_PER_TILE * S5_STATE
S5_SCAN_ROWS = 8
VMEM_LIMIT = 56 * 1024 * 1024


def _cparams(sem):
    return pltpu.CompilerParams(dimension_semantics=sem, vmem_limit_bytes=VMEM_LIMIT)


def _mm(a, b):
    return jnp.dot(a.astype(BF16), b.astype(BF16), preferred_element_type=F32)


def _mm_nt(a, b):
    return lax.dot_general(a.astype(BF16), b.astype(BF16), (((1,), (1,)), ((), ())),
                           preferred_element_type=F32)


def _mm_tn(a, b):
    return lax.dot_general(a.astype(BF16), b.astype(BF16), (((0,), (0,)), ((), ())),
                           preferred_element_type=F32)


def _mm_f32(a, b):
    return jnp.dot(a, b, preferred_element_type=F32, precision=lax.Precision.HIGHEST)


def _mm_nt_x3(a, b):
    a_hi, b_hi = a.astype(BF16), b.astype(BF16)
    a_lo = (a - a_hi.astype(F32)).astype(BF16)
    b_lo = (b - b_hi.astype(F32)).astype(BF16)
    return _mm_nt(a_hi, b_hi) + _mm_nt(a_hi, b_lo) + _mm_nt(a_lo, b_hi)


def _rms(x, g):
    return x * lax.rsqrt(jnp.mean(x * x, axis=-1, keepdims=True) + NORM_EPS) * g


def _sigmoid(x):
    return 1.0 / (1.0 + jnp.exp(-x))


def _silu(x):
    return x * _sigmoid(x)


def _iota(shape, dim):
    return lax.broadcasted_iota(jnp.int32, shape, dim)


_DT = X_HEAD_DIM // LANE
_MEM_SUB = X_HEADS * _DT


def _memkv_kernel(m_ref, g_ref, wk_ref, wv_ref, k_ref, v_ref, kt_ref, vt_ref):
    mn = _rms(m_ref[...], g_ref[0]).astype(BF16)
    tm = m_ref.shape[0]
    for w_ref, o_ref, t_ref in ((wk_ref, k_ref, kt_ref), (wv_ref, v_ref, vt_ref)):
        kv = jnp.dot(mn, w_ref[0], preferred_element_type=F32)
        o_ref[0] = kv
        for h in range(X_HEADS):
            for dt in range(_DT):
                lo = h * X_HEAD_DIM + dt * LANE
                t_ref[0, pl.ds(dt * X_HEADS + h, tm, stride=_MEM_SUB), :] = kv[:, lo:lo + LANE]


def _memkv(mem2d, norm_mem, wk_b, wv_b):
    rows = mem2d.shape[0]
    tm = 512
    out = jax.ShapeDtypeStruct((DEPTH, rows, D_MODEL), F32)
    out_t = jax.ShapeDtypeStruct((DEPTH, rows * _MEM_SUB, LANE), F32)
    nat = pl.BlockSpec((1, tm, D_MODEL), lambda l, i: (l, i, 0))
    til = pl.BlockSpec((1, tm * _MEM_SUB, LANE), lambda l, i: (l, i, 0))
    return pl.pallas_call(
        _memkv_kernel,
        grid=(DEPTH, rows // tm),
        in_specs=[
            pl.BlockSpec((tm, D_MODEL), lambda l, i: (i, 0)),
            pl.BlockSpec((1, 1, D_MODEL), lambda l, i: (l, 0, 0)),
            pl.BlockSpec((1, D_MODEL, D_MODEL), lambda l, i: (l, 0, 0)),
            pl.BlockSpec((1, D_MODEL, D_MODEL), lambda l, i: (l, 0, 0)),
        ],
        out_specs=[nat, nat, til, til],
        out_shape=[out, out, out_t, out_t],
        compiler_params=_cparams(("arbitrary", "arbitrary")),
        name="memkv",
    )(mem2d, norm_mem.reshape(DEPTH, 1, D_MODEL), wk_b, wv_b)


_IN_SPLITS = (0, SHIFT_COLS, SHIFT_COLS + RWKV_WIDTH, SHIFT_COLS + RWKV_WIDTH + S5_WIDTH, IN_COLS)


def _inproj_kernel(x_ref, g_ref, w_ref, psh_ref, grw_ref, us5_ref, gs5_ref):
    xn = _rms(x_ref[...], g_ref[0]).astype(BF16)
    proj = lambda lo, hi: jnp.dot(xn, w_ref[0, :, lo:hi], preferred_element_type=F32)
    psh_ref[...] = proj(_IN_SPLITS[0], _IN_SPLITS[1])
    grw_ref[...] = proj(_IN_SPLITS[1], _IN_SPLITS[2])
    for o_ref, lo in ((us5_ref, _IN_SPLITS[2]), (gs5_ref, _IN_SPLITS[3])):
        for j in range(S5_TILES):
            o_ref[j] = proj(lo + j * LANE, lo + (j + 1) * LANE)


def _inproj(x2d, norm_mix3, w_in_b, l):
    rows = x2d.shape[0]
    tm = 256
    s5_spec = pl.BlockSpec((S5_TILES, tm, LANE), lambda i: (0, i, 0))
    s5_shape = jax.ShapeDtypeStruct((S5_TILES, rows, LANE), F32)
    return pl.pallas_call(
        _inproj_kernel,
        grid=(rows // tm,),
        in_specs=[
            pl.BlockSpec((tm, D_MODEL), lambda i: (i, 0)),
            pl.BlockSpec((1, 1, D_MODEL), lambda i: (l, 0, 0)),
            pl.BlockSpec((1, D_MODEL, IN_COLS), lambda i: (l, 0, 0)),
        ],
        out_specs=[pl.BlockSpec((tm, SHIFT_COLS), lambda i: (i, 0)),
                   pl.BlockSpec((tm, RWKV_WIDTH), lambda i: (i, 0)), s5_spec, s5_spec],
        out_shape=[jax.ShapeDtypeStruct((rows, SHIFT_COLS), F32),
                   jax.ShapeDtypeStruct((rows, RWKV_WIDTH), F32), s5_shape, s5_shape],
        compiler_params=_cparams(("arbitrary",)),
        name="inproj",
    )(x2d, norm_mix3, w_in_b)


_V_W0, _V_A0, _V_KK, _V_KA, _V_GNW, _V_GNB, _V_RK, _V_V0 = range(8)


def _rwkv_prep(ps, vfirst, vecs, wla, v1, v2):
    r = ps[:, 0:RWKV_WIDTH]
    k = ps[:, RWKV_WIDTH:2 * RWKV_WIDTH]
    v = ps[:, 2 * RWKV_WIDTH:3 * RWKV_WIDTH]
    lora_in = ps[:, 3 * RWKV_WIDTH:SHIFT_COLS]
    lane = _iota(lora_in.shape, 1)
    lora_in = jnp.where(lane < LORA_W, jnp.tanh(lora_in), lora_in)
    lora = _mm(lora_in, wla)
    wpre = vecs[_V_W0:_V_W0 + 1] + lora[:, 0:RWKV_WIDTH]
    z = -wpre
    softplus = jnp.maximum(z, 0.0) + jnp.log(1.0 + jnp.exp(-jnp.abs(z)))
    logdecay = -jnp.exp(-softplus - 0.5)
    a = _sigmoid(vecs[_V_A0:_V_A0 + 1] + lora[:, RWKV_WIDTH:2 * RWKV_WIDTH])
    if vfirst is not None:
        gate = _sigmoid(vecs[_V_V0:_V_V0 + 1] + _mm(_mm(v, v1), v2))
        v = v + (vfirst - v) * gate
    kku = k * vecs[_V_KK:_V_KK + 1]
    kmod = k * (1.0 + (a - 1.0) * vecs[_V_KA:_V_KA + 1])
    return r, kku, kmod, v, a, logdecay


def _wkv_chunk(lw, load_head, states, seg_len, vecs):
    C = WKV_CHUNK
    shift = int(math.log2(seg_len))
    per_grp = SUBLANES // seg_len
    ngrp = C // SUBLANES
    row = _iota((C, C), 0)
    col = _iota((C, C), 1)
    same = (row >> shift) == (col >> shift)
    incl = jnp.logical_and(col <= row, same)
    strict = jnp.logical_and(col < row, same)
    cum = _mm_f32(incl.astype(F32), lw)
    tot = _mm_f32(same.astype(F32), lw)
    g_inc_all = jnp.exp(cum)
    g_exc_all = jnp.exp(cum - lw)
    g_inv_all = jnp.exp(-cum)
    g_end_all = jnp.exp(tot - cum)
    g_tot_all = jnp.exp(tot)
    eye = (row == col).astype(F32)
    nsq = max(shift - 1, 0)
    seg16 = (_iota((2 * SUBLANES, 1), 0) & (SUBLANES - 1)) >> shift

    hd = []
    for h in range(RWKV_HEADS):
        sl = slice(h * RWKV_HEAD, (h + 1) * RWKV_HEAD)
        r, kku, kmod, v, a = load_head(h)
        ss = jnp.sum(kku * kku, axis=-1, keepdims=True)
        kkn = kku * lax.rsqrt(jnp.maximum(ss, 1e-24))
        b = kkn * a
        qk = kkn * g_exc_all[:, sl]
        qr = r * g_inc_all[:, sl]
        hd.append(dict(sl=sl, r=r, kmod=kmod, v=v, qk=qk, qr=qr,
                       q2=jnp.concatenate([qk, qr], axis=0).astype(BF16),
                       kd=kmod * g_inv_all[:, sl], bd=b * g_inv_all[:, sl],
                       kc=kmod * g_end_all[:, sl], bc=b * g_end_all[:, sl]))
    for d in hd:
        d['ak'] = _mm_nt(d['q2'], d['kd'])
        d['ab'] = _mm_nt(d['q2'], d['bd'])
    for d in hd:
        d['akk'] = jnp.where(strict, d['ak'][:C], 0.0)
        d['ark'] = jnp.where(incl, d['ak'][C:], 0.0)
        d['arb'] = jnp.where(incl, d['ab'][C:], 0.0)
        d['n'] = -jnp.where(strict, d['ab'][:C], 0.0)
        d['tinv'] = eye + d['n']
    for _ in range(nsq):
        for d in hd:
            d['n'] = _mm(d['n'], d['n'])
        for d in hd:
            d['tinv'] = d['tinv'] + _mm(d['tinv'], d['n'])
    for d in hd:
        d['av'] = _mm(jnp.concatenate([d['akk'], d['ark']], axis=0), d['v'])
    for d in hd:
        d['u_ind'] = _mm(d['tinv'], d['av'][:C])
        d['tq'] = _mm(d['tinv'], d['qk'])
    for h, d in enumerate(hd):
        us, yq = [], []
        for g in range(ngrp):
            rs = slice(g * SUBLANES, (g + 1) * SUBLANES)
            lhs = jnp.concatenate([d['tq'][rs], d['qr'][rs]], axis=0).astype(BF16)
            qs = None
            for s in range(per_grp):
                o = _mm_nt(lhs, states[h][g * per_grp + s])
                qs = o if qs is None else jnp.where(seg16 == s, o, qs)
            us.append(d['u_ind'][rs] + qs[:SUBLANES])
            yq.append(qs[SUBLANES:])
        d['u'] = jnp.concatenate(us, axis=0)
        d['yq'] = jnp.concatenate(yq, axis=0)
    for d in hd:
        d['y'] = d['yq'] + d['av'][C:] - _mm(d['arb'], d['u'])
    new_states = []
    for h, d in enumerate(hd):
        hs = []
        for g in range(ngrp):
            rs = slice(g * SUBLANES, (g + 1) * SUBLANES)
            vu = jnp.concatenate([d['v'][rs], d['u'][rs]], axis=0)
            kb = jnp.concatenate([d['kc'][rs], -d['bc'][rs]], axis=0).astype(BF16)
            for s in range(per_grp):
                i = g * per_grp + s
                gt = g_tot_all[i * seg_len:i * seg_len + 1, d['sl']]
                hs.append(states[h][i] * gt + _mm_tn(jnp.where(seg16 == s, vu, 0.0), kb))
        new_states.append(hs)
    ys = []
    for d in hd:
        y, sl = d['y'], d['sl']
        mu = jnp.mean(y, axis=-1, keepdims=True)
        yc = y - mu
        var = jnp.mean(yc * yc, axis=-1, keepdims=True)
        yn = yc * lax.rsqrt(var + GN_EPS) * vecs[_V_GNW:_V_GNW + 1, sl] + vecs[_V_GNB:_V_GNB + 1, sl]
        bonus = jnp.sum(d['r'] * d['kmod'] * vecs[_V_RK:_V_RK + 1, sl], axis=-1, keepdims=True)
        ys.append(yn + bonus * d['v'])
    return jnp.concatenate(ys, axis=-1), new_states


def _rwkv_prompt_kernel(*refs, first_layer, tt):
    if first_layer:
        (x_ref, nm_ref, win_ref, mu_ref, vecs_ref, wla_ref, v1_ref, v2_ref,
         o_ref, vf_out_ref, us5_ref, gs5_ref, shift_out_ref, wkv_out_ref,
         carry_ref, s_ref, r_s, kkn_s, kmod_s, v_s, b_s, lw_s, y_s) = refs
        vf_ref = None
    else:
        (x_ref, nm_ref, win_ref, vf_ref, mu_ref, vecs_ref, wla_ref, v1_ref, v2_ref,
         o_ref, us5_ref, gs5_ref, shift_out_ref, wkv_out_ref,
         carry_ref, s_ref, r_s, kkn_s, kmod_s, v_s, b_s, lw_s, y_s) = refs
        vf_out_ref = None
    t = pl.program_id(1)

    @pl.when(t == 0)
    def _():
        carry_ref[...] = jnp.zeros_like(carry_ref)
        s_ref[...] = jnp.zeros_like(s_ref)

    xn = _rms(x_ref[0], nm_ref[0]).astype(BF16)
    proj = lambda lo, hi: jnp.dot(xn, win_ref[0, :, lo:hi], preferred_element_type=F32)
    p = proj(_IN_SPLITS[0], _IN_SPLITS[1])
    o_ref[0] = proj(_IN_SPLITS[1], _IN_SPLITS[2])
    for s5_ref, lo in ((us5_ref, _IN_SPLITS[2]), (gs5_ref, _IN_SPLITS[3])):
        for j in range(S5_TILES):
            s5_ref[j] = proj(lo + j * LANE, lo + (j + 1) * LANE)
    rowi = _iota(p.shape, 0)
    prev = jnp.where(rowi == 0, carry_ref[...], pltpu.roll(p, 1, 0))
    ps = p + (prev - p) * mu_ref[0]
    last = p[tt - 1:tt, :]
    carry_ref[...] = last
    shift_out_ref[0] = last

    vecs = vecs_ref[0]
    vfirst = None if first_layer else vf_ref[0]
    r, kku, kmod, v, a, lw = _rwkv_prep(ps, vfirst, vecs, wla_ref[0], v1_ref[0], v2_ref[0])
    if first_layer:
        vf_out_ref[0] = v

    hw = HALF_HEADS * RWKV_HEAD
    bd_bool = (_iota((hw, hw), 0) >> 6) == (_iota((hw, hw), 1) >> 6)
    bd_ones = bd_bool.astype(F32).astype(BF16)

    def head_sums(x):
        n = x.shape[0]
        hi = x.astype(BF16)
        lo = (x - hi.astype(F32)).astype(BF16)
        st = jnp.concatenate([hi, lo], axis=0)
        outs = []
        for hf in range(2):
            s2 = jnp.dot(st[:, hf * hw:(hf + 1) * hw], bd_ones, preferred_element_type=F32)
            outs.append(s2[:n] + s2[n:])
        return jnp.concatenate(outs, axis=1)

    kkn = kku * lax.rsqrt(jnp.maximum(head_sums(kku * kku), 1e-24))
    r_s[...] = r
    kkn_s[...] = kkn
    kmod_s[...] = kmod
    v_s[...] = v
    b_s[...] = kkn * a
    lw_s[...] = lw

    C = WKV_CHUNK
    trow = _iota((C, hw), 0)
    tcol = _iota((C, hw), 1) & (RWKV_HEAD - 1)
    strict = tcol < trow
    incl = tcol <= trow
    eye = (tcol == trow).astype(F32)
    crow = _iota((C, RWKV_WIDTH), 0)

    def bd(x):
        return jnp.concatenate([x.astype(BF16)] * HALF_HEADS, axis=0) * bd_ones

    def group(g, carry):
        chains = []
        for k in range(WKV_UNROLL):
            rows = pl.ds(pl.multiple_of((g * WKV_UNROLL + k) * C, C), C)
            lwc = lw_s[rows, :]
            cum = lwc
            for d in (1, 2, 4):
                cum = cum + jnp.where(crow >= d, pltpu.roll(cum, d, 0), 0.0)
            for d in (8, 16, 32):
                cum = jnp.concatenate([cum[:d], cum[d:] + cum[:C - d]], axis=0)
            tot = cum[C - 1:C, :]
            g_inc = jnp.exp(cum)
            g_exc = jnp.exp(cum - lwc)
            g_inv = jnp.exp(-cum)
            g_end = jnp.exp(tot - cum)
            g_tot = jnp.exp(tot)
            rr, kk, km, bb, vv = (r_s[rows, :], kkn_s[rows, :], kmod_s[rows, :], b_s[rows, :],
                                  v_s[rows, :])
            qk_a, qr_a = kk * g_exc, rr * g_inc
            kd_a, bd_a = km * g_inv, bb * g_inv
            kc_a, bc_a = km * g_end, bb * g_end
            for hf in range(2):
                ls = slice(hf * hw, (hf + 1) * hw)
                chains.append(dict(k=k, hf=hf, rows=rows, qk=qk_a[:, ls], qr=qr_a[:, ls], kd=kd_a[:, ls],
                                   bd=bd_a[:, ls], kc=kc_a[:, ls], bc=bc_a[:, ls], v=vv[:, ls],
                                   g_tot=g_tot[:, ls]))
        for ch in chains:
            ch['q2'] = jnp.concatenate([ch['qk'], ch['qr']], axis=0).astype(BF16)
            ch['ak'] = _mm_nt(ch['q2'], bd(ch['kd']))
        for ch in chains:
            ab = _mm_nt(ch['q2'], bd(ch['bd']))
            ch['akk'] = jnp.where(strict, ch['ak'][:C], 0.0)
            ch['ark'] = jnp.where(incl, ch['ak'][C:], 0.0)
            ch['arb'] = jnp.where(incl, ab[C:], 0.0)
            ch['n'] = -jnp.where(strict, ab[:C], 0.0)
            ch['tinv'] = eye + ch['n']
        for ch in chains:
            ch['npow'] = _mm(ch['n'], bd(ch['n']))
        for _ in range(4):
            for ch in chains:
                st = _mm(jnp.concatenate([ch['tinv'], ch['npow']], axis=0), bd(ch['npow']))
                ch['tinv'] = ch['tinv'] + st[:C]
                ch['npow'] = st[C:]
        for ch in chains:
            ch['av'] = _mm(jnp.concatenate([ch['akk'], ch['ark']], axis=0), bd(ch['v']))
        for ch in chains:
            ch['tinv'] = ch['tinv'] + _mm(ch['tinv'], bd(ch['npow']))
        for ch in chains:
            ch['u_ind'] = _mm(ch['tinv'], bd(ch['av'][:C]))
            ch['tq'] = _mm(ch['tinv'], bd(ch['qk']))
        for ch in chains:
            ch['gc'] = jnp.where(bd_bool, _mm_tn(ch['tq'], ch['bc']), 0.0).astype(BF16)
            ch['nc'] = jnp.where(bd_bool, _mm_tn(jnp.concatenate([ch['v'], ch['u_ind']], axis=0),
                                                 jnp.concatenate([ch['kc'], -ch['bc']], axis=0)), 0.0)
        state = [s_ref[0], s_ref[1]]
        for ch in chains:
            s_old = state[ch['hf']]
            ch['s_b'] = s_old.astype(BF16)
            state[ch['hf']] = s_old * ch['g_tot'] - _mm(ch['s_b'], ch['gc']) + ch['nc']
        s_ref[0] = state[0]
        s_ref[1] = state[1]
        for ch in chains:
            ch['qs'] = _mm_nt(jnp.concatenate([ch['tq'], ch['qr']], axis=0), ch['s_b'])
        for ch in chains:
            u = ch['u_ind'] + ch['qs'][:C]
            ch['y'] = ch['qs'][C:] + ch['av'][C:] - _mm(ch['arb'], bd(u))
        for k in range(WKV_UNROLL):
            y_s[chains[2 * k]['rows'], :] = jnp.concatenate([chains[2 * k]['y'], chains[2 * k + 1]['y']], axis=1)
        return carry

    lax.fori_loop(0, tt // (C * WKV_UNROLL), group, 0)

    y = y_s[...]
    inv_n = 1.0 / RWKV_HEAD
    mu = head_sums(y) * inv_n
    yc = y - mu
    var = head_sums(yc * yc) * inv_n
    yn = yc * lax.rsqrt(var + GN_EPS) * vecs[_V_GNW:_V_GNW + 1] + vecs[_V_GNB:_V_GNB + 1]
    rr, km, vv = r_s[...], kmod_s[...], v_s[...]
    bonus = head_sums(rr * km * vecs[_V_RK:_V_RK + 1])
    o_ref[0] = (yn + bonus * vv) * _silu(o_ref[0])

    @pl.when(t == pl.num_programs(1) - 1)
    def _():
        for h in range(RWKV_HEADS):
            o = (h % HALF_HEADS) * RWKV_HEAD
            wkv_out_ref[0, h] = s_ref[h // HALF_HEADS, o:o + RWKV_HEAD, o:o + RWKV_HEAD]


def _rwkv_prompt(x2d, norm_mix3, w_in_b, vfirst, mu3, vecs, wla, v1, v2, l, batch, seq):
    tt = 512
    tiles = seq // tt
    first = vfirst is None
    lv = max(l - 1, 0)
    tok = lambda w: pl.BlockSpec((1, tt, w), lambda b, t: (b, t, 0))
    s5_spec = pl.BlockSpec((S5_TILES, tt, LANE), lambda b, t: (0, b * tiles + t, 0))
    s5_shape = jax.ShapeDtypeStruct((S5_TILES, batch * seq, LANE), F32)
    in_specs = [tok(D_MODEL),
                pl.BlockSpec((1, 1, D_MODEL), lambda b, t: (l, 0, 0)),
                pl.BlockSpec((1, D_MODEL, IN_COLS), lambda b, t: (l, 0, 0))]
    args = [x2d.reshape(batch, seq, D_MODEL), norm_mix3, w_in_b]
    if not first:
        in_specs.append(tok(RWKV_WIDTH))
        args.append(vfirst.reshape(batch, seq, RWKV_WIDTH))
    in_specs += [
        pl.BlockSpec((1, 1, SHIFT_COLS), lambda b, t: (l, 0, 0)),
        pl.BlockSpec((1, 8, RWKV_WIDTH), lambda b, t: (l, 0, 0)),
        pl.BlockSpec((1, LORA_W + LORA_A, 2 * RWKV_WIDTH), lambda b, t: (l, 0, 0)),
        pl.BlockSpec((1, RWKV_WIDTH, LORA_V), lambda b, t: (lv, 0, 0)),
        pl.BlockSpec((1, LORA_V, RWKV_WIDTH), lambda b, t: (lv, 0, 0)),
    ]
    args += [mu3, vecs, wla, v1, v2]
    out_specs = [tok(RWKV_WIDTH)]
    out_shape = [jax.ShapeDtypeStruct((batch, seq, RWKV_WIDTH), F32)]
    if first:
        out_specs.append(tok(RWKV_WIDTH))
        out_shape.append(jax.ShapeDtypeStruct((batch, seq, RWKV_WIDTH), F32))
    out_specs += [
        s5_spec, s5_spec,
        pl.BlockSpec((1, 1, SHIFT_COLS), lambda b, t: (b, 0, 0)),
        pl.BlockSpec((1, RWKV_HEADS, RWKV_HEAD, RWKV_HEAD), lambda b, t: (b, 0, 0, 0)),
    ]
    out_shape += [
        s5_shape, s5_shape,
        jax.ShapeDtypeStruct((batch, 1, SHIFT_COLS), F32),
        jax.ShapeDtypeStruct((batch, RWKV_HEADS, RWKV_HEAD, RWKV_HEAD), F32),
    ]
    scratch = [
        pltpu.VMEM((1, SHIFT_COLS), F32),
        pltpu.VMEM((2, HALF_HEADS * RWKV_HEAD, HALF_HEADS * RWKV_HEAD), F32),
    ] + [pltpu.VMEM((tt, RWKV_WIDTH), F32) for _ in range(7)]
    outs = pl.pallas_call(
        functools.partial(_rwkv_prompt_kernel, first_layer=first, tt=tt),
        grid=(batch, seq // tt),
        in_specs=in_specs,
        out_specs=out_specs,
        out_shape=out_shape,
        scratch_shapes=scratch,
        compiler_params=_cparams(("arbitrary", "arbitrary")),
        name="rwkv_prompt",
    )(*args)
    if first:
        o, vf, us5, gs5, sh, wkv = outs
    else:
        (o, us5, gs5, sh, wkv), vf = outs, vfirst
    return (o.reshape(batch * seq, RWKV_WIDTH), vf.reshape(batch * seq, RWKV_WIDTH), us5, gs5,
            sh[:, 0], wkv)


def _rwkv_sample_kernel(*refs, first_layer, seq):
    if first_layer:
        (psh_ref, grw_ref, prev_ref, wkv_ref, mu_ref, vecs_ref, wla_ref, v1_ref, v2_ref,
         o_ref, vf_out_ref, wkv_out_ref) = refs
        vf_ref = None
    else:
        (psh_ref, grw_ref, vf_ref, prev_ref, wkv_ref, mu_ref, vecs_ref, wla_ref, v1_ref, v2_ref,
         o_ref, wkv_out_ref) = refs
        vf_out_ref = None
    nseq = WKV_CHUNK // seq
    p = psh_ref[...]
    rowi = _iota(p.shape, 0)
    prev = jnp.where((rowi & (seq - 1)) == 0, prev_ref[...], pltpu.roll(p, 1, 0))
    ps = p + (prev - p) * mu_ref[0]

    vecs = vecs_ref[0]
    vfirst = None if first_layer else vf_ref[...]
    r, kku, kmod, v, a, lw = _rwkv_prep(ps, vfirst, vecs, wla_ref[0], v1_ref[0], v2_ref[0])
    if first_layer:
        vf_out_ref[...] = v

    def load_head(h):
        sl = slice(h * RWKV_HEAD, (h + 1) * RWKV_HEAD)
        return (r[:, sl], kku[:, sl], kmod[:, sl], v[:, sl], a[:, sl])

    states = [[wkv_ref[0, s, h] for s in range(nseq)] for h in range(RWKV_HEADS)]
    y, new_states = _wkv_chunk(lw, load_head, states, seq, vecs)
    for h in range(RWKV_HEADS):
        for s in range(nseq):
            wkv_out_ref[0, s, h] = new_states[h][s]
    o_ref[...] = y * _silu(grw_ref[...])


def _rwkv_sample(psh, grw, vfirst, prev_rows, wkv_all, mu3, vecs, wla, v1, v2, l, seq):
    rows = psh.shape[0]
    nseq = WKV_CHUNK // seq
    first = vfirst is None
    lv = max(l - 1, 0)
    tok = lambda w: pl.BlockSpec((WKV_CHUNK, w), lambda i: (i, 0))
    st = pl.BlockSpec((1, nseq, RWKV_HEADS, RWKV_HEAD, RWKV_HEAD), lambda i: (l, i, 0, 0, 0))
    in_specs = [tok(SHIFT_COLS), tok(RWKV_WIDTH)]
    args = [psh, grw]
    if not first:
        in_specs.append(tok(RWKV_WIDTH))
        args.append(vfirst)
    in_specs += [
        tok(SHIFT_COLS), st,
        pl.BlockSpec((1, 1, SHIFT_COLS), lambda i: (l, 0, 0)),
        pl.BlockSpec((1, 8, RWKV_WIDTH), lambda i: (l, 0, 0)),
        pl.BlockSpec((1, LORA_W + LORA_A, 2 * RWKV_WIDTH), lambda i: (l, 0, 0)),
        pl.BlockSpec((1, RWKV_WIDTH, LORA_V), lambda i: (lv, 0, 0)),
        pl.BlockSpec((1, LORA_V, RWKV_WIDTH), lambda i: (lv, 0, 0)),
    ]
    wkv_arg = len(args) + 1
    args += [prev_rows, wkv_all, mu3, vecs, wla, v1, v2]
    out_specs = [tok(RWKV_WIDTH)]
    out_shape = [jax.ShapeDtypeStruct((rows, RWKV_WIDTH), F32)]
    if first:
        out_specs.append(tok(RWKV_WIDTH))
        out_shape.append(jax.ShapeDtypeStruct((rows, RWKV_WIDTH), F32))
    out_specs.append(st)
    out_shape.append(jax.ShapeDtypeStruct(wkv_all.shape, F32))
    outs = pl.pallas_call(
        functools.partial(_rwkv_sample_kernel, first_layer=first, seq=seq),
        grid=(rows // WKV_CHUNK,),
        in_specs=in_specs,
        out_specs=out_specs,
        out_shape=out_shape,
        input_output_aliases={wkv_arg: len(out_shape) - 1},
        compiler_params=_cparams(("arbitrary",)),
        name="rwkv_sample",
    )(*args)
    if first:
        o, vf, wkv = outs
    else:
        (o, wkv), vf = outs, vfirst
    sh = psh.reshape(rows // seq, seq, SHIFT_COLS)[:, seq - 1]
    return o, vf, sh, wkv


def _s5_prep_kernel(lam_re_ref, lam_im_ref, ldt_ref, bt_re_ref, bt_im_ref, ct_re_ref, ct_im_ref,
                    wz_ref, wy_ref, ap_re_ref, ap_im_ref, *, cs):
    n = TILE_STATES
    row_g = _iota((LANE, n), 0) >> 4
    lane_g = _iota((LANE, n), 1) >> 6
    emb_mask = row_g == lane_g
    bd_mask = (_iota((LANE, LANE), 0) >> 4) == (_iota((LANE, LANE), 1) >> 4)

    def rep(x):
        return jnp.concatenate([x] * GROUPS_PER_TILE, axis=0)

    def embed(x):
        return jnp.where(emb_mask, rep(x), 0.0)

    for j in range(S5_TILES):
        ls = slice(j * n, (j + 1) * n)
        lr = lam_re_ref[0, :, ls]
        li = lam_im_ref[0, :, ls]
        dt = jnp.exp(ldt_ref[0, :, ls])
        mag = jnp.exp(lr * dt)
        a_re = mag * jnp.cos(li * dt)
        a_im = mag * jnp.sin(li * dt)
        q_re = a_re - 1.0
        q_im = a_im
        den = lr * lr + li * li
        f_re = (q_re * lr + q_im * li) / den
        f_im = (q_im * lr - q_re * li) / den
        b_re = bt_re_ref[0, :, ls]
        b_im = bt_im_ref[0, :, ls]
        bb_re = f_re * b_re - f_im * b_im
        bb_im = f_re * b_im + f_im * b_re
        c_re = ct_re_ref[0, :, ls]
        c_im = ct_im_ref[0, :, ls]
        pw = [(jnp.ones_like(a_re), jnp.zeros_like(a_re))]
        for _ in range(cs):
            pr, pi = pw[-1]
            pw.append((pr * a_re - pi * a_im, pr * a_im + pi * a_re))
        e = [(pr * bb_re - pi * bb_im, pr * bb_im + pi * bb_re) for pr, pi in pw[:cs]]
        wz_ref[0, j, :, 0:n] = jnp.concatenate(
            [embed(e[cs - 1 - s][0]) for s in range(cs)], axis=0).astype(BF16)
        wz_ref[0, j, :, n:2 * n] = jnp.concatenate(
            [embed(e[cs - 1 - s][1]) for s in range(cs)], axis=0).astype(BF16)
        wc_re = jnp.concatenate(
            [embed(c_re * pw[s + 1][0] - c_im * pw[s + 1][1]) for s in range(cs)], axis=0)
        wc_im = jnp.concatenate(
            [embed(-(c_re * pw[s + 1][1] + c_im * pw[s + 1][0])) for s in range(cs)], axis=0)
        wy_ref[0, j, 0:n, :] = wc_re.T.astype(BF16)
        wy_ref[0, j, n:2 * n, :] = wc_im.T.astype(BF16)
        ce_re = embed(c_re)
        ce_im = embed(c_im)
        dt_blk = [jnp.where(bd_mask, _mm_nt_x3(rep(er), ce_re) - _mm_nt_x3(rep(ei), ce_im), 0.0)
                  for er, ei in e]
        zero = jnp.zeros((LANE, LANE), F32)
        wy_ref[0, j, 2 * n:, :] = jnp.concatenate(
            [jnp.concatenate([dt_blk[s - sp] if sp <= s else zero for s in range(cs)], axis=1)
             for sp in range(cs)], axis=0).astype(BF16)
        pr, pi = pw[cs]
        for k in range(S5_SCAN_ROWS):
            ap_re_ref[0, k:k + 1, ls] = pr
            ap_im_ref[0, k:k + 1, ls] = pi
            pr, pi = pr * pr - pi * pi, 2.0 * pr * pi


def _s5_prep(lam_re, lam_im, ldt_rep, bt_re, bt_im, ct_re, ct_im, cs):
    vec = pl.BlockSpec((1, 1, S5_LANES), lambda l: (l, 0, 0))
    mat = pl.BlockSpec((1, S5_GROUP, S5_LANES), lambda l: (l, 0, 0))
    k = cs * LANE
    wz_dims = (S5_TILES, k, 2 * TILE_STATES)
    wy_dims = (S5_TILES, 2 * TILE_STATES + k, k)
    ap_spec = pl.BlockSpec((1, S5_SCAN_ROWS, S5_LANES), lambda l: (l, 0, 0))
    ap_shape = jax.ShapeDtypeStruct((DEPTH, S5_SCAN_ROWS, S5_LANES), F32)
    return pl.pallas_call(
        functools.partial(_s5_prep_kernel, cs=cs),
        grid=(DEPTH,),
        in_specs=[vec, vec, vec, mat, mat, mat, mat],
        out_specs=[pl.BlockSpec((1,) + wz_dims, lambda l: (l, 0, 0, 0)),
                   pl.BlockSpec((1,) + wy_dims, lambda l: (l, 0, 0, 0)), ap_spec, ap_spec],
        out_shape=[jax.ShapeDtypeStruct((DEPTH,) + wz_dims, BF16),
                   jax.ShapeDtypeStruct((DEPTH,) + wy_dims, BF16), ap_shape, ap_shape],
        compiler_params=_cparams(("arbitrary",)),
        name="s5_prep",
    )(lam_re, lam_im, ldt_rep, bt_re, bt_im, ct_re, ct_im)


def _gelu(x):
    return 0.5 * x * (1.0 + lax.erf(x * (1.0 / math.sqrt(2.0))))


def _s5_kernel(*refs, cs, rows, chain):
    if chain:
        (u_ref, g_ref, wz_ref, wy_ref, ap_re_ref, ap_im_ref,
         dsk_ref, wglu_ref, o_ref, hr_out_ref, hi_out_ref, hc_r_ref, hc_i_ref) = refs

        @pl.when(pl.program_id(1) == 0)
        def _():
            hc_r_ref[...] = jnp.zeros_like(hc_r_ref)
            hc_i_ref[...] = jnp.zeros_like(hc_i_ref)
    else:
        (u_ref, g_ref, h0r_ref, h0i_ref, wz_ref, wy_ref, ap_re_ref, ap_im_ref,
         dsk_ref, wglu_ref, o_ref, hr_out_ref, hi_out_ref) = refs
    n = TILE_STATES
    k = cs * LANE
    def load_rows(ref, j, s):
        return ref[j, pl.ds(s, rows, stride=cs), :]

    def store_rows(ref, j, s, val):
        ref[j, pl.ds(s, rows, stride=cs), :] = val

    us = [[load_rows(u_ref, j, s) for s in range(cs)] for j in range(S5_TILES)]
    x8 = [jnp.concatenate(us[j], axis=1).astype(BF16) for j in range(S5_TILES)]
    z = [jnp.dot(x8[j], wz_ref[0, j], preferred_element_type=F32) for j in range(S5_TILES)]
    rowi = _iota((rows, n), 0)
    row8 = _iota((SUBLANES, n), 0)
    hin = []
    for j in range(S5_TILES):
        ls = slice(j * n, (j + 1) * n)
        zr, zi = z[j][:, :n], z[j][:, n:]
        ar, ai = ap_re_ref[0, 0:1, ls], ap_im_ref[0, 0:1, ls]
        if chain:
            cr, ci = hc_r_ref[:, ls], hc_i_ref[:, ls]
            hr = jnp.concatenate([zr[:SUBLANES] + jnp.where(row8 == 0, ar * cr - ai * ci, 0.0),
                                  zr[SUBLANES:]], axis=0)
            hi = jnp.concatenate([zi[:SUBLANES] + jnp.where(row8 == 0, ar * ci + ai * cr, 0.0),
                                  zi[SUBLANES:]], axis=0)
            d, lvl = 1, 0
            while d < rows:
                pr, pi = ap_re_ref[0, lvl:lvl + 1, ls], ap_im_ref[0, lvl:lvl + 1, ls]
                if d < SUBLANES:
                    sr = jnp.where(rowi >= d, pltpu.roll(hr, d, 0), 0.0)
                    si = jnp.where(rowi >= d, pltpu.roll(hi, d, 0), 0.0)
                    hr, hi = hr + pr * sr - pi * si, hi + pr * si + pi * sr
                else:
                    sr, si = hr[:rows - d], hi[:rows - d]
                    hr, hi = (jnp.concatenate([hr[:d], hr[d:] + pr * sr - pi * si], axis=0),
                              jnp.concatenate([hi[:d], hi[d:] + pr * si + pi * sr], axis=0))
                d *= 2
                lvl += 1
            hin_r = jnp.where(rowi >= 1, pltpu.roll(hr, 1, 0), cr)
            hin_i = jnp.where(rowi >= 1, pltpu.roll(hi, 1, 0), ci)
            hc_r_ref[:, ls] = hr[rows - 1:rows]
            hc_i_ref[:, ls] = hi[rows - 1:rows]
            hr_out_ref[0, :, ls] = hr[rows - 1:rows]
            hi_out_ref[0, :, ls] = hi[rows - 1:rows]
        else:
            hin_r, hin_i = h0r_ref[:, ls], h0i_ref[:, ls]
            hr_out_ref[:, ls] = ar * hin_r - ai * hin_i + zr
            hi_out_ref[:, ls] = ar * hin_i + ai * hin_r + zi
        hin.append((hin_r.astype(BF16), hin_i.astype(BF16)))
    y8 = []
    for j in range(S5_TILES):
        lhs = jnp.concatenate([hin[j][0], hin[j][1], x8[j]], axis=1)
        cols = []
        for b in range(k // 256):
            kk = 2 * n + (b + 1) * 256
            cols.append(jnp.dot(lhs[:, :kk], wy_ref[0, j, 0:kk, b * 256:(b + 1) * 256],
                                preferred_element_type=F32))
        y8.append(jnp.concatenate(cols, axis=1))
    dsk = dsk_ref[0]
    for s in range(cs):
        y = jnp.concatenate([y8[j][:, s * LANE:(s + 1) * LANE] for j in range(S5_TILES)], axis=1)
        u_s = jnp.concatenate([us[j][s] for j in range(S5_TILES)], axis=1)
        y = _gelu(y + dsk * u_s)
        y = y * _sigmoid(jnp.dot(y.astype(BF16), wglu_ref[0], preferred_element_type=F32))
        gate = jnp.concatenate([load_rows(g_ref, j, s) for j in range(S5_TILES)], axis=1)
        out = y * _silu(gate)
        for j in range(S5_TILES):
            store_rows(o_ref, j, s, out[:, j * LANE:(j + 1) * LANE])


def _s5_mix(u, g, h0, ops, dsk3, wglu_b, l, cs, nb):
    seq = u.shape[1] // nb
    chain = h0 is None
    ntok = min(seq, 1024) if chain else seq
    tiles = seq // ntok
    rows = ntok // cs
    wz, wy, ap_re, ap_im = ops
    tok = pl.BlockSpec((S5_TILES, ntok, LANE), lambda b, t: (0, b * tiles + t, 0))
    ap_spec = pl.BlockSpec((1, S5_SCAN_ROWS, S5_LANES), lambda b, t: (l, 0, 0))
    in_specs = [tok, tok]
    args = [u, g]
    scratch = []
    if not chain:
        hs = pl.BlockSpec((rows, S5_LANES), lambda b, t: (0, 0))
        in_specs += [hs, hs]
        args += [h0[0], h0[1]]
        st_spec = hs
        st_shape = jax.ShapeDtypeStruct((rows, S5_LANES), F32)
    else:
        st_spec = pl.BlockSpec((1, 1, S5_LANES), lambda b, t: (b, 0, 0))
        st_shape = jax.ShapeDtypeStruct((nb, 1, S5_LANES), F32)
        scratch = [pltpu.VMEM((1, S5_LANES), F32), pltpu.VMEM((1, S5_LANES), F32)]
    in_specs += [
        pl.BlockSpec((1,) + wz.shape[1:], lambda b, t: (l, 0, 0, 0)),
        pl.BlockSpec((1,) + wy.shape[1:], lambda b, t: (l, 0, 0, 0)),
        ap_spec, ap_spec,
        pl.BlockSpec((1, 1, S5_WIDTH), lambda b, t: (l, 0, 0)),
        pl.BlockSpec((1, S5_WIDTH, S5_WIDTH), lambda b, t: (l, 0, 0)),
    ]
    args += [wz, wy, ap_re, ap_im, dsk3, wglu_b]
    o, hr, hi = pl.pallas_call(
        functools.partial(_s5_kernel, cs=cs, rows=rows, chain=chain),
        grid=(nb, tiles),
        in_specs=in_specs,
        out_specs=[tok, st_spec, st_spec],
        out_shape=[jax.ShapeDtypeStruct(u.shape, F32), st_shape, st_shape],
        scratch_shapes=scratch,
        compiler_params=_cparams(("arbitrary", "arbitrary")),
        name="s5_mix",
    )(*args)
    return o, hr, hi


def _softmax(s):
    e = jnp.exp(s - jnp.max(s, axis=-1, keepdims=True))
    return (e / jnp.sum(e, axis=-1, keepdims=True)).astype(BF16)


def _attend(q, mk_ref, mv_ref):
    heads = [slice(h * X_HEAD_DIM, (h + 1) * X_HEAD_DIM) for h in range(X_HEADS)]
    s = [_mm_nt(q[:, hs], mk_ref[0, 0, :, hs]) for hs in heads]
    p = [_softmax(sh) for sh in s]
    return jnp.concatenate([_mm(ph, mv_ref[0, 0, :, hs]) for ph, hs in zip(p, heads)], axis=-1)


def _attend_tiled(q, mk_ref, mv_ref, nseg, row_seg):
    def sub(ref, i, h, dt):
        return ref[0, i, pl.ds(dt * X_HEADS + h, MEM_LEN, stride=_MEM_SUB), :]

    def pick(vals):
        out = vals[0]
        for i in range(1, nseg):
            out = jnp.where(row_seg == i, vals[i], out)
        return out

    qb = q.astype(BF16)
    qp = lambda h, dt: qb[:, h * X_HEAD_DIM + dt * LANE:h * X_HEAD_DIM + (dt + 1) * LANE]
    s = [[sum(_mm_nt(qp(h, dt), sub(mk_ref, i, h, dt)) for dt in range(_DT)) for i in range(nseg)]
         for h in range(X_HEADS)]
    p = [_softmax(pick(sh)) for sh in s]
    outs = [pick([_mm(p[h], sub(mv_ref, i, h, dt)) for i in range(nseg)])
            for h in range(X_HEADS) for dt in range(_DT)]
    return jnp.concatenate(outs, axis=-1)


def _post_kernel(x_ref, orw_ref, os5_ref, wout_ref, nx_ref, wq_ref, wo_ref, mk_ref, mv_ref, nf_ref,
                 o_ref, *, nseg, seg_rows, final):
    x = x_ref[...]
    os5 = jnp.concatenate([os5_ref[j] for j in range(S5_TILES)], axis=1)
    x1 = (x + jnp.dot(orw_ref[...].astype(BF16), wout_ref[0, 0:RWKV_WIDTH, :], preferred_element_type=F32)
          + jnp.dot(os5.astype(BF16), wout_ref[0, RWKV_WIDTH:, :], preferred_element_type=F32))
    xc = _rms(x1, nx_ref[0]).astype(BF16)
    q = jnp.dot(xc, wq_ref[0], preferred_element_type=F32) * (1.0 / math.sqrt(X_HEAD_DIM))
    if nseg == 1:
        att = _attend(q, mk_ref, mv_ref)
    else:
        row_seg = _iota((q.shape[0], 1), 0) >> int(math.log2(seg_rows))
        att = _attend_tiled(q, mk_ref, mv_ref, nseg, row_seg)
    x2 = x1 + jnp.dot(att.astype(BF16), wo_ref[0], preferred_element_type=F32)
    if final:
        x2 = _rms(x2, nf_ref[...])
    o_ref[...] = x2


def _post(x2d, orw, os5, wout_b, nx3, wq_b, wo_b, mk, mv, nf2, l, nb, nseg, seg_rows, tq, final):
    tiles = x2d.shape[0] // nb // tq
    tok = lambda w: pl.BlockSpec((tq, w), lambda b, t: (b * tiles + t, 0))
    tok_s5 = pl.BlockSpec((S5_TILES, tq, LANE), lambda b, t: (0, b * tiles + t, 0))
    if nseg == 1:
        mem = pl.BlockSpec((1, 1, MEM_LEN, D_MODEL), lambda b, t: (l, b, 0, 0))
    else:
        mem = pl.BlockSpec((1, nseg, MEM_LEN * _MEM_SUB, LANE), lambda b, t: (l, t, 0, 0))
    wsp = pl.BlockSpec((1, D_MODEL, D_MODEL), lambda b, t: (l, 0, 0))
    return pl.pallas_call(
        functools.partial(_post_kernel, nseg=nseg, seg_rows=seg_rows, final=final),
        grid=(nb, tiles),
        in_specs=[tok(D_MODEL), tok(RWKV_WIDTH), tok_s5, wsp,
                  pl.BlockSpec((1, 1, D_MODEL), lambda b, t: (l, 0, 0)), wsp, wsp, mem, mem,
                  pl.BlockSpec((1, D_MODEL), lambda b, t: (0, 0))],
        out_specs=tok(D_MODEL),
        out_shape=jax.ShapeDtypeStruct(x2d.shape, F32),
        compiler_params=_cparams(("arbitrary", "arbitrary")),
        name="post",
    )(x2d, orw, os5, wout_b, nx3, wq_b, wo_b, mk, mv, nf2)


def _trunk(x, states, mem_k, mem_v, W):
    batch, seq, _ = x.shape
    prompt = states is None
    x2d = x.reshape(batch * seq, D_MODEL)
    new_shift, new_wkv, new_re, new_im = [], [], [], []
    vfirst = None
    wkv_all = None if prompt else states[1]
    for l in range(DEPTH):
        rw_args = (W['mu'], W['vecs'], W['wla'], W['v1'], W['v2'], l)
        if prompt:
            orw, vfirst, us5, gs5, sh, wkv = _rwkv_prompt(x2d, W['norm_mix'], W['w_in'], vfirst, *rw_args,
                                                          batch, seq)
            os5, hr, hi = _s5_mix(us5, gs5, None, W['s5_prompt'], W['d_skip'], W['w_glu'], l, 8, batch)
            hr = hr.reshape(batch, S5_GROUPS, S5_STATE)
            hi = hi.reshape(batch, S5_GROUPS, S5_STATE)
            x2d = _post(x2d, orw, os5, W['w_out'], W['norm_x'], W['wq'], W['wo'], mem_k, mem_v,
                        W['norm_f'], l, batch, 1, seq, 512, l == DEPTH - 1)
        else:
            shift0, _, re0, im0 = states
            psh, grw, us5, gs5 = _inproj(x2d, W['norm_mix'], W['w_in'], l)
            prev_rows = jnp.repeat(shift0[l], seq, axis=0)
            orw, vfirst, sh, wkv_all = _rwkv_sample(psh, grw, vfirst, prev_rows, wkv_all, *rw_args, seq)
            h0 = (re0[l].reshape(batch, S5_LANES), im0[l].reshape(batch, S5_LANES))
            os5, hr, hi = _s5_mix(us5, gs5, h0, W['s5_sample'], W['d_skip'], W['w_glu'], l, seq, 1)
            hr = hr.reshape(batch, S5_GROUPS, S5_STATE)
            hi = hi.reshape(batch, S5_GROUPS, S5_STATE)
            nseg = 32 // seq
            x2d = _post(x2d, orw, os5, W['w_out'], W['norm_x'], W['wq'], W['wo'], mem_k, mem_v,
                        W['norm_f'], l, 1, nseg, seq, nseg * seq, l == DEPTH - 1)
        new_shift.append(sh)
        if prompt:
            new_wkv.append(wkv)
        new_re.append(hr)
        new_im.append(hi)
    y = x2d.reshape(batch, seq, D_MODEL)
    wkv_out = jnp.stack(new_wkv) if prompt else wkv_all
    return y, jnp.stack(new_shift), wkv_out, jnp.stack(new_re), jnp.stack(new_im)


def kernel(x_prompt, x_sample, state_shift, state_wkv, state_s5_re, state_s5_im, cache_mem_k, cache_mem_v, mem_prompt, norm_mix, w_in, mu_shift, w0, w2, a0, a2, v0, v1, v2, k_k, k_a, r_k, gn_w, gn_b, lam_re, lam_im, log_dt, b_re, b_im, c_re, c_im, d_skip, w_glu, w_out, norm_x, norm_mem, wq, wk, wv, wo, norm_f):
    L = DEPTH
    bp, mp = mem_prompt.shape[0], mem_prompt.shape[1]
    zpad = jnp.zeros((1, RWKV_WIDTH), F32)
    vecs = jnp.stack([w0, a0, k_k, k_a, gn_w, gn_b, r_k.reshape(L, RWKV_WIDTH),
                      jnp.concatenate([zpad, v0], axis=0)], axis=1)
    zl = jnp.zeros((L, LORA_W, RWKV_WIDTH), F32)
    wla = jnp.concatenate([jnp.concatenate([w2, zl], axis=2),
                           jnp.concatenate([zl, a2], axis=2)], axis=1).astype(BF16)
    tr = lambda t: jnp.transpose(t, (0, 3, 1, 2)).reshape(L, S5_GROUP, S5_LANES)
    s5_in = (lam_re.reshape(L, 1, S5_LANES), lam_im.reshape(L, 1, S5_LANES),
             jnp.repeat(log_dt, S5_STATE, axis=1).reshape(L, 1, S5_LANES),
             tr(b_re), tr(b_im),
             jnp.transpose(c_re, (0, 2, 1, 3)).reshape(L, S5_GROUP, S5_LANES),
             jnp.transpose(c_im, (0, 2, 1, 3)).reshape(L, S5_GROUP, S5_LANES))
    W = dict(
        norm_mix=norm_mix.reshape(L, 1, D_MODEL), w_in=w_in.astype(BF16),
        mu=mu_shift.reshape(L, 1, SHIFT_COLS), vecs=vecs, wla=wla,
        v1=v1.astype(BF16), v2=v2.astype(BF16),
        s5_prompt=_s5_prep(*s5_in, 8), s5_sample=_s5_prep(*s5_in, x_sample.shape[1]),
        d_skip=d_skip.reshape(L, 1, S5_WIDTH), w_glu=w_glu.astype(BF16),
        w_out=w_out.astype(BF16), norm_x=norm_x.reshape(L, 1, D_MODEL),
        wq=wq.astype(BF16), wo=wo.astype(BF16), norm_f=norm_f.reshape(1, D_MODEL),
    )
    pk, pv, pk_t, pv_t = _memkv(mem_prompt.reshape(bp * mp, D_MODEL), norm_mem,
                                wk.astype(BF16), wv.astype(BF16))
    pk = pk.reshape(L, bp, mp, D_MODEL)
    pv = pv.reshape(L, bp, mp, D_MODEL)
    y_prompt, p_shift, p_wkv, p_re, p_im = _trunk(x_prompt, None, pk, pv, W)
    tiled = lambda c: jnp.transpose(c.reshape(L, -1, MEM_LEN, X_HEADS, _DT, LANE),
                                    (0, 1, 2, 4, 3, 5)).reshape(L, -1, MEM_LEN * _MEM_SUB, LANE)
    untiled = lambda c: jnp.transpose(c.reshape(L, -1, MEM_LEN, _DT, X_HEADS, LANE),
                                      (0, 1, 2, 4, 3, 5)).reshape(L, -1, MEM_LEN, X_HEADS, X_HEAD_DIM)
    y_sample, s_shift, s_wkv, s_re, s_im = _trunk(
        x_sample, (state_shift, state_wkv, state_s5_re, state_s5_im), tiled(cache_mem_k),
        tiled(cache_mem_v), W)
    return (y_prompt, y_sample, p_shift, p_wkv, p_re, p_im, untiled(pk_t), untiled(pv_t),
            s_shift, s_wkv, s_re, s_im)
```

```python
import functools
import math

import jax
import jax.numpy as jnp
from jax import lax
from jax.experimental import pallas as pl
from jax.experimental.pallas import tpu as pltpu

F32 = jnp.float32
BF16 = jnp.bfloat16

D_MODEL = 1024
DEPTH = 4
RWKV_WIDTH = 512
RWKV_HEAD = 64
RWKV_HEADS = 8
LORA_W = 64
LORA_A = 64
LORA_V = 32
S5_WIDTH = 512
S5_GROUP = 16
S5_GROUPS = 32
S5_STATE = 64
S5_LANES = S5_GROUPS * S5_STATE
MEM_LEN = 256
X_HEADS = 4
X_HEAD_DIM = 256
SHIFT_COLS = 3 * RWKV_WIDTH + LORA_W + LORA_A
IN_COLS = SHIFT_COLS + RWKV_WIDTH + 2 * S5_WIDTH
NORM_EPS = 1e-6
GN_EPS = 64e-5

LANE = 128
SUBLANES = 8
WKV_CHUNK = 64
HALF_HEADS = 4
WKV_UNROLL = 8
S5_TILES = S5_WIDTH // LANE
GROUPS_PER_TILE = LANE // S5_GROUP
TILE_STATES = GROUPS_PER_TILE * S5_STATE
S5_SCAN_ROWS = 8
S5_SEG = 128
S5_SEG_PITCH = S5_SEG + SUBLANES
VMEM_LIMIT = 56 * 1024 * 1024


def _cparams(sem):
    return pltpu.CompilerParams(dimension_semantics=sem, vmem_limit_bytes=VMEM_LIMIT)


def _mm(a, b):
    return jnp.dot(a.astype(BF16), b.astype(BF16), preferred_element_type=F32)


def _mm_nt(a, b):
    return lax.dot_general(a.astype(BF16), b.astype(BF16), (((1,), (1,)), ((), ())),
                           preferred_element_type=F32)


def _mm_tn(a, b):
    return lax.dot_general(a.astype(BF16), b.astype(BF16), (((0,), (0,)), ((), ())),
                           preferred_element_type=F32)


def _mm_f32(a, b):
    return jnp.dot(a, b, preferred_element_type=F32, precision=lax.Precision.HIGHEST)


def _mm_nt_x3(a, b):
    a_hi, b_hi = a.astype(BF16), b.astype(BF16)
    a_lo = (a - a_hi.astype(F32)).astype(BF16)
    b_lo = (b - b_hi.astype(F32)).astype(BF16)
    return _mm_nt(a_hi, b_hi) + _mm_nt(a_hi, b_lo) + _mm_nt(a_lo, b_hi)


def _rms(x, g):
    return x * lax.rsqrt(jnp.mean(x * x, axis=-1, keepdims=True) + NORM_EPS) * g


def _sigmoid(x):
    return 1.0 / (1.0 + jnp.exp(-x))


def _silu(x):
    return x * _sigmoid(x)


def _iota(shape, dim):
    return lax.broadcasted_iota(jnp.int32, shape, dim)


_DT = X_HEAD_DIM // LANE
_MEM_SUB = X_HEADS * _DT


def _memkv_kernel(m_ref, g_ref, wk_ref, wv_ref, k_ref, v_ref, kt_ref, vt_ref):
    mn = _rms(m_ref[...], g_ref[0]).astype(BF16)
    tm = m_ref.shape[0]
    for w_ref, o_ref, t_ref in ((wk_ref, k_ref, kt_ref), (wv_ref, v_ref, vt_ref)):
        kv = jnp.dot(mn, w_ref[0], preferred_element_type=F32)
        o_ref[0] = kv
        for h in range(X_HEADS):
            for dt in range(_DT):
                lo = h * X_HEAD_DIM + dt * LANE
                t_ref[0, pl.ds(dt * X_HEADS + h, tm, stride=_MEM_SUB), :] = kv[:, lo:lo + LANE]


def _memkv(mem2d, norm_mem, wk_b, wv_b):
    rows = mem2d.shape[0]
    tm = 512
    out = jax.ShapeDtypeStruct((DEPTH, rows, D_MODEL), F32)
    out_t = jax.ShapeDtypeStruct((DEPTH, rows * _MEM_SUB, LANE), F32)
    nat = pl.BlockSpec((1, tm, D_MODEL), lambda l, i: (l, i, 0))
    til = pl.BlockSpec((1, tm * _MEM_SUB, LANE), lambda l, i: (l, i, 0))
    return pl.pallas_call(
        _memkv_kernel,
        grid=(DEPTH, rows // tm),
        in_specs=[
            pl.BlockSpec((tm, D_MODEL), lambda l, i: (i, 0)),
            pl.BlockSpec((1, 1, D_MODEL), lambda l, i: (l, 0, 0)),
            pl.BlockSpec((1, D_MODEL, D_MODEL), lambda l, i: (l, 0, 0)),
            pl.BlockSpec((1, D_MODEL, D_MODEL), lambda l, i: (l, 0, 0)),
        ],
        out_specs=[nat, nat, til, til],
        out_shape=[out, out, out_t, out_t],
        compiler_params=_cparams(("arbitrary", "arbitrary")),
        name="memkv",
    )(mem2d, norm_mem.reshape(DEPTH, 1, D_MODEL), wk_b, wv_b)


_IN_SPLITS = (0, SHIFT_COLS, SHIFT_COLS + RWKV_WIDTH, SHIFT_COLS + RWKV_WIDTH + S5_WIDTH, IN_COLS)


def _inproj_kernel(x_ref, g_ref, w_ref, psh_ref, grw_ref, us5_ref, gs5_ref):
    xn = _rms(x_ref[...], g_ref[0]).astype(BF16)
    proj = lambda lo, hi: jnp.dot(xn, w_ref[0, :, lo:hi], preferred_element_type=F32)
    psh_ref[...] = proj(_IN_SPLITS[0], _IN_SPLITS[1])
    grw_ref[...] = proj(_IN_SPLITS[1], _IN_SPLITS[2])
    for o_ref, lo in ((us5_ref, _IN_SPLITS[2]), (gs5_ref, _IN_SPLITS[3])):
        for j in range(S5_TILES):
            o_ref[j] = proj(lo + j * LANE, lo + (j + 1) * LANE)


def _inproj(x2d, norm_mix3, w_in_b, l):
    rows = x2d.shape[0]
    tm = 256
    s5_spec = pl.BlockSpec((S5_TILES, tm, LANE), lambda i: (0, i, 0))
    s5_shape = jax.ShapeDtypeStruct((S5_TILES, rows, LANE), F32)
    return pl.pallas_call(
        _inproj_kernel,
        grid=(rows // tm,),
        in_specs=[
            pl.BlockSpec((tm, D_MODEL), lambda i: (i, 0)),
            pl.BlockSpec((1, 1, D_MODEL), lambda i: (l, 0, 0)),
            pl.BlockSpec((1, D_MODEL, IN_COLS), lambda i: (l, 0, 0)),
        ],
        out_specs=[pl.BlockSpec((tm, SHIFT_COLS), lambda i: (i, 0)),
                   pl.BlockSpec((tm, RWKV_WIDTH), lambda i: (i, 0)), s5_spec, s5_spec],
        out_shape=[jax.ShapeDtypeStruct((rows, SHIFT_COLS), F32),
                   jax.ShapeDtypeStruct((rows, RWKV_WIDTH), F32), s5_shape, s5_shape],
        compiler_params=_cparams(("arbitrary",)),
        name="inproj",
    )(x2d, norm_mix3, w_in_b)


_V_W0, _V_A0, _V_KK, _V_KA, _V_GNW, _V_GNB, _V_RK, _V_V0 = range(8)


def _rwkv_prep(ps, vfirst, vecs, wla, v1, v2):
    r = ps[:, 0:RWKV_WIDTH]
    k = ps[:, RWKV_WIDTH:2 * RWKV_WIDTH]
    v = ps[:, 2 * RWKV_WIDTH:3 * RWKV_WIDTH]
    lora_in = ps[:, 3 * RWKV_WIDTH:SHIFT_COLS]
    lane = _iota(lora_in.shape, 1)
    lora_in = jnp.where(lane < LORA_W, jnp.tanh(lora_in), lora_in)
    lora = _mm(lora_in, wla)
    wpre = vecs[_V_W0:_V_W0 + 1] + lora[:, 0:RWKV_WIDTH]
    z = -wpre
    softplus = jnp.maximum(z, 0.0) + jnp.log(1.0 + jnp.exp(-jnp.abs(z)))
    logdecay = -jnp.exp(-softplus - 0.5)
    a = _sigmoid(vecs[_V_A0:_V_A0 + 1] + lora[:, RWKV_WIDTH:2 * RWKV_WIDTH])
    if vfirst is not None:
        gate = _sigmoid(vecs[_V_V0:_V_V0 + 1] + _mm(_mm(v, v1), v2))
        v = v + (vfirst - v) * gate
    kku = k * vecs[_V_KK:_V_KK + 1]
    kmod = k * (1.0 + (a - 1.0) * vecs[_V_KA:_V_KA + 1])
    return r, kku, kmod, v, a, logdecay


def _wkv_chunk(lw, load_head, states, seg_len, vecs):
    C = WKV_CHUNK
    shift = int(math.log2(seg_len))
    per_grp = SUBLANES // seg_len
    ngrp = C // SUBLANES
    row = _iota((C, C), 0)
    col = _iota((C, C), 1)
    same = (row >> shift) == (col >> shift)
    incl = jnp.logical_and(col <= row, same)
    strict = jnp.logical_and(col < row, same)
    cum = _mm_f32(incl.astype(F32), lw)
    tot = _mm_f32(same.astype(F32), lw)
    g_inc_all = jnp.exp(cum)
    g_exc_all = jnp.exp(cum - lw)
    g_inv_all = jnp.exp(-cum)
    g_end_all = jnp.exp(tot - cum)
    g_tot_all = jnp.exp(tot)
    eye = (row == col).astype(F32)
    nsq = max(shift - 1, 0)
    seg16 = (_iota((2 * SUBLANES, 1), 0) & (SUBLANES - 1)) >> shift

    hd = []
    for h in range(RWKV_HEADS):
        sl = slice(h * RWKV_HEAD, (h + 1) * RWKV_HEAD)
        r, kku, kmod, v, a = load_head(h)
        ss = jnp.sum(kku * kku, axis=-1, keepdims=True)
        kkn = kku * lax.rsqrt(jnp.maximum(ss, 1e-24))
        b = kkn * a
        qk = kkn * g_exc_all[:, sl]
        qr = r * g_inc_all[:, sl]
        hd.append(dict(sl=sl, r=r, kmod=kmod, v=v, qk=qk, qr=qr,
                       q2=jnp.concatenate([qk, qr], axis=0).astype(BF16),
                       kd=kmod * g_inv_all[:, sl], bd=b * g_inv_all[:, sl],
                       kc=kmod * g_end_all[:, sl], bc=b * g_end_all[:, sl]))
    for d in hd:
        d['ak'] = _mm_nt(d['q2'], d['kd'])
        d['ab'] = _mm_nt(d['q2'], d['bd'])
    for d in hd:
        d['akk'] = jnp.where(strict, d['ak'][:C], 0.0)
        d['ark'] = jnp.where(incl, d['ak'][C:], 0.0)
        d['arb'] = jnp.where(incl, d['ab'][C:], 0.0)
        d['n'] = -jnp.where(strict, d['ab'][:C], 0.0)
        d['tinv'] = eye + d['n']
    for _ in range(nsq):
        for d in hd:
            d['n'] = _mm(d['n'], d['n'])
        for d in hd:
            d['tinv'] = d['tinv'] + _mm(d['tinv'], d['n'])
    for d in hd:
        d['av'] = _mm(jnp.concatenate([d['akk'], d['ark']], axis=0), d['v'])
    for d in hd:
        d['u_ind'] = _mm(d['tinv'], d['av'][:C])
        d['tq'] = _mm(d['tinv'], d['qk'])
    for h, d in enumerate(hd):
        us, yq = [], []
        for g in range(ngrp):
            rs = slice(g * SUBLANES, (g + 1) * SUBLANES)
            lhs = jnp.concatenate([d['tq'][rs], d['qr'][rs]], axis=0).astype(BF16)
            qs = None
            for s in range(per_grp):
                o = _mm_nt(lhs, states[h][g * per_grp + s])
                qs = o if qs is None else jnp.where(seg16 == s, o, qs)
            us.append(d['u_ind'][rs] + qs[:SUBLANES])
            yq.append(qs[SUBLANES:])
        d['u'] = jnp.concatenate(us, axis=0)
        d['yq'] = jnp.concatenate(yq, axis=0)
    for d in hd:
        d['y'] = d['yq'] + d['av'][C:] - _mm(d['arb'], d['u'])
    new_states = []
    for h, d in enumerate(hd):
        hs = []
        for g in range(ngrp):
            rs = slice(g * SUBLANES, (g + 1) * SUBLANES)
            vu = jnp.concatenate([d['v'][rs], d['u'][rs]], axis=0)
            kb = jnp.concatenate([d['kc'][rs], -d['bc'][rs]], axis=0).astype(BF16)
            for s in range(per_grp):
                i = g * per_grp + s
                gt = g_tot_all[i * seg_len:i * seg_len + 1, d['sl']]
                hs.append(states[h][i] * gt + _mm_tn(jnp.where(seg16 == s, vu, 0.0), kb))
        new_states.append(hs)
    ys = []
    for d in hd:
        y, sl = d['y'], d['sl']
        mu = jnp.mean(y, axis=-1, keepdims=True)
        yc = y - mu
        var = jnp.mean(yc * yc, axis=-1, keepdims=True)
        yn = yc * lax.rsqrt(var + GN_EPS) * vecs[_V_GNW:_V_GNW + 1, sl] + vecs[_V_GNB:_V_GNB + 1, sl]
        bonus = jnp.sum(d['r'] * d['kmod'] * vecs[_V_RK:_V_RK + 1, sl], axis=-1, keepdims=True)
        ys.append(yn + bonus * d['v'])
    return jnp.concatenate(ys, axis=-1), new_states


def _rwkv_prompt_kernel(*refs, first_layer, tt):
    if first_layer:
        (x_ref, nm_ref, win_ref, mu_ref, vecs_ref, wla_ref, v1_ref, v2_ref,
         o_ref, vf_out_ref, us5_ref, gs5_ref, shift_out_ref, wkv_out_ref,
         carry_ref, s_ref, r_s, kkn_s, kmod_s, v_s, b_s, lw_s, y_s) = refs
        vf_ref = None
    else:
        (x_ref, nm_ref, win_ref, vf_ref, mu_ref, vecs_ref, wla_ref, v1_ref, v2_ref,
         o_ref, us5_ref, gs5_ref, shift_out_ref, wkv_out_ref,
         carry_ref, s_ref, r_s, kkn_s, kmod_s, v_s, b_s, lw_s, y_s) = refs
        vf_out_ref = None
    t = pl.program_id(1)

    @pl.when(t == 0)
    def _():
        carry_ref[...] = jnp.zeros_like(carry_ref)
        s_ref[...] = jnp.zeros_like(s_ref)

    xn = _rms(x_ref[0], nm_ref[0]).astype(BF16)
    proj = lambda lo, hi: jnp.dot(xn, win_ref[0, :, lo:hi], preferred_element_type=F32)
    p = proj(_IN_SPLITS[0], _IN_SPLITS[1])
    o_ref[0] = proj(_IN_SPLITS[1], _IN_SPLITS[2])
    pad = jnp.zeros((S5_SEG_PITCH - S5_SEG, LANE), F32)
    for s5_ref, lo in ((us5_ref, _IN_SPLITS[2]), (gs5_ref, _IN_SPLITS[3])):
        for j in range(S5_TILES):
            col = proj(lo + j * LANE, lo + (j + 1) * LANE)
            for q in range(tt // S5_SEG):
                s5_ref[j, q * S5_SEG_PITCH:q * S5_SEG_PITCH + S5_SEG, :] = col[q * S5_SEG:(q + 1) * S5_SEG]
                s5_ref[j, q * S5_SEG_PITCH + S5_SEG:(q + 1) * S5_SEG_PITCH, :] = pad
    rowi = _iota(p.shape, 0)
    prev = jnp.where(rowi == 0, carry_ref[...], pltpu.roll(p, 1, 0))
    ps = p + (prev - p) * mu_ref[0]
    last = p[tt - 1:tt, :]
    carry_ref[...] = last
    shift_out_ref[0] = last

    vecs = vecs_ref[0]
    vfirst = None if first_layer else vf_ref[0]
    r, kku, kmod, v, a, lw = _rwkv_prep(ps, vfirst, vecs, wla_ref[0], v1_ref[0], v2_ref[0])
    if first_layer:
        vf_out_ref[0] = v

    hw = HALF_HEADS * RWKV_HEAD
    bd_bool = (_iota((hw, hw), 0) >> 6) == (_iota((hw, hw), 1) >> 6)
    bd_ones = bd_bool.astype(F32).astype(BF16)

    def head_sums(x):
        n = x.shape[0]
        hi = x.astype(BF16)
        lo = (x - hi.astype(F32)).astype(BF16)
        st = jnp.concatenate([hi, lo], axis=0)
        outs = []
        for hf in range(2):
            s2 = jnp.dot(st[:, hf * hw:(hf + 1) * hw], bd_ones, preferred_element_type=F32)
            outs.append(s2[:n] + s2[n:])
        return jnp.concatenate(outs, axis=1)

    kkn = kku * lax.rsqrt(jnp.maximum(head_sums(kku * kku), 1e-24))
    r_s[...] = r
    kkn_s[...] = kkn
    kmod_s[...] = kmod
    v_s[...] = v
    b_s[...] = kkn * a
    lw_s[...] = lw

    C = WKV_CHUNK
    trow = _iota((C, hw), 0)
    tcol = _iota((C, hw), 1) & (RWKV_HEAD - 1)
    strict = tcol < trow
    incl = tcol <= trow
    eye = (tcol == trow).astype(F32)
    crow = _iota((C, RWKV_WIDTH), 0)

    def bd(x):
        return jnp.concatenate([x.astype(BF16)] * HALF_HEADS, axis=0) * bd_ones

    def group(g, carry):
        chains = []
        for k in range(WKV_UNROLL):
            rows = pl.ds(pl.multiple_of((g * WKV_UNROLL + k) * C, C), C)
            lwc = lw_s[rows, :]
            cum = lwc
            for d in (1, 2, 4):
                cum = cum + jnp.where(crow >= d, pltpu.roll(cum, d, 0), 0.0)
            for d in (8, 16, 32):
                cum = jnp.concatenate([cum[:d], cum[d:] + cum[:C - d]], axis=0)
            tot = cum[C - 1:C, :]
            g_inc = jnp.exp(cum)
            g_exc = jnp.exp(cum - lwc)
            g_inv = jnp.exp(-cum)
            g_end = jnp.exp(tot - cum)
            g_tot = jnp.exp(tot)
            rr, kk, km, bb, vv = (r_s[rows, :], kkn_s[rows, :], kmod_s[rows, :], b_s[rows, :],
                                  v_s[rows, :])
            qk_a, qr_a = kk * g_exc, rr * g_inc
            kd_a, bd_a = km * g_inv, bb * g_inv
            kc_a, bc_a = km * g_end, bb * g_end
            for hf in range(2):
                ls = slice(hf * hw, (hf + 1) * hw)
                chains.append(dict(k=k, hf=hf, rows=rows, qk=qk_a[:, ls], qr=qr_a[:, ls], kd=kd_a[:, ls],
                                   bd=bd_a[:, ls], kc=kc_a[:, ls], bc=bc_a[:, ls], v=vv[:, ls],
                                   g_tot=g_tot[:, ls]))
        for ch in chains:
            ch['q2'] = jnp.concatenate([ch['qk'], ch['qr']], axis=0).astype(BF16)
            ch['ak'] = _mm_nt(ch['q2'], bd(ch['kd']))
        for ch in chains:
            ab = _mm_nt(ch['q2'], bd(ch['bd']))
            ch['akk'] = jnp.where(strict, ch['ak'][:C], 0.0)
            ch['ark'] = jnp.where(incl, ch['ak'][C:], 0.0)
            ch['arb'] = jnp.where(incl, ab[C:], 0.0)
            ch['n'] = -jnp.where(strict, ab[:C], 0.0)
            ch['tinv'] = eye + ch['n']
        for ch in chains:
            ch['npow'] = _mm(ch['n'], bd(ch['n']))
        for _ in range(4):
            for ch in chains:
                st = _mm(jnp.concatenate([ch['tinv'], ch['npow']], axis=0), bd(ch['npow']))
                ch['tinv'] = ch['tinv'] + st[:C]
                ch['npow'] = st[C:]
        for ch in chains:
            ch['av'] = _mm(jnp.concatenate([ch['akk'], ch['ark']], axis=0), bd(ch['v']))
        for ch in chains:
            ch['tinv'] = ch['tinv'] + _mm(ch['tinv'], bd(ch['npow']))
        for ch in chains:
            ch['u_ind'] = _mm(ch['tinv'], bd(ch['av'][:C]))
            ch['tq'] = _mm(ch['tinv'], bd(ch['qk']))
        for ch in chains:
            ch['gc'] = jnp.where(bd_bool, _mm_tn(ch['tq'], ch['bc']), 0.0).astype(BF16)
            ch['nc'] = jnp.where(bd_bool, _mm_tn(jnp.concatenate([ch['v'], ch['u_ind']], axis=0),
                                                 jnp.concatenate([ch['kc'], -ch['bc']], axis=0)), 0.0)
        state = [s_ref[0], s_ref[1]]
        for ch in chains:
            s_old = state[ch['hf']]
            ch['s_b'] = s_old.astype(BF16)
            state[ch['hf']] = s_old * ch['g_tot'] - _mm(ch['s_b'], ch['gc']) + ch['nc']
        s_ref[0] = state[0]
        s_ref[1] = state[1]
        for ch in chains:
            ch['qs'] = _mm_nt(jnp.concatenate([ch['tq'], ch['qr']], axis=0), ch['s_b'])
        for ch in chains:
            u = ch['u_ind'] + ch['qs'][:C]
            ch['y'] = ch['qs'][C:] + ch['av'][C:] - _mm(ch['arb'], bd(u))
        for k in range(WKV_UNROLL):
            y_s[chains[2 * k]['rows'], :] = jnp.concatenate([chains[2 * k]['y'], chains[2 * k + 1]['y']], axis=1)
        return carry

    lax.fori_loop(0, tt // (C * WKV_UNROLL), group, 0)

    y = y_s[...]
    inv_n = 1.0 / RWKV_HEAD
    mu = head_sums(y) * inv_n
    yc = y - mu
    var = head_sums(yc * yc) * inv_n
    yn = yc * lax.rsqrt(var + GN_EPS) * vecs[_V_GNW:_V_GNW + 1] + vecs[_V_GNB:_V_GNB + 1]
    rr, km, vv = r_s[...], kmod_s[...], v_s[...]
    bonus = head_sums(rr * km * vecs[_V_RK:_V_RK + 1])
    o_ref[0] = (yn + bonus * vv) * _silu(o_ref[0])

    @pl.when(t == pl.num_programs(1) - 1)
    def _():
        for h in range(RWKV_HEADS):
            o = (h % HALF_HEADS) * RWKV_HEAD
            wkv_out_ref[0, h] = s_ref[h // HALF_HEADS, o:o + RWKV_HEAD, o:o + RWKV_HEAD]


def _rwkv_prompt(x2d, norm_mix3, w_in_b, vfirst, mu3, vecs, wla, v1, v2, l, batch, seq):
    tt = 512
    tiles = seq // tt
    first = vfirst is None
    lv = max(l - 1, 0)
    tok = lambda w: pl.BlockSpec((1, tt, w), lambda b, t: (b, t, 0))
    s5_spec = pl.BlockSpec((S5_TILES, tt // S5_SEG * S5_SEG_PITCH, LANE), lambda b, t: (0, b * tiles + t, 0))
    s5_shape = jax.ShapeDtypeStruct((S5_TILES, batch * seq // S5_SEG * S5_SEG_PITCH, LANE), F32)
    in_specs = [tok(D_MODEL),
                pl.BlockSpec((1, 1, D_MODEL), lambda b, t: (l, 0, 0)),
                pl.BlockSpec((1, D_MODEL, IN_COLS), lambda b, t: (l, 0, 0))]
    args = [x2d.reshape(batch, seq, D_MODEL), norm_mix3, w_in_b]
    if not first:
        in_specs.append(tok(RWKV_WIDTH))
        args.append(vfirst.reshape(batch, seq, RWKV_WIDTH))
    in_specs += [
        pl.BlockSpec((1, 1, SHIFT_COLS), lambda b, t: (l, 0, 0)),
        pl.BlockSpec((1, 8, RWKV_WIDTH), lambda b, t: (l, 0, 0)),
        pl.BlockSpec((1, LORA_W + LORA_A, 2 * RWKV_WIDTH), lambda b, t: (l, 0, 0)),
        pl.BlockSpec((1, RWKV_WIDTH, LORA_V), lambda b, t: (lv, 0, 0)),
        pl.BlockSpec((1, LORA_V, RWKV_WIDTH), lambda b, t: (lv, 0, 0)),
    ]
    args += [mu3, vecs, wla, v1, v2]
    out_specs = [tok(RWKV_WIDTH)]
    out_shape = [jax.ShapeDtypeStruct((batch, seq, RWKV_WIDTH), F32)]
    if first:
        out_specs.append(tok(RWKV_WIDTH))
        out_shape.append(jax.ShapeDtypeStruct((batch, seq, RWKV_WIDTH), F32))
    out_specs += [
        s5_spec, s5_spec,
        pl.BlockSpec((1, 1, SHIFT_COLS), lambda b, t: (b, 0, 0)),
        pl.BlockSpec((1, RWKV_HEADS, RWKV_HEAD, RWKV_HEAD), lambda b, t: (b, 0, 0, 0)),
    ]
    out_shape += [
        s5_shape, s5_shape,
        jax.ShapeDtypeStruct((batch, 1, SHIFT_COLS), F32),
        jax.ShapeDtypeStruct((batch, RWKV_HEADS, RWKV_HEAD, RWKV_HEAD), F32),
    ]
    scratch = [
        pltpu.VMEM((1, SHIFT_COLS), F32),
        pltpu.VMEM((2, HALF_HEADS * RWKV_HEAD, HALF_HEADS * RWKV_HEAD), F32),
    ] + [pltpu.VMEM((tt, RWKV_WIDTH), F32) for _ in range(7)]
    outs = pl.pallas_call(
        functools.partial(_rwkv_prompt_kernel, first_layer=first, tt=tt),
        grid=(batch, seq // tt),
        in_specs=in_specs,
        out_specs=out_specs,
        out_shape=out_shape,
        scratch_shapes=scratch,
        compiler_params=_cparams(("arbitrary", "arbitrary")),
        name="rwkv_prompt",
    )(*args)
    if first:
        o, vf, us5, gs5, sh, wkv = outs
    else:
        (o, us5, gs5, sh, wkv), vf = outs, vfirst
    return (o.reshape(batch * seq, RWKV_WIDTH), vf.reshape(batch * seq, RWKV_WIDTH), us5, gs5,
            sh[:, 0], wkv)


def _rwkv_sample_kernel(*refs, first_layer, seq):
    if first_layer:
        (psh_ref, grw_ref, prev_ref, wkv_ref, mu_ref, vecs_ref, wla_ref, v1_ref, v2_ref,
         o_ref, vf_out_ref, wkv_out_ref) = refs
        vf_ref = None
    else:
        (psh_ref, grw_ref, vf_ref, prev_ref, wkv_ref, mu_ref, vecs_ref, wla_ref, v1_ref, v2_ref,
         o_ref, wkv_out_ref) = refs
        vf_out_ref = None
    nseq = WKV_CHUNK // seq
    p = psh_ref[...]
    rowi = _iota(p.shape, 0)
    prev = jnp.where((rowi & (seq - 1)) == 0, prev_ref[...], pltpu.roll(p, 1, 0))
    ps = p + (prev - p) * mu_ref[0]

    vecs = vecs_ref[0]
    vfirst = None if first_layer else vf_ref[...]
    r, kku, kmod, v, a, lw = _rwkv_prep(ps, vfirst, vecs, wla_ref[0], v1_ref[0], v2_ref[0])
    if first_layer:
        vf_out_ref[...] = v

    def load_head(h):
        sl = slice(h * RWKV_HEAD, (h + 1) * RWKV_HEAD)
        return (r[:, sl], kku[:, sl], kmod[:, sl], v[:, sl], a[:, sl])

    states = [[wkv_ref[0, s, h] for s in range(nseq)] for h in range(RWKV_HEADS)]
    y, new_states = _wkv_chunk(lw, load_head, states, seq, vecs)
    for h in range(RWKV_HEADS):
        for s in range(nseq):
            wkv_out_ref[0, s, h] = new_states[h][s]
    o_ref[...] = y * _silu(grw_ref[...])


def _rwkv_sample(psh, grw, vfirst, prev_rows, wkv_all, mu3, vecs, wla, v1, v2, l, seq):
    rows = psh.shape[0]
    nseq = WKV_CHUNK // seq
    first = vfirst is None
    lv = max(l - 1, 0)
    tok = lambda w: pl.BlockSpec((WKV_CHUNK, w), lambda i: (i, 0))
    st = pl.BlockSpec((1, nseq, RWKV_HEADS, RWKV_HEAD, RWKV_HEAD), lambda i: (l, i, 0, 0, 0))
    in_specs = [tok(SHIFT_COLS), tok(RWKV_WIDTH)]
    args = [psh, grw]
    if not first:
        in_specs.append(tok(RWKV_WIDTH))
        args.append(vfirst)
    in_specs += [
        tok(SHIFT_COLS), st,
        pl.BlockSpec((1, 1, SHIFT_COLS), lambda i: (l, 0, 0)),
        pl.BlockSpec((1, 8, RWKV_WIDTH), lambda i: (l, 0, 0)),
        pl.BlockSpec((1, LORA_W + LORA_A, 2 * RWKV_WIDTH), lambda i: (l, 0, 0)),
        pl.BlockSpec((1, RWKV_WIDTH, LORA_V), lambda i: (lv, 0, 0)),
        pl.BlockSpec((1, LORA_V, RWKV_WIDTH), lambda i: (lv, 0, 0)),
    ]
    wkv_arg = len(args) + 1
    args += [prev_rows, wkv_all, mu3, vecs, wla, v1, v2]
    out_specs = [tok(RWKV_WIDTH)]
    out_shape = [jax.ShapeDtypeStruct((rows, RWKV_WIDTH), F32)]
    if first:
        out_specs.append(tok(RWKV_WIDTH))
        out_shape.append(jax.ShapeDtypeStruct((rows, RWKV_WIDTH), F32))
    out_specs.append(st)
    out_shape.append(jax.ShapeDtypeStruct(wkv_all.shape, F32))
    outs = pl.pallas_call(
        functools.partial(_rwkv_sample_kernel, first_layer=first, seq=seq),
        grid=(rows // WKV_CHUNK,),
        in_specs=in_specs,
        out_specs=out_specs,
        out_shape=out_shape,
        input_output_aliases={wkv_arg: len(out_shape) - 1},
        compiler_params=_cparams(("arbitrary",)),
        name="rwkv_sample",
    )(*args)
    if first:
        o, vf, wkv = outs
    else:
        (o, wkv), vf = outs, vfirst
    sh = psh.reshape(rows // seq, seq, SHIFT_COLS)[:, seq - 1]
    return o, vf, sh, wkv


def _s5_prep_kernel(lam_re_ref, lam_im_ref, ldt_ref, bt_re_ref, bt_im_ref, ct_re_ref, ct_im_ref,
                    wz_ref, wy_ref, ap_re_ref, ap_im_ref, *, cs):
    n = TILE_STATES
    row_g = _iota((LANE, n), 0) >> 4
    lane_g = _iota((LANE, n), 1) >> 6
    emb_mask = row_g == lane_g
    bd_mask = (_iota((LANE, LANE), 0) >> 4) == (_iota((LANE, LANE), 1) >> 4)

    def rep(x):
        return jnp.concatenate([x] * GROUPS_PER_TILE, axis=0)

    def embed(x):
        return jnp.where(emb_mask, rep(x), 0.0)

    for j in range(S5_TILES):
        ls = slice(j * n, (j + 1) * n)
        lr = lam_re_ref[0, :, ls]
        li = lam_im_ref[0, :, ls]
        dt = jnp.exp(ldt_ref[0, :, ls])
        mag = jnp.exp(lr * dt)
        a_re = mag * jnp.cos(li * dt)
        a_im = mag * jnp.sin(li * dt)
        q_re = a_re - 1.0
        q_im = a_im
        den = lr * lr + li * li
        f_re = (q_re * lr + q_im * li) / den
        f_im = (q_im * lr - q_re * li) / den
        b_re = bt_re_ref[0, :, ls]
        b_im = bt_im_ref[0, :, ls]
        bb_re = f_re * b_re - f_im * b_im
        bb_im = f_re * b_im + f_im * b_re
        c_re = ct_re_ref[0, :, ls]
        c_im = ct_im_ref[0, :, ls]
        pw = [(jnp.ones_like(a_re), jnp.zeros_like(a_re))]
        for _ in range(cs):
            pr, pi = pw[-1]
            pw.append((pr * a_re - pi * a_im, pr * a_im + pi * a_re))
        e = [(pr * bb_re - pi * bb_im, pr * bb_im + pi * bb_re) for pr, pi in pw[:cs]]
        wz_ref[0, j, :, 0:n] = jnp.concatenate(
            [embed(e[cs - 1 - s][0]) for s in range(cs)], axis=0).astype(BF16)
        wz_ref[0, j, :, n:2 * n] = jnp.concatenate(
            [embed(e[cs - 1 - s][1]) for s in range(cs)], axis=0).astype(BF16)
        wc_re = jnp.concatenate(
            [embed(c_re * pw[s + 1][0] - c_im * pw[s + 1][1]) for s in range(cs)], axis=0)
        wc_im = jnp.concatenate(
            [embed(-(c_re * pw[s + 1][1] + c_im * pw[s + 1][0])) for s in range(cs)], axis=0)
        wy_ref[0, j, 0:n, :] = wc_re.T.astype(BF16)
        wy_ref[0, j, n:2 * n, :] = wc_im.T.astype(BF16)
        ce_re = embed(c_re)
        ce_im = embed(c_im)
        dt_blk = [jnp.where(bd_mask, _mm_nt_x3(rep(er), ce_re) - _mm_nt_x3(rep(ei), ce_im), 0.0)
                  for er, ei in e]
        zero = jnp.zeros((LANE, LANE), F32)
        wy_ref[0, j, 2 * n:, :] = jnp.concatenate(
            [jnp.concatenate([dt_blk[s - sp] if sp <= s else zero for s in range(cs)], axis=1)
             for sp in range(cs)], axis=0).astype(BF16)
        pr, pi = pw[cs]
        for k in range(S5_SCAN_ROWS):
            ap_re_ref[0, k:k + 1, ls] = pr
            ap_im_ref[0, k:k + 1, ls] = pi
            pr, pi = pr * pr - pi * pi, 2.0 * pr * pi


def _s5_prep(lam_re, lam_im, ldt_rep, bt_re, bt_im, ct_re, ct_im, cs):
    vec = pl.BlockSpec((1, 1, S5_LANES), lambda l: (l, 0, 0))
    mat = pl.BlockSpec((1, S5_GROUP, S5_LANES), lambda l: (l, 0, 0))
    k = cs * LANE
    wz_dims = (S5_TILES, k, 2 * TILE_STATES)
    wy_dims = (S5_TILES, 2 * TILE_STATES + k, k)
    ap_spec = pl.BlockSpec((1, S5_SCAN_ROWS, S5_LANES), lambda l: (l, 0, 0))
    ap_shape = jax.ShapeDtypeStruct((DEPTH, S5_SCAN_ROWS, S5_LANES), F32)
    return pl.pallas_call(
        functools.partial(_s5_prep_kernel, cs=cs),
        grid=(DEPTH,),
        in_specs=[vec, vec, vec, mat, mat, mat, mat],
        out_specs=[pl.BlockSpec((1,) + wz_dims, lambda l: (l, 0, 0, 0)),
                   pl.BlockSpec((1,) + wy_dims, lambda l: (l, 0, 0, 0)), ap_spec, ap_spec],
        out_shape=[jax.ShapeDtypeStruct((DEPTH,) + wz_dims, BF16),
                   jax.ShapeDtypeStruct((DEPTH,) + wy_dims, BF16), ap_shape, ap_shape],
        compiler_params=_cparams(("arbitrary",)),
        name="s5_prep",
    )(lam_re, lam_im, ldt_rep, bt_re, bt_im, ct_re, ct_im)


def _gelu(x):
    return 0.5 * x * (1.0 + lax.erf(x * (1.0 / math.sqrt(2.0))))


def _s5_kernel(*refs, cs, rows, chain):
    if chain:
        (u_ref, g_ref, wz_ref, wy_ref, ap_re_ref, ap_im_ref,
         dsk_ref, wglu_ref, o_ref, hr_out_ref, hi_out_ref, hc_r_ref, hc_i_ref) = refs

        @pl.when(pl.program_id(1) == 0)
        def _():
            hc_r_ref[...] = jnp.zeros_like(hc_r_ref)
            hc_i_ref[...] = jnp.zeros_like(hc_i_ref)
    else:
        (u_ref, g_ref, h0r_ref, h0i_ref, wz_ref, wy_ref, ap_re_ref, ap_im_ref,
         dsk_ref, wglu_ref, o_ref, hr_out_ref, hi_out_ref) = refs
    n = TILE_STATES
    k = cs * LANE
    if chain:
        steps = rows // SUBLANES

        def load_rows(ref, j, s):
            return jnp.concatenate([ref[j, pl.ds(i * cs + s, SUBLANES, stride=S5_SEG_PITCH), :]
                                    for i in range(steps)], axis=0)

        def store_rows(ref, j, s, val):
            for i in range(steps):
                ref[j, pl.ds(i * cs + s, SUBLANES, stride=S5_SEG_PITCH), :] = (
                    val[i * SUBLANES:(i + 1) * SUBLANES])

        pad = jnp.zeros((S5_SEG_PITCH - S5_SEG, LANE), F32)
        for j in range(S5_TILES):
            for g in range(SUBLANES):
                o_ref[j, g * S5_SEG_PITCH + S5_SEG:(g + 1) * S5_SEG_PITCH, :] = pad
    else:
        def load_rows(ref, j, s):
            return ref[j, pl.ds(s, rows, stride=cs), :]

        def store_rows(ref, j, s, val):
            ref[j, pl.ds(s, rows, stride=cs), :] = val

    us = [[load_rows(u_ref, j, s) for s in range(cs)] for j in range(S5_TILES)]
    x8 = [jnp.concatenate(us[j], axis=1).astype(BF16) for j in range(S5_TILES)]
    z = [jnp.dot(x8[j], wz_ref[0, j], preferred_element_type=F32) for j in range(S5_TILES)]
    hin = []
    for j in range(S5_TILES):
        ls = slice(j * n, (j + 1) * n)
        zr, zi = z[j][:, :n], z[j][:, n:]
        ar, ai = ap_re_ref[0, 0:1, ls], ap_im_ref[0, 0:1, ls]
        if chain:
            hr = jnp.zeros((SUBLANES, n), F32)
            hi = jnp.zeros((SUBLANES, n), F32)
            loc = []
            for i in range(steps):
                loc.append((hr, hi))
                rs = slice(i * SUBLANES, (i + 1) * SUBLANES)
                hr, hi = ar * hr - ai * hi + zr[rs], ar * hi + ai * hr + zi[rs]
            lvl = int(math.log2(steps))
            sr, si = ap_re_ref[0, lvl:lvl + 1, ls], ap_im_ref[0, lvl:lvl + 1, ls]
            cr, ci = hc_r_ref[:, ls], hc_i_ref[:, ls]
            seg_r, seg_i = [], []
            for g in range(SUBLANES):
                seg_r.append(cr)
                seg_i.append(ci)
                cr, ci = sr * cr - si * ci + hr[g:g + 1], sr * ci + si * cr + hi[g:g + 1]
            hc_r_ref[:, ls] = cr
            hc_i_ref[:, ls] = ci
            hr_out_ref[0, :, ls] = cr
            hi_out_ref[0, :, ls] = ci
            seg_r = jnp.concatenate(seg_r, axis=0)
            seg_i = jnp.concatenate(seg_i, axis=0)
            pr, pi = jnp.ones_like(ar), jnp.zeros_like(ar)
            hin_r, hin_i = [], []
            for i in range(steps):
                hin_r.append(loc[i][0] + pr * seg_r - pi * seg_i)
                hin_i.append(loc[i][1] + pr * seg_i + pi * seg_r)
                pr, pi = pr * ar - pi * ai, pr * ai + pi * ar
            hin_r = jnp.concatenate(hin_r, axis=0)
            hin_i = jnp.concatenate(hin_i, axis=0)
        else:
            hin_r, hin_i = h0r_ref[:, ls], h0i_ref[:, ls]
            hr_out_ref[:, ls] = ar * hin_r - ai * hin_i + zr
            hi_out_ref[:, ls] = ar * hin_i + ai * hin_r + zi
        hin.append((hin_r.astype(BF16), hin_i.astype(BF16)))
    y8 = []
    for j in range(S5_TILES):
        lhs = jnp.concatenate([hin[j][0], hin[j][1], x8[j]], axis=1)
        cols = []
        for b in range(k // 256):
            kk = 2 * n + (b + 1) * 256
            cols.append(jnp.dot(lhs[:, :kk], wy_ref[0, j, 0:kk, b * 256:(b + 1) * 256],
                                preferred_element_type=F32))
        y8.append(jnp.concatenate(cols, axis=1))
    dsk = dsk_ref[0]
    for s in range(cs):
        y = jnp.concatenate([y8[j][:, s * LANE:(s + 1) * LANE] for j in range(S5_TILES)], axis=1)
        u_s = jnp.concatenate([us[j][s] for j in range(S5_TILES)], axis=1)
        y = _gelu(y + dsk * u_s)
        y = y * _sigmoid(jnp.dot(y.astype(BF16), wglu_ref[0], preferred_element_type=F32))
        gate = jnp.concatenate([load_rows(g_ref, j, s) for j in range(S5_TILES)], axis=1)
        out = y * _silu(gate)
        for j in range(S5_TILES):
            store_rows(o_ref, j, s, out[:, j * LANE:(j + 1) * LANE])


def _s5_mix(u, g, h0, ops, dsk3, wglu_b, l, cs, nb):
    chain = h0 is None
    wz, wy, ap_re, ap_im = ops
    if chain:
        ntok = SUBLANES * S5_SEG
        assert S5_SEG % cs == 0 and (S5_SEG // cs) & (S5_SEG // cs - 1) == 0
        tiles = u.shape[1] // S5_SEG_PITCH // SUBLANES // nb
        tok = pl.BlockSpec((S5_TILES, SUBLANES * S5_SEG_PITCH, LANE), lambda b, t: (0, b * tiles + t, 0))
    else:
        ntok = u.shape[1]
        tiles = 1
        tok = pl.BlockSpec((S5_TILES, ntok, LANE), lambda b, t: (0, 0, 0))
    rows = ntok // cs
    ap_spec = pl.BlockSpec((1, S5_SCAN_ROWS, S5_LANES), lambda b, t: (l, 0, 0))
    in_specs = [tok, tok]
    args = [u, g]
    scratch = []
    if not chain:
        hs = pl.BlockSpec((rows, S5_LANES), lambda b, t: (0, 0))
        in_specs += [hs, hs]
        args += [h0[0], h0[1]]
        st_spec = hs
        st_shape = jax.ShapeDtypeStruct((rows, S5_LANES), F32)
    else:
        st_spec = pl.BlockSpec((1, 1, S5_LANES), lambda b, t: (b, 0, 0))
        st_shape = jax.ShapeDtypeStruct((nb, 1, S5_LANES), F32)
        scratch = [pltpu.VMEM((1, S5_LANES), F32), pltpu.VMEM((1, S5_LANES), F32)]
    in_specs += [
        pl.BlockSpec((1,) + wz.shape[1:], lambda b, t: (l, 0, 0, 0)),
        pl.BlockSpec((1,) + wy.shape[1:], lambda b, t: (l, 0, 0, 0)),
        ap_spec, ap_spec,
        pl.BlockSpec((1, 1, S5_WIDTH), lambda b, t: (l, 0, 0)),
        pl.BlockSpec((1, S5_WIDTH, S5_WIDTH), lambda b, t: (l, 0, 0)),
    ]
    args += [wz, wy, ap_re, ap_im, dsk3, wglu_b]
    o, hr, hi = pl.pallas_call(
        functools.partial(_s5_kernel, cs=cs, rows=rows, chain=chain),
        grid=(nb, tiles),
        in_specs=in_specs,
        out_specs=[tok, st_spec, st_spec],
        out_shape=[jax.ShapeDtypeStruct(u.shape, F32), st_shape, st_shape],
        scratch_shapes=scratch,
        compiler_params=_cparams(("arbitrary", "arbitrary")),
        name="s5_mix",
    )(*args)
    return o, hr, hi


def _softmax(s):
    e = jnp.exp(s - jnp.max(s, axis=-1, keepdims=True))
    return (e / jnp.sum(e, axis=-1, keepdims=True)).astype(BF16)


def _attend(q, mk_ref, mv_ref):
    heads = [slice(h * X_HEAD_DIM, (h + 1) * X_HEAD_DIM) for h in range(X_HEADS)]
    s = [_mm_nt(q[:, hs], mk_ref[0, 0, :, hs]) for hs in heads]
    p = [_softmax(sh) for sh in s]
    return jnp.concatenate([_mm(ph, mv_ref[0, 0, :, hs]) for ph, hs in zip(p, heads)], axis=-1)


def _attend_tiled(q, mk_ref, mv_ref, nseg, row_seg):
    def sub(ref, i, h, dt):
        return ref[0, i, pl.ds(dt * X_HEADS + h, MEM_LEN, stride=_MEM_SUB), :]

    def pick(vals):
        out = vals[0]
        for i in range(1, nseg):
            out = jnp.where(row_seg == i, vals[i], out)
        return out

    qb = q.astype(BF16)
    qp = lambda h, dt: qb[:, h * X_HEAD_DIM + dt * LANE:h * X_HEAD_DIM + (dt + 1) * LANE]
    s = [[sum(_mm_nt(qp(h, dt), sub(mk_ref, i, h, dt)) for dt in range(_DT)) for i in range(nseg)]
         for h in range(X_HEADS)]
    p = [_softmax(pick(sh)) for sh in s]
    outs = [pick([_mm(p[h], sub(mv_ref, i, h, dt)) for i in range(nseg)])
            for h in range(X_HEADS) for dt in range(_DT)]
    return jnp.concatenate(outs, axis=-1)


def _post_kernel(x_ref, orw_ref, os5_ref, wout_ref, nx_ref, wq_ref, wo_ref, mk_ref, mv_ref, nf_ref,
                 o_ref, *, nseg, seg_rows, final):
    x = x_ref[...]
    if nseg == 1:
        nq = x.shape[0] // S5_SEG
        os5 = jnp.concatenate(
            [jnp.concatenate([os5_ref[j, q * S5_SEG_PITCH:q * S5_SEG_PITCH + S5_SEG, :] for q in range(nq)], axis=0)
             for j in range(S5_TILES)], axis=1)
    else:
        os5 = jnp.concatenate([os5_ref[j] for j in range(S5_TILES)], axis=1)
    x1 = (x + jnp.dot(orw_ref[...].astype(BF16), wout_ref[0, 0:RWKV_WIDTH, :], preferred_element_type=F32)
          + jnp.dot(os5.astype(BF16), wout_ref[0, RWKV_WIDTH:, :], preferred_element_type=F32))
    xc = _rms(x1, nx_ref[0]).astype(BF16)
    q = jnp.dot(xc, wq_ref[0], preferred_element_type=F32) * (1.0 / math.sqrt(X_HEAD_DIM))
    if nseg == 1:
        att = _attend(q, mk_ref, mv_ref)
    else:
        row_seg = _iota((q.shape[0], 1), 0) >> int(math.log2(seg_rows))
        att = _attend_tiled(q, mk_ref, mv_ref, nseg, row_seg)
    x2 = x1 + jnp.dot(att.astype(BF16), wo_ref[0], preferred_element_type=F32)
    if final:
        x2 = _rms(x2, nf_ref[...])
    o_ref[...] = x2


def _post(x2d, orw, os5, wout_b, nx3, wq_b, wo_b, mk, mv, nf2, l, nb, nseg, seg_rows, tq, final):
    tiles = x2d.shape[0] // nb // tq
    tok = lambda w: pl.BlockSpec((tq, w), lambda b, t: (b * tiles + t, 0))
    s5_rows = tq // S5_SEG * S5_SEG_PITCH if nseg == 1 else tq
    tok_s5 = pl.BlockSpec((S5_TILES, s5_rows, LANE), lambda b, t: (0, b * tiles + t, 0))
    if nseg == 1:
        mem = pl.BlockSpec((1, 1, MEM_LEN, D_MODEL), lambda b, t: (l, b, 0, 0))
    else:
        mem = pl.BlockSpec((1, nseg, MEM_LEN * _MEM_SUB, LANE), lambda b, t: (l, t, 0, 0))
    wsp = pl.BlockSpec((1, D_MODEL, D_MODEL), lambda b, t: (l, 0, 0))
    return pl.pallas_call(
        functools.partial(_post_kernel, nseg=nseg, seg_rows=seg_rows, final=final),
        grid=(nb, tiles),
        in_specs=[tok(D_MODEL), tok(RWKV_WIDTH), tok_s5, wsp,
                  pl.BlockSpec((1, 1, D_MODEL), lambda b, t: (l, 0, 0)), wsp, wsp, mem, mem,
                  pl.BlockSpec((1, D_MODEL), lambda b, t: (0, 0))],
        out_specs=tok(D_MODEL),
        out_shape=jax.ShapeDtypeStruct(x2d.shape, F32),
        compiler_params=_cparams(("arbitrary", "arbitrary")),
        name="post",
    )(x2d, orw, os5, wout_b, nx3, wq_b, wo_b, mk, mv, nf2)


def _trunk(x, states, mem_k, mem_v, W):
    batch, seq, _ = x.shape
    prompt = states is None
    x2d = x.reshape(batch * seq, D_MODEL)
    new_shift, new_wkv, new_re, new_im = [], [], [], []
    vfirst = None
    wkv_all = None if prompt else states[1]
    for l in range(DEPTH):
        rw_args = (W['mu'], W['vecs'], W['wla'], W['v1'], W['v2'], l)
        if prompt:
            orw, vfirst, us5, gs5, sh, wkv = _rwkv_prompt(x2d, W['norm_mix'], W['w_in'], vfirst, *rw_args,
                                                          batch, seq)
            os5, hr, hi = _s5_mix(us5, gs5, None, W['s5_prompt'], W['d_skip'], W['w_glu'], l, 8, batch)
            hr = hr.reshape(batch, S5_GROUPS, S5_STATE)
            hi = hi.reshape(batch, S5_GROUPS, S5_STATE)
            x2d = _post(x2d, orw, os5, W['w_out'], W['norm_x'], W['wq'], W['wo'], mem_k, mem_v,
                        W['norm_f'], l, batch, 1, seq, 1024, l == DEPTH - 1)
        else:
            shift0, _, re0, im0 = states
            psh, grw, us5, gs5 = _inproj(x2d, W['norm_mix'], W['w_in'], l)
            prev_rows = jnp.repeat(shift0[l], seq, axis=0)
            orw, vfirst, sh, wkv_all = _rwkv_sample(psh, grw, vfirst, prev_rows, wkv_all, *rw_args, seq)
            h0 = (re0[l].reshape(batch, S5_LANES), im0[l].reshape(batch, S5_LANES))
            os5, hr, hi = _s5_mix(us5, gs5, h0, W['s5_sample'], W['d_skip'], W['w_glu'], l, seq, 1)
            hr = hr.reshape(batch, S5_GROUPS, S5_STATE)
            hi = hi.reshape(batch, S5_GROUPS, S5_STATE)
            nseg = 32 // seq
            x2d = _post(x2d, orw, os5, W['w_out'], W['norm_x'], W['wq'], W['wo'], mem_k, mem_v,
                        W['norm_f'], l, 1, nseg, seq, nseg * seq, l == DEPTH - 1)
        new_shift.append(sh)
        if prompt:
            new_wkv.append(wkv)
        new_re.append(hr)
        new_im.append(hi)
    y = x2d.reshape(batch, seq, D_MODEL)
    wkv_out = jnp.stack(new_wkv) if prompt else wkv_all
    return y, jnp.stack(new_shift), wkv_out, jnp.stack(new_re), jnp.stack(new_im)


def kernel(x_prompt, x_sample, state_shift, state_wkv, state_s5_re, state_s5_im, cache_mem_k, cache_mem_v, mem_prompt, norm_mix, w_in, mu_shift, w0, w2, a0, a2, v0, v1, v2, k_k, k_a, r_k, gn_w, gn_b, lam_re, lam_im, log_dt, b_re, b_im, c_re, c_im, d_skip, w_glu, w_out, norm_x, norm_mem, wq, wk, wv, wo, norm_f):
    L = DEPTH
    bp, mp = mem_prompt.shape[0], mem_prompt.shape[1]
    zpad = jnp.zeros((1, RWKV_WIDTH), F32)
    vecs = jnp.stack([w0, a0, k_k, k_a, gn_w, gn_b, r_k.reshape(L, RWKV_WIDTH),
                      jnp.concatenate([zpad, v0], axis=0)], axis=1)
    zl = jnp.zeros((L, LORA_W, RWKV_WIDTH), F32)
    wla = jnp.concatenate([jnp.concatenate([w2, zl], axis=2),
                           jnp.concatenate([zl, a2], axis=2)], axis=1).astype(BF16)
    tr = lambda t: jnp.transpose(t, (0, 3, 1, 2)).reshape(L, S5_GROUP, S5_LANES)
    s5_in = (lam_re.reshape(L, 1, S5_LANES), lam_im.reshape(L, 1, S5_LANES),
             jnp.repeat(log_dt, S5_STATE, axis=1).reshape(L, 1, S5_LANES),
             tr(b_re), tr(b_im),
             jnp.transpose(c_re, (0, 2, 1, 3)).reshape(L, S5_GROUP, S5_LANES),
             jnp.transpose(c_im, (0, 2, 1, 3)).reshape(L, S5_GROUP, S5_LANES))
    W = dict(
        norm_mix=norm_mix.reshape(L, 1, D_MODEL), w_in=w_in.astype(BF16),
        mu=mu_shift.reshape(L, 1, SHIFT_COLS), vecs=vecs, wla=wla,
        v1=v1.astype(BF16), v2=v2.astype(BF16),
        s5_prompt=_s5_prep(*s5_in, 8), s5_sample=_s5_prep(*s5_in, x_sample.shape[1]),
        d_skip=d_skip.reshape(L, 1, S5_WIDTH), w_glu=w_glu.astype(BF16),
        w_out=w_out.astype(BF16), norm_x=norm_x.reshape(L, 1, D_MODEL),
        wq=wq.astype(BF16), wo=wo.astype(BF16), norm_f=norm_f.reshape(1, D_MODEL),
    )
    pk, pv, pk_t, pv_t = _memkv(mem_prompt.reshape(bp * mp, D_MODEL), norm_mem,
                                wk.astype(BF16), wv.astype(BF16))
    pk = pk.reshape(L, bp, mp, D_MODEL)
    pv = pv.reshape(L, bp, mp, D_MODEL)
    y_prompt, p_shift, p_wkv, p_re, p_im = _trunk(x_prompt, None, pk, pv, W)
    tiled = lambda c: jnp.transpose(c.reshape(L, -1, MEM_LEN, X_HEADS, _DT, LANE),
                                    (0, 1, 2, 4, 3, 5)).reshape(L, -1, MEM_LEN * _MEM_SUB, LANE)
    untiled = lambda c: jnp.transpose(c.reshape(L, -1, MEM_LEN, _DT, X_HEADS, LANE),
                                      (0, 1, 2, 4, 3, 5)).reshape(L, -1, MEM_LEN, X_HEADS, X_HEAD_DIM)
    y_sample, s_shift, s_wkv, s_re, s_im = _trunk(
        x_sample, (state_shift, state_wkv, state_s5_re, state_s5_im), tiled(cache_mem_k),
        tiled(cache_mem_v), W)
    return (y_prompt, y_sample, p_shift, p_wkv, p_re, p_im, untiled(pk_t), untiled(pv_t),
            s_shift, s_wkv, s_re, s_im)
```

```python
import functools
import math

import jax
import jax.numpy as jnp
from jax import lax
from jax.experimental import pallas as pl
from jax.experimental.pallas import tpu as pltpu

F32 = jnp.float32
BF16 = jnp.bfloat16

D_MODEL = 1024
DEPTH = 4
RWKV_WIDTH = 512
RWKV_HEAD = 64
RWKV_HEADS = 8
LORA_W = 64
LORA_A = 64
LORA_V = 32
S5_WIDTH = 512
S5_GROUP = 16
S5_GROUPS = 32
S5_STATE = 64
S5_LANES = S5_GROUPS * S5_STATE
MEM_LEN = 256
X_HEADS = 4
X_HEAD_DIM = 256
SHIFT_COLS = 3 * RWKV_WIDTH + LORA_W + LORA_A
IN_COLS = SHIFT_COLS + RWKV_WIDTH + 2 * S5_WIDTH
NORM_EPS = 1e-6
GN_EPS = 64e-5

LANE = 128
SUBLANES = 8
WKV_CHUNK = 64
HALF_HEADS = 4
WKV_UNROLL = 8
S5_TILES = S5_WIDTH // LANE
GROUPS_PER_TILE = LANE // S5_GROUP
TILE_STATES = GROUPS_PER_TILE * S5_STATE
S5_SCAN_ROWS = 8
S5_SEG = 128
S5_SEG_PITCH = S5_SEG + SUBLANES
VMEM_LIMIT = 56 * 1024 * 1024


def _cparams(sem):
    return pltpu.CompilerParams(dimension_semantics=sem, vmem_limit_bytes=VMEM_LIMIT)


def _mm(a, b):
    return jnp.dot(a.astype(BF16), b.astype(BF16), preferred_element_type=F32)


def _mm_nt(a, b):
    return lax.dot_general(a.astype(BF16), b.astype(BF16), (((1,), (1,)), ((), ())),
                           preferred_element_type=F32)


def _mm_tn(a, b):
    return lax.dot_general(a.astype(BF16), b.astype(BF16), (((0,), (0,)), ((), ())),
                           preferred_element_type=F32)


def _mm_f32(a, b):
    return jnp.dot(a, b, preferred_element_type=F32, precision=lax.Precision.HIGHEST)


def _mm_nt_x3(a, b):
    a_hi, b_hi = a.astype(BF16), b.astype(BF16)
    a_lo = (a - a_hi.astype(F32)).astype(BF16)
    b_lo = (b - b_hi.astype(F32)).astype(BF16)
    return _mm_nt(a_hi, b_hi) + _mm_nt(a_hi, b_lo) + _mm_nt(a_lo, b_hi)


def _rms(x, g):
    return x * lax.rsqrt(jnp.mean(x * x, axis=-1, keepdims=True) + NORM_EPS) * g


def _sigmoid(x):
    return 1.0 / (1.0 + jnp.exp(-x))


def _silu(x):
    return x * _sigmoid(x)


def _iota(shape, dim):
    return lax.broadcasted_iota(jnp.int32, shape, dim)


_DT = X_HEAD_DIM // LANE
_MEM_SUB = X_HEADS * _DT


def _memkv_kernel(m_ref, g_ref, wk_ref, wv_ref, k_ref, v_ref, kt_ref, vt_ref):
    mn = _rms(m_ref[...], g_ref[0]).astype(BF16)
    tm = m_ref.shape[0]
    for w_ref, o_ref, t_ref in ((wk_ref, k_ref, kt_ref), (wv_ref, v_ref, vt_ref)):
        kv = jnp.dot(mn, w_ref[0], preferred_element_type=F32)
        o_ref[0] = kv
        for h in range(X_HEADS):
            for dt in range(_DT):
                lo = h * X_HEAD_DIM + dt * LANE
                t_ref[0, pl.ds(dt * X_HEADS + h, tm, stride=_MEM_SUB), :] = kv[:, lo:lo + LANE]


def _memkv(mem2d, norm_mem, wk_b, wv_b):
    rows = mem2d.shape[0]
    tm = 512
    out = jax.ShapeDtypeStruct((DEPTH, rows, D_MODEL), F32)
    out_t = jax.ShapeDtypeStruct((DEPTH, rows * _MEM_SUB, LANE), F32)
    nat = pl.BlockSpec((1, tm, D_MODEL), lambda l, i: (l, i, 0))
    til = pl.BlockSpec((1, tm * _MEM_SUB, LANE), lambda l, i: (l, i, 0))
    return pl.pallas_call(
        _memkv_kernel,
        grid=(DEPTH, rows // tm),
        in_specs=[
            pl.BlockSpec((tm, D_MODEL), lambda l, i: (i, 0)),
            pl.BlockSpec((1, 1, D_MODEL), lambda l, i: (l, 0, 0)),
            pl.BlockSpec((1, D_MODEL, D_MODEL), lambda l, i: (l, 0, 0)),
            pl.BlockSpec((1, D_MODEL, D_MODEL), lambda l, i: (l, 0, 0)),
        ],
        out_specs=[nat, nat, til, til],
        out_shape=[out, out, out_t, out_t],
        compiler_params=_cparams(("arbitrary", "arbitrary")),
        name="memkv",
    )(mem2d, norm_mem.reshape(DEPTH, 1, D_MODEL), wk_b, wv_b)


_IN_SPLITS = (0, SHIFT_COLS, SHIFT_COLS + RWKV_WIDTH, SHIFT_COLS + RWKV_WIDTH + S5_WIDTH, IN_COLS)


def _inproj_kernel(x_ref, g_ref, w_ref, psh_ref, grw_ref, us5_ref, gs5_ref):
    xn = _rms(x_ref[...], g_ref[0]).astype(BF16)
    proj = lambda lo, hi: jnp.dot(xn, w_ref[0, :, lo:hi], preferred_element_type=F32)
    psh_ref[...] = proj(_IN_SPLITS[0], _IN_SPLITS[1])
    grw_ref[...] = proj(_IN_SPLITS[1], _IN_SPLITS[2])
    for o_ref, lo in ((us5_ref, _IN_SPLITS[2]), (gs5_ref, _IN_SPLITS[3])):
        cols = proj(lo, lo + S5_WIDTH)
        for j in range(S5_TILES):
            o_ref[j] = cols[:, j * LANE:(j + 1) * LANE]


def _inproj(x2d, norm_mix3, w_in_b, l):
    rows = x2d.shape[0]
    tm = 256
    s5_spec = pl.BlockSpec((S5_TILES, tm, LANE), lambda i: (0, i, 0))
    s5_shape = jax.ShapeDtypeStruct((S5_TILES, rows, LANE), F32)
    return pl.pallas_call(
        _inproj_kernel,
        grid=(rows // tm,),
        in_specs=[
            pl.BlockSpec((tm, D_MODEL), lambda i: (i, 0)),
            pl.BlockSpec((1, 1, D_MODEL), lambda i: (l, 0, 0)),
            pl.BlockSpec((1, D_MODEL, IN_COLS), lambda i: (l, 0, 0)),
        ],
        out_specs=[pl.BlockSpec((tm, SHIFT_COLS), lambda i: (i, 0)),
                   pl.BlockSpec((tm, RWKV_WIDTH), lambda i: (i, 0)), s5_spec, s5_spec],
        out_shape=[jax.ShapeDtypeStruct((rows, SHIFT_COLS), F32),
                   jax.ShapeDtypeStruct((rows, RWKV_WIDTH), F32), s5_shape, s5_shape],
        compiler_params=_cparams(("arbitrary",)),
        name="inproj",
    )(x2d, norm_mix3, w_in_b)


_V_W0, _V_A0, _V_KK, _V_KA, _V_GNW, _V_GNB, _V_RK, _V_V0 = range(8)


def _rwkv_prep(ps, vfirst, vecs, wla, v1, v2):
    r = ps[:, 0:RWKV_WIDTH]
    k = ps[:, RWKV_WIDTH:2 * RWKV_WIDTH]
    v = ps[:, 2 * RWKV_WIDTH:3 * RWKV_WIDTH]
    lora_in = ps[:, 3 * RWKV_WIDTH:SHIFT_COLS]
    lane = _iota(lora_in.shape, 1)
    lora_in = jnp.where(lane < LORA_W, jnp.tanh(lora_in), lora_in)
    lora = _mm(lora_in, wla)
    wpre = vecs[_V_W0:_V_W0 + 1] + lora[:, 0:RWKV_WIDTH]
    z = -wpre
    softplus = jnp.maximum(z, 0.0) + jnp.log(1.0 + jnp.exp(-jnp.abs(z)))
    logdecay = -jnp.exp(-softplus - 0.5)
    a = _sigmoid(vecs[_V_A0:_V_A0 + 1] + lora[:, RWKV_WIDTH:2 * RWKV_WIDTH])
    if vfirst is not None:
        gate = _sigmoid(vecs[_V_V0:_V_V0 + 1] + _mm(_mm(v, v1), v2))
        v = v + (vfirst - v) * gate
    kku = k * vecs[_V_KK:_V_KK + 1]
    kmod = k * (1.0 + (a - 1.0) * vecs[_V_KA:_V_KA + 1])
    return r, kku, kmod, v, a, logdecay


def _wkv_chunk(lw, load_head, states, seg_len, vecs):
    C = WKV_CHUNK
    shift = int(math.log2(seg_len))
    per_grp = SUBLANES // seg_len
    ngrp = C // SUBLANES
    row = _iota((C, C), 0)
    col = _iota((C, C), 1)
    same = (row >> shift) == (col >> shift)
    incl = jnp.logical_and(col <= row, same)
    strict = jnp.logical_and(col < row, same)
    cum = _mm_f32(incl.astype(F32), lw)
    tot = _mm_f32(same.astype(F32), lw)
    g_inc_all = jnp.exp(cum)
    g_exc_all = jnp.exp(cum - lw)
    g_inv_all = jnp.exp(-cum)
    g_end_all = jnp.exp(tot - cum)
    g_tot_all = jnp.exp(tot)
    eye = (row == col).astype(F32)
    nsq = max(shift - 1, 0)
    seg16 = (_iota((2 * SUBLANES, 1), 0) & (SUBLANES - 1)) >> shift

    hd = []
    for h in range(RWKV_HEADS):
        sl = slice(h * RWKV_HEAD, (h + 1) * RWKV_HEAD)
        r, kku, kmod, v, a = load_head(h)
        ss = jnp.sum(kku * kku, axis=-1, keepdims=True)
        kkn = kku * lax.rsqrt(jnp.maximum(ss, 1e-24))
        b = kkn * a
        qk = kkn * g_exc_all[:, sl]
        qr = r * g_inc_all[:, sl]
        hd.append(dict(sl=sl, r=r, kmod=kmod, v=v, qk=qk, qr=qr,
                       q2=jnp.concatenate([qk, qr], axis=0).astype(BF16),
                       kd=kmod * g_inv_all[:, sl], bd=b * g_inv_all[:, sl],
                       kc=kmod * g_end_all[:, sl], bc=b * g_end_all[:, sl]))
    for d in hd:
        d['ak'] = _mm_nt(d['q2'], d['kd'])
        d['ab'] = _mm_nt(d['q2'], d['bd'])
    for d in hd:
        d['akk'] = jnp.where(strict, d['ak'][:C], 0.0)
        d['ark'] = jnp.where(incl, d['ak'][C:], 0.0)
        d['arb'] = jnp.where(incl, d['ab'][C:], 0.0)
        d['n'] = -jnp.where(strict, d['ab'][:C], 0.0)
        d['tinv'] = eye + d['n']
    for _ in range(nsq):
        for d in hd:
            d['n'] = _mm(d['n'], d['n'])
        for d in hd:
            d['tinv'] = d['tinv'] + _mm(d['tinv'], d['n'])
    for d in hd:
        d['av'] = _mm(jnp.concatenate([d['akk'], d['ark']], axis=0), d['v'])
    for d in hd:
        d['u_ind'] = _mm(d['tinv'], d['av'][:C])
        d['tq'] = _mm(d['tinv'], d['qk'])
    for h, d in enumerate(hd):
        us, yq = [], []
        for g in range(ngrp):
            rs = slice(g * SUBLANES, (g + 1) * SUBLANES)
            lhs = jnp.concatenate([d['tq'][rs], d['qr'][rs]], axis=0).astype(BF16)
            qs = None
            for s in range(per_grp):
                o = _mm_nt(lhs, states[h][g * per_grp + s])
                qs = o if qs is None else jnp.where(seg16 == s, o, qs)
            us.append(d['u_ind'][rs] + qs[:SUBLANES])
            yq.append(qs[SUBLANES:])
        d['u'] = jnp.concatenate(us, axis=0)
        d['yq'] = jnp.concatenate(yq, axis=0)
    for d in hd:
        d['y'] = d['yq'] + d['av'][C:] - _mm(d['arb'], d['u'])
    new_states = []
    for h, d in enumerate(hd):
        hs = []
        for g in range(ngrp):
            rs = slice(g * SUBLANES, (g + 1) * SUBLANES)
            vu = jnp.concatenate([d['v'][rs], d['u'][rs]], axis=0)
            kb = jnp.concatenate([d['kc'][rs], -d['bc'][rs]], axis=0).astype(BF16)
            for s in range(per_grp):
                i = g * per_grp + s
                gt = g_tot_all[i * seg_len:i * seg_len + 1, d['sl']]
                hs.append(states[h][i] * gt + _mm_tn(jnp.where(seg16 == s, vu, 0.0), kb))
        new_states.append(hs)
    ys = []
    for d in hd:
        y, sl = d['y'], d['sl']
        mu = jnp.mean(y, axis=-1, keepdims=True)
        yc = y - mu
        var = jnp.mean(yc * yc, axis=-1, keepdims=True)
        yn = yc * lax.rsqrt(var + GN_EPS) * vecs[_V_GNW:_V_GNW + 1, sl] + vecs[_V_GNB:_V_GNB + 1, sl]
        bonus = jnp.sum(d['r'] * d['kmod'] * vecs[_V_RK:_V_RK + 1, sl], axis=-1, keepdims=True)
        ys.append(yn + bonus * d['v'])
    return jnp.concatenate(ys, axis=-1), new_states


def _rwkv_prompt_kernel(*refs, first_layer, tt):
    if first_layer:
        (x_ref, nm_ref, win_ref, mu_ref, vecs_ref, wla_ref, v1_ref, v2_ref,
         o_ref, vf_out_ref, us5_ref, gs5_ref, shift_out_ref, wkv_out_ref,
         carry_ref, s_ref, r_s, kkn_s, kmod_s, v_s, b_s, lw_s, y_s) = refs
        vf_ref = None
    else:
        (x_ref, nm_ref, win_ref, vf_ref, mu_ref, vecs_ref, wla_ref, v1_ref, v2_ref,
         o_ref, us5_ref, gs5_ref, shift_out_ref, wkv_out_ref,
         carry_ref, s_ref, r_s, kkn_s, kmod_s, v_s, b_s, lw_s, y_s) = refs
        vf_out_ref = None
    t = pl.program_id(1)

    @pl.when(t == 0)
    def _():
        carry_ref[...] = jnp.zeros_like(carry_ref)
        s_ref[...] = jnp.zeros_like(s_ref)

    xn = _rms(x_ref[0], nm_ref[0]).astype(BF16)
    proj = lambda lo, hi: jnp.dot(xn, win_ref[0, :, lo:hi], preferred_element_type=F32)
    p = proj(_IN_SPLITS[0], _IN_SPLITS[1])
    o_ref[0] = proj(_IN_SPLITS[1], _IN_SPLITS[2])
    pad = jnp.zeros((S5_SEG_PITCH - S5_SEG, LANE), F32)
    for s5_ref, lo in ((us5_ref, _IN_SPLITS[2]), (gs5_ref, _IN_SPLITS[3])):
        cols = proj(lo, lo + S5_WIDTH)
        for j in range(S5_TILES):
            for q in range(tt // S5_SEG):
                s5_ref[j, q * S5_SEG_PITCH:q * S5_SEG_PITCH + S5_SEG, :] = (
                    cols[q * S5_SEG:(q + 1) * S5_SEG, j * LANE:(j + 1) * LANE])
                s5_ref[j, q * S5_SEG_PITCH + S5_SEG:(q + 1) * S5_SEG_PITCH, :] = pad
    rowi = _iota(p.shape, 0)
    prev = jnp.where(rowi == 0, carry_ref[...], pltpu.roll(p, 1, 0))
    ps = p + (prev - p) * mu_ref[0]
    last = p[tt - 1:tt, :]
    carry_ref[...] = last
    shift_out_ref[0] = last

    vecs = vecs_ref[0]
    vfirst = None if first_layer else vf_ref[0]
    r, kku, kmod, v, a, lw = _rwkv_prep(ps, vfirst, vecs, wla_ref[0], v1_ref[0], v2_ref[0])
    if first_layer:
        vf_out_ref[0] = v

    hw = HALF_HEADS * RWKV_HEAD
    bd_bool = (_iota((hw, hw), 0) >> 6) == (_iota((hw, hw), 1) >> 6)
    bd_ones = bd_bool.astype(F32).astype(BF16)

    def head_sums(x):
        n = x.shape[0]
        hi = x.astype(BF16)
        lo = (x - hi.astype(F32)).astype(BF16)
        st = jnp.concatenate([hi, lo], axis=0)
        outs = []
        for hf in range(2):
            s2 = jnp.dot(st[:, hf * hw:(hf + 1) * hw], bd_ones, preferred_element_type=F32)
            outs.append(s2[:n] + s2[n:])
        return jnp.concatenate(outs, axis=1)

    kkn = kku * lax.rsqrt(jnp.maximum(head_sums(kku * kku), 1e-24))
    r_s[...] = r
    kkn_s[...] = kkn
    kmod_s[...] = kmod
    v_s[...] = v
    b_s[...] = kkn * a
    lw_s[...] = lw

    C = WKV_CHUNK
    trow = _iota((C, hw), 0)
    tcol = _iota((C, hw), 1) & (RWKV_HEAD - 1)
    strict = tcol < trow
    incl = tcol <= trow
    eye = (tcol == trow).astype(F32)
    crow = _iota((C, RWKV_WIDTH), 0)

    def bd(x):
        return jnp.concatenate([x.astype(BF16)] * HALF_HEADS, axis=0) * bd_ones

    def group(g, carry):
        chains = []
        for k in range(WKV_UNROLL):
            rows = pl.ds(pl.multiple_of((g * WKV_UNROLL + k) * C, C), C)
            lwc = lw_s[rows, :]
            cum = lwc
            for d in (1, 2, 4):
                cum = cum + jnp.where(crow >= d, pltpu.roll(cum, d, 0), 0.0)
            for d in (8, 16, 32):
                cum = jnp.concatenate([cum[:d], cum[d:] + cum[:C - d]], axis=0)
            tot = cum[C - 1:C, :]
            g_inc = jnp.exp(cum)
            g_exc = jnp.exp(cum - lwc)
            g_inv = jnp.exp(-cum)
            g_end = jnp.exp(tot - cum)
            g_tot = jnp.exp(tot)
            rr, kk, km, bb, vv = (r_s[rows, :], kkn_s[rows, :], kmod_s[rows, :], b_s[rows, :],
                                  v_s[rows, :])
            qk_a, qr_a = kk * g_exc, rr * g_inc
            kd_a, bd_a = km * g_inv, bb * g_inv
            kc_a, bc_a = km * g_end, bb * g_end
            for hf in range(2):
                ls = slice(hf * hw, (hf + 1) * hw)
                chains.append(dict(k=k, hf=hf, rows=rows, qk=qk_a[:, ls], qr=qr_a[:, ls], kd=kd_a[:, ls],
                                   bd=bd_a[:, ls], kc=kc_a[:, ls], bc=bc_a[:, ls], v=vv[:, ls],
                                   g_tot=g_tot[:, ls]))
        for ch in chains:
            ch['q2'] = jnp.concatenate([ch['qk'], ch['qr']], axis=0).astype(BF16)
            ch['ak'] = _mm_nt(ch['q2'], bd(ch['kd']))
        for ch in chains:
            ab = _mm_nt(ch['q2'], bd(ch['bd']))
            ch['akk'] = jnp.where(strict, ch['ak'][:C], 0.0)
            ch['ark'] = jnp.where(incl, ch['ak'][C:], 0.0)
            ch['arb'] = jnp.where(incl, ab[C:], 0.0)
            ch['n'] = -jnp.where(strict, ab[:C], 0.0)
            ch['tinv'] = eye + ch['n']
        for ch in chains:
            ch['npow'] = _mm(ch['n'], bd(ch['n']))
        for _ in range(4):
            for ch in chains:
                st = _mm(jnp.concatenate([ch['tinv'], ch['npow']], axis=0), bd(ch['npow']))
                ch['tinv'] = ch['tinv'] + st[:C]
                ch['npow'] = st[C:]
        for ch in chains:
            ch['av'] = _mm(jnp.concatenate([ch['akk'], ch['ark']], axis=0), bd(ch['v']))
        for ch in chains:
            ch['tinv'] = ch['tinv'] + _mm(ch['tinv'], bd(ch['npow']))
        for ch in chains:
            ch['u_ind'] = _mm(ch['tinv'], bd(ch['av'][:C]))
            ch['tq'] = _mm(ch['tinv'], bd(ch['qk']))
        for ch in chains:
            ch['gc'] = jnp.where(bd_bool, _mm_tn(ch['tq'], ch['bc']), 0.0).astype(BF16)
            ch['nc'] = jnp.where(bd_bool, _mm_tn(jnp.concatenate([ch['v'], ch['u_ind']], axis=0),
                                                 jnp.concatenate([ch['kc'], -ch['bc']], axis=0)), 0.0)
        state = [s_ref[0], s_ref[1]]
        for ch in chains:
            s_old = state[ch['hf']]
            ch['s_b'] = s_old.astype(BF16)
            state[ch['hf']] = s_old * ch['g_tot'] - _mm(ch['s_b'], ch['gc']) + ch['nc']
        s_ref[0] = state[0]
        s_ref[1] = state[1]
        for ch in chains:
            ch['qs'] = _mm_nt(jnp.concatenate([ch['tq'], ch['qr']], axis=0), ch['s_b'])
        for ch in chains:
            u = ch['u_ind'] + ch['qs'][:C]
            ch['y'] = ch['qs'][C:] + ch['av'][C:] - _mm(ch['arb'], bd(u))
        for k in range(WKV_UNROLL):
            y_s[chains[2 * k]['rows'], :] = jnp.concatenate([chains[2 * k]['y'], chains[2 * k + 1]['y']], axis=1)
        return carry

    lax.fori_loop(0, tt // (C * WKV_UNROLL), group, 0)

    y = y_s[...]
    inv_n = 1.0 / RWKV_HEAD
    mu = head_sums(y) * inv_n
    yc = y - mu
    var = head_sums(yc * yc) * inv_n
    yn = yc * lax.rsqrt(var + GN_EPS) * vecs[_V_GNW:_V_GNW + 1] + vecs[_V_GNB:_V_GNB + 1]
    rr, km, vv = r_s[...], kmod_s[...], v_s[...]
    bonus = head_sums(rr * km * vecs[_V_RK:_V_RK + 1])
    o_ref[0] = (yn + bonus * vv) * _silu(o_ref[0])

    @pl.when(t == pl.num_programs(1) - 1)
    def _():
        for h in range(RWKV_HEADS):
            o = (h % HALF_HEADS) * RWKV_HEAD
            wkv_out_ref[0, h] = s_ref[h // HALF_HEADS, o:o + RWKV_HEAD, o:o + RWKV_HEAD]


def _rwkv_prompt(x2d, norm_mix3, w_in_b, vfirst, mu3, vecs, wla, v1, v2, l, batch, seq):
    tt = 512
    tiles = seq // tt
    first = vfirst is None
    lv = max(l - 1, 0)
    tok = lambda w: pl.BlockSpec((1, tt, w), lambda b, t: (b, t, 0))
    s5_spec = pl.BlockSpec((S5_TILES, tt // S5_SEG * S5_SEG_PITCH, LANE), lambda b, t: (0, b * tiles + t, 0))
    s5_shape = jax.ShapeDtypeStruct((S5_TILES, batch * seq // S5_SEG * S5_SEG_PITCH, LANE), F32)
    in_specs = [tok(D_MODEL),
                pl.BlockSpec((1, 1, D_MODEL), lambda b, t: (l, 0, 0)),
                pl.BlockSpec((1, D_MODEL, IN_COLS), lambda b, t: (l, 0, 0))]
    args = [x2d.reshape(batch, seq, D_MODEL), norm_mix3, w_in_b]
    if not first:
        in_specs.append(tok(RWKV_WIDTH))
        args.append(vfirst.reshape(batch, seq, RWKV_WIDTH))
    in_specs += [
        pl.BlockSpec((1, 1, SHIFT_COLS), lambda b, t: (l, 0, 0)),
        pl.BlockSpec((1, 8, RWKV_WIDTH), lambda b, t: (l, 0, 0)),
        pl.BlockSpec((1, LORA_W + LORA_A, 2 * RWKV_WIDTH), lambda b, t: (l, 0, 0)),
        pl.BlockSpec((1, RWKV_WIDTH, LORA_V), lambda b, t: (lv, 0, 0)),
        pl.BlockSpec((1, LORA_V, RWKV_WIDTH), lambda b, t: (lv, 0, 0)),
    ]
    args += [mu3, vecs, wla, v1, v2]
    out_specs = [tok(RWKV_WIDTH)]
    out_shape = [jax.ShapeDtypeStruct((batch, seq, RWKV_WIDTH), F32)]
    if first:
        out_specs.append(tok(RWKV_WIDTH))
        out_shape.append(jax.ShapeDtypeStruct((batch, seq, RWKV_WIDTH), F32))
    out_specs += [
        s5_spec, s5_spec,
        pl.BlockSpec((1, 1, SHIFT_COLS), lambda b, t: (b, 0, 0)),
        pl.BlockSpec((1, RWKV_HEADS, RWKV_HEAD, RWKV_HEAD), lambda b, t: (b, 0, 0, 0)),
    ]
    out_shape += [
        s5_shape, s5_shape,
        jax.ShapeDtypeStruct((batch, 1, SHIFT_COLS), F32),
        jax.ShapeDtypeStruct((batch, RWKV_HEADS, RWKV_HEAD, RWKV_HEAD), F32),
    ]
    scratch = [
        pltpu.VMEM((1, SHIFT_COLS), F32),
        pltpu.VMEM((2, HALF_HEADS * RWKV_HEAD, HALF_HEADS * RWKV_HEAD), F32),
    ] + [pltpu.VMEM((tt, RWKV_WIDTH), F32) for _ in range(7)]
    outs = pl.pallas_call(
        functools.partial(_rwkv_prompt_kernel, first_layer=first, tt=tt),
        grid=(batch, seq // tt),
        in_specs=in_specs,
        out_specs=out_specs,
        out_shape=out_shape,
        scratch_shapes=scratch,
        compiler_params=_cparams(("arbitrary", "arbitrary")),
        name="rwkv_prompt",
    )(*args)
    if first:
        o, vf, us5, gs5, sh, wkv = outs
    else:
        (o, us5, gs5, sh, wkv), vf = outs, vfirst
    return (o.reshape(batch * seq, RWKV_WIDTH), vf.reshape(batch * seq, RWKV_WIDTH), us5, gs5,
            sh[:, 0], wkv)


def _rwkv_sample_kernel(*refs, first_layer, seq):
    if first_layer:
        (psh_ref, grw_ref, prev_ref, wkv_ref, mu_ref, vecs_ref, wla_ref, v1_ref, v2_ref,
         o_ref, vf_out_ref, wkv_out_ref) = refs
        vf_ref = None
    else:
        (psh_ref, grw_ref, vf_ref, prev_ref, wkv_ref, mu_ref, vecs_ref, wla_ref, v1_ref, v2_ref,
         o_ref, wkv_out_ref) = refs
        vf_out_ref = None
    nseq = WKV_CHUNK // seq
    p = psh_ref[...]
    rowi = _iota(p.shape, 0)
    prev = jnp.where((rowi & (seq - 1)) == 0, prev_ref[...], pltpu.roll(p, 1, 0))
    ps = p + (prev - p) * mu_ref[0]

    vecs = vecs_ref[0]
    vfirst = None if first_layer else vf_ref[...]
    r, kku, kmod, v, a, lw = _rwkv_prep(ps, vfirst, vecs, wla_ref[0], v1_ref[0], v2_ref[0])
    if first_layer:
        vf_out_ref[...] = v

    def load_head(h):
        sl = slice(h * RWKV_HEAD, (h + 1) * RWKV_HEAD)
        return (r[:, sl], kku[:, sl], kmod[:, sl], v[:, sl], a[:, sl])

    states = [[wkv_ref[0, s, h] for s in range(nseq)] for h in range(RWKV_HEADS)]
    y, new_states = _wkv_chunk(lw, load_head, states, seq, vecs)
    for h in range(RWKV_HEADS):
        for s in range(nseq):
            wkv_out_ref[0, s, h] = new_states[h][s]
    o_ref[...] = y * _silu(grw_ref[...])


def _rwkv_sample(psh, grw, vfirst, prev_rows, wkv_all, mu3, vecs, wla, v1, v2, l, seq):
    rows = psh.shape[0]
    nseq = WKV_CHUNK // seq
    first = vfirst is None
    lv = max(l - 1, 0)
    tok = lambda w: pl.BlockSpec((WKV_CHUNK, w), lambda i: (i, 0))
    st = pl.BlockSpec((1, nseq, RWKV_HEADS, RWKV_HEAD, RWKV_HEAD), lambda i: (l, i, 0, 0, 0))
    in_specs = [tok(SHIFT_COLS), tok(RWKV_WIDTH)]
    args = [psh, grw]
    if not first:
        in_specs.append(tok(RWKV_WIDTH))
        args.append(vfirst)
    in_specs += [
        tok(SHIFT_COLS), st,
        pl.BlockSpec((1, 1, SHIFT_COLS), lambda i: (l, 0, 0)),
        pl.BlockSpec((1, 8, RWKV_WIDTH), lambda i: (l, 0, 0)),
        pl.BlockSpec((1, LORA_W + LORA_A, 2 * RWKV_WIDTH), lambda i: (l, 0, 0)),
        pl.BlockSpec((1, RWKV_WIDTH, LORA_V), lambda i: (lv, 0, 0)),
        pl.BlockSpec((1, LORA_V, RWKV_WIDTH), lambda i: (lv, 0, 0)),
    ]
    wkv_arg = len(args) + 1
    args += [prev_rows, wkv_all, mu3, vecs, wla, v1, v2]
    out_specs = [tok(RWKV_WIDTH)]
    out_shape = [jax.ShapeDtypeStruct((rows, RWKV_WIDTH), F32)]
    if first:
        out_specs.append(tok(RWKV_WIDTH))
        out_shape.append(jax.ShapeDtypeStruct((rows, RWKV_WIDTH), F32))
    out_specs.append(st)
    out_shape.append(jax.ShapeDtypeStruct(wkv_all.shape, F32))
    outs = pl.pallas_call(
        functools.partial(_rwkv_sample_kernel, first_layer=first, seq=seq),
        grid=(rows // WKV_CHUNK,),
        in_specs=in_specs,
        out_specs=out_specs,
        out_shape=out_shape,
        input_output_aliases={wkv_arg: len(out_shape) - 1},
        compiler_params=_cparams(("arbitrary",)),
        name="rwkv_sample",
    )(*args)
    if first:
        o, vf, wkv = outs
    else:
        (o, wkv), vf = outs, vfirst
    sh = psh.reshape(rows // seq, seq, SHIFT_COLS)[:, seq - 1]
    return o, vf, sh, wkv


def _s5_prep_kernel(lam_re_ref, lam_im_ref, ldt_ref, bt_re_ref, bt_im_ref, ct_re_ref, ct_im_ref,
                    wz_ref, wy_ref, ap_re_ref, ap_im_ref, *, cs):
    n = TILE_STATES
    row_g = _iota((LANE, n), 0) >> 4
    lane_g = _iota((LANE, n), 1) >> 6
    emb_mask = row_g == lane_g
    bd_mask = (_iota((LANE, LANE), 0) >> 4) == (_iota((LANE, LANE), 1) >> 4)

    def rep(x):
        return jnp.concatenate([x] * GROUPS_PER_TILE, axis=0)

    def embed(x):
        return jnp.where(emb_mask, rep(x), 0.0)

    for j in range(S5_TILES):
        ls = slice(j * n, (j + 1) * n)
        lr = lam_re_ref[0, :, ls]
        li = lam_im_ref[0, :, ls]
        dt = jnp.exp(ldt_ref[0, :, ls])
        mag = jnp.exp(lr * dt)
        a_re = mag * jnp.cos(li * dt)
        a_im = mag * jnp.sin(li * dt)
        q_re = a_re - 1.0
        q_im = a_im
        den = lr * lr + li * li
        f_re = (q_re * lr + q_im * li) / den
        f_im = (q_im * lr - q_re * li) / den
        b_re = bt_re_ref[0, :, ls]
        b_im = bt_im_ref[0, :, ls]
        bb_re = f_re * b_re - f_im * b_im
        bb_im = f_re * b_im + f_im * b_re
        c_re = ct_re_ref[0, :, ls]
        c_im = ct_im_ref[0, :, ls]
        pw = [(jnp.ones_like(a_re), jnp.zeros_like(a_re))]
        for _ in range(cs):
            pr, pi = pw[-1]
            pw.append((pr * a_re - pi * a_im, pr * a_im + pi * a_re))
        e = [(pr * bb_re - pi * bb_im, pr * bb_im + pi * bb_re) for pr, pi in pw[:cs]]
        wz_ref[0, j, :, 0:n] = jnp.concatenate(
            [embed(e[cs - 1 - s][0]) for s in range(cs)], axis=0).astype(BF16)
        wz_ref[0, j, :, n:2 * n] = jnp.concatenate(
            [embed(e[cs - 1 - s][1]) for s in range(cs)], axis=0).astype(BF16)
        wc_re = jnp.concatenate(
            [embed(c_re * pw[s + 1][0] - c_im * pw[s + 1][1]) for s in range(cs)], axis=0)
        wc_im = jnp.concatenate(
            [embed(-(c_re * pw[s + 1][1] + c_im * pw[s + 1][0])) for s in range(cs)], axis=0)
        wy_ref[0, j, 0:n, :] = wc_re.T.astype(BF16)
        wy_ref[0, j, n:2 * n, :] = wc_im.T.astype(BF16)
        ce_re = embed(c_re)
        ce_im = embed(c_im)
        dt_blk = [jnp.where(bd_mask, _mm_nt_x3(rep(er), ce_re) - _mm_nt_x3(rep(ei), ce_im), 0.0)
                  for er, ei in e]
        zero = jnp.zeros((LANE, LANE), F32)
        wy_ref[0, j, 2 * n:, :] = jnp.concatenate(
            [jnp.concatenate([dt_blk[s - sp] if sp <= s else zero for s in range(cs)], axis=1)
             for sp in range(cs)], axis=0).astype(BF16)
        pr, pi = pw[cs]
        for k in range(S5_SCAN_ROWS):
            ap_re_ref[0, k:k + 1, ls] = pr
            ap_im_ref[0, k:k + 1, ls] = pi
            pr, pi = pr * pr - pi * pi, 2.0 * pr * pi


def _s5_prep(lam_re, lam_im, ldt_rep, bt_re, bt_im, ct_re, ct_im, cs):
    vec = pl.BlockSpec((1, 1, S5_LANES), lambda l: (l, 0, 0))
    mat = pl.BlockSpec((1, S5_GROUP, S5_LANES), lambda l: (l, 0, 0))
    k = cs * LANE
    wz_dims = (S5_TILES, k, 2 * TILE_STATES)
    wy_dims = (S5_TILES, 2 * TILE_STATES + k, k)
    ap_spec = pl.BlockSpec((1, S5_SCAN_ROWS, S5_LANES), lambda l: (l, 0, 0))
    ap_shape = jax.ShapeDtypeStruct((DEPTH, S5_SCAN_ROWS, S5_LANES), F32)
    return pl.pallas_call(
        functools.partial(_s5_prep_kernel, cs=cs),
        grid=(DEPTH,),
        in_specs=[vec, vec, vec, mat, mat, mat, mat],
        out_specs=[pl.BlockSpec((1,) + wz_dims, lambda l: (l, 0, 0, 0)),
                   pl.BlockSpec((1,) + wy_dims, lambda l: (l, 0, 0, 0)), ap_spec, ap_spec],
        out_shape=[jax.ShapeDtypeStruct((DEPTH,) + wz_dims, BF16),
                   jax.ShapeDtypeStruct((DEPTH,) + wy_dims, BF16), ap_shape, ap_shape],
        compiler_params=_cparams(("arbitrary",)),
        name="s5_prep",
    )(lam_re, lam_im, ldt_rep, bt_re, bt_im, ct_re, ct_im)


def _gelu(x):
    return 0.5 * x * (1.0 + lax.erf(x * (1.0 / math.sqrt(2.0))))


def _s5_kernel(*refs, cs, rows, chain):
    if chain:
        (u_ref, g_ref, wz_ref, wy_ref, ap_re_ref, ap_im_ref,
         dsk_ref, wglu_ref, o_ref, hr_out_ref, hi_out_ref, hc_r_ref, hc_i_ref) = refs

        @pl.when(pl.program_id(1) == 0)
        def _():
            hc_r_ref[...] = jnp.zeros_like(hc_r_ref)
            hc_i_ref[...] = jnp.zeros_like(hc_i_ref)
    else:
        (u_ref, g_ref, h0r_ref, h0i_ref, wz_ref, wy_ref, ap_re_ref, ap_im_ref,
         dsk_ref, wglu_ref, o_ref, hr_out_ref, hi_out_ref) = refs
    n = TILE_STATES
    k = cs * LANE
    if chain:
        steps = rows // SUBLANES

        def load_rows(ref, j, s):
            return jnp.concatenate([ref[j, pl.ds(i * cs + s, SUBLANES, stride=S5_SEG_PITCH), :]
                                    for i in range(steps)], axis=0)

        def store_rows(ref, j, s, val):
            for i in range(steps):
                ref[j, pl.ds(i * cs + s, SUBLANES, stride=S5_SEG_PITCH), :] = (
                    val[i * SUBLANES:(i + 1) * SUBLANES])

        pad = jnp.zeros((S5_SEG_PITCH - S5_SEG, LANE), F32)
        for j in range(S5_TILES):
            for g in range(SUBLANES):
                o_ref[j, g * S5_SEG_PITCH + S5_SEG:(g + 1) * S5_SEG_PITCH, :] = pad
    else:
        def load_rows(ref, j, s):
            return ref[j, pl.ds(s, rows, stride=cs), :]

        def store_rows(ref, j, s, val):
            ref[j, pl.ds(s, rows, stride=cs), :] = val

    us = [[load_rows(u_ref, j, s) for s in range(cs)] for j in range(S5_TILES)]
    x8 = [jnp.concatenate(us[j], axis=1).astype(BF16) for j in range(S5_TILES)]
    z = [jnp.dot(x8[j], wz_ref[0, j], preferred_element_type=F32) for j in range(S5_TILES)]
    hin = []
    for j in range(S5_TILES):
        ls = slice(j * n, (j + 1) * n)
        zr, zi = z[j][:, :n], z[j][:, n:]
        ar, ai = ap_re_ref[0, 0:1, ls], ap_im_ref[0, 0:1, ls]
        if chain:
            hr = jnp.zeros((SUBLANES, n), F32)
            hi = jnp.zeros((SUBLANES, n), F32)
            loc = []
            for i in range(steps):
                loc.append((hr, hi))
                rs = slice(i * SUBLANES, (i + 1) * SUBLANES)
                hr, hi = ar * hr - ai * hi + zr[rs], ar * hi + ai * hr + zi[rs]
            lvl = int(math.log2(steps))
            sr, si = ap_re_ref[0, lvl:lvl + 1, ls], ap_im_ref[0, lvl:lvl + 1, ls]
            cr, ci = hc_r_ref[:, ls], hc_i_ref[:, ls]
            seg_r, seg_i = [], []
            for g in range(SUBLANES):
                seg_r.append(cr)
                seg_i.append(ci)
                cr, ci = sr * cr - si * ci + hr[g:g + 1], sr * ci + si * cr + hi[g:g + 1]
            hc_r_ref[:, ls] = cr
            hc_i_ref[:, ls] = ci
            hr_out_ref[0, :, ls] = cr
            hi_out_ref[0, :, ls] = ci
            seg_r = jnp.concatenate(seg_r, axis=0)
            seg_i = jnp.concatenate(seg_i, axis=0)
            pr, pi = jnp.ones_like(ar), jnp.zeros_like(ar)
            hin_r, hin_i = [], []
            for i in range(steps):
                hin_r.append(loc[i][0] + pr * seg_r - pi * seg_i)
                hin_i.append(loc[i][1] + pr * seg_i + pi * seg_r)
                pr, pi = pr * ar - pi * ai, pr * ai + pi * ar
            hin_r = jnp.concatenate(hin_r, axis=0)
            hin_i = jnp.concatenate(hin_i, axis=0)
        else:
            hin_r, hin_i = h0r_ref[:, ls], h0i_ref[:, ls]
            hr_out_ref[:, ls] = ar * hin_r - ai * hin_i + zr
            hi_out_ref[:, ls] = ar * hin_i + ai * hin_r + zi
        hin.append((hin_r.astype(BF16), hin_i.astype(BF16)))
    y8 = []
    for j in range(S5_TILES):
        lhs = jnp.concatenate([hin[j][0], hin[j][1], x8[j]], axis=1)
        cols = []
        for b in range(k // 256):
            kk = 2 * n + (b + 1) * 256
            cols.append(jnp.dot(lhs[:, :kk], wy_ref[0, j, 0:kk, b * 256:(b + 1) * 256],
                                preferred_element_type=F32))
        y8.append(jnp.concatenate(cols, axis=1))
    dsk = dsk_ref[0]
    for s in range(cs):
        y = jnp.concatenate([y8[j][:, s * LANE:(s + 1) * LANE] for j in range(S5_TILES)], axis=1)
        u_s = jnp.concatenate([us[j][s] for j in range(S5_TILES)], axis=1)
        y = _gelu(y + dsk * u_s)
        y = y * _sigmoid(jnp.dot(y.astype(BF16), wglu_ref[0], preferred_element_type=F32))
        gate = jnp.concatenate([load_rows(g_ref, j, s) for j in range(S5_TILES)], axis=1)
        out = y * _silu(gate)
        for j in range(S5_TILES):
            store_rows(o_ref, j, s, out[:, j * LANE:(j + 1) * LANE])


def _s5_mix(u, g, h0, ops, dsk3, wglu_b, l, cs, nb):
    chain = h0 is None
    wz, wy, ap_re, ap_im = ops
    if chain:
        ntok = SUBLANES * S5_SEG
        assert S5_SEG % cs == 0 and (S5_SEG // cs) & (S5_SEG // cs - 1) == 0
        tiles = u.shape[1] // S5_SEG_PITCH // SUBLANES // nb
        tok = pl.BlockSpec((S5_TILES, SUBLANES * S5_SEG_PITCH, LANE), lambda b, t: (0, b * tiles + t, 0))
    else:
        ntok = u.shape[1]
        tiles = 1
        tok = pl.BlockSpec((S5_TILES, ntok, LANE), lambda b, t: (0, 0, 0))
    rows = ntok // cs
    ap_spec = pl.BlockSpec((1, S5_SCAN_ROWS, S5_LANES), lambda b, t: (l, 0, 0))
    in_specs = [tok, tok]
    args = [u, g]
    scratch = []
    if not chain:
        hs = pl.BlockSpec((rows, S5_LANES), lambda b, t: (0, 0))
        in_specs += [hs, hs]
        args += [h0[0], h0[1]]
        st_spec = hs
        st_shape = jax.ShapeDtypeStruct((rows, S5_LANES), F32)
    else:
        st_spec = pl.BlockSpec((1, 1, S5_LANES), lambda b, t: (b, 0, 0))
        st_shape = jax.ShapeDtypeStruct((nb, 1, S5_LANES), F32)
        scratch = [pltpu.VMEM((1, S5_LANES), F32), pltpu.VMEM((1, S5_LANES), F32)]
    in_specs += [
        pl.BlockSpec((1,) + wz.shape[1:], lambda b, t: (l, 0, 0, 0)),
        pl.BlockSpec((1,) + wy.shape[1:], lambda b, t: (l, 0, 0, 0)),
        ap_spec, ap_spec,
        pl.BlockSpec((1, 1, S5_WIDTH), lambda b, t: (l, 0, 0)),
        pl.BlockSpec((1, S5_WIDTH, S5_WIDTH), lambda b, t: (l, 0, 0)),
    ]
    args += [wz, wy, ap_re, ap_im, dsk3, wglu_b]
    o, hr, hi = pl.pallas_call(
        functools.partial(_s5_kernel, cs=cs, rows=rows, chain=chain),
        grid=(nb, tiles),
        in_specs=in_specs,
        out_specs=[tok, st_spec, st_spec],
        out_shape=[jax.ShapeDtypeStruct(u.shape, F32), st_shape, st_shape],
        scratch_shapes=scratch,
        compiler_params=_cparams(("arbitrary", "arbitrary")),
        name="s5_mix",
    )(*args)
    return o, hr, hi


def _softmax(s):
    e = jnp.exp(s - jnp.max(s, axis=-1, keepdims=True))
    return (e / jnp.sum(e, axis=-1, keepdims=True)).astype(BF16)


def _attend(q, mk_ref, mv_ref):
    heads = [slice(h * X_HEAD_DIM, (h + 1) * X_HEAD_DIM) for h in range(X_HEADS)]
    s = [_mm_nt(q[:, hs], mk_ref[0, 0, :, hs]) for hs in heads]
    p = [_softmax(sh) for sh in s]
    return jnp.concatenate([_mm(ph, mv_ref[0, 0, :, hs]) for ph, hs in zip(p, heads)], axis=-1)


def _attend_tiled(q, mk_ref, mv_ref, nseg, row_seg):
    def head(ref, i, h):
        return jnp.concatenate([ref[0, i, pl.ds(dt * X_HEADS + h, MEM_LEN, stride=_MEM_SUB), :]
                                for dt in range(_DT)], axis=1)

    def pick(vals):
        out = vals[0]
        for i in range(1, nseg):
            out = jnp.where(row_seg == i, vals[i], out)
        return out

    qb = q.astype(BF16)
    heads = [slice(h * X_HEAD_DIM, (h + 1) * X_HEAD_DIM) for h in range(X_HEADS)]
    s = [[_mm_nt(qb[:, hs], head(mk_ref, i, h)) for i in range(nseg)] for h, hs in enumerate(heads)]
    p = [_softmax(pick(sh)) for sh in s]
    outs = [pick([_mm(p[h], head(mv_ref, i, h)) for i in range(nseg)]) for h in range(X_HEADS)]
    return jnp.concatenate(outs, axis=-1)


def _post_kernel(x_ref, orw_ref, os5_ref, wout_ref, nx_ref, wq_ref, wo_ref, mk_ref, mv_ref, nf_ref,
                 o_ref, *, nseg, seg_rows, final):
    x = x_ref[...]
    if nseg == 1:
        nq = x.shape[0] // S5_SEG
        os5 = jnp.concatenate(
            [jnp.concatenate([os5_ref[j, q * S5_SEG_PITCH:q * S5_SEG_PITCH + S5_SEG, :] for q in range(nq)], axis=0)
             for j in range(S5_TILES)], axis=1)
    else:
        os5 = jnp.concatenate([os5_ref[j] for j in range(S5_TILES)], axis=1)
    x1 = (x + jnp.dot(orw_ref[...].astype(BF16), wout_ref[0, 0:RWKV_WIDTH, :], preferred_element_type=F32)
          + jnp.dot(os5.astype(BF16), wout_ref[0, RWKV_WIDTH:, :], preferred_element_type=F32))
    xc = _rms(x1, nx_ref[0]).astype(BF16)
    q = jnp.dot(xc, wq_ref[0], preferred_element_type=F32) * (1.0 / math.sqrt(X_HEAD_DIM))
    if nseg == 1:
        att = _attend(q, mk_ref, mv_ref)
    else:
        row_seg = _iota((q.shape[0], 1), 0) >> int(math.log2(seg_rows))
        att = _attend_tiled(q, mk_ref, mv_ref, nseg, row_seg)
    x2 = x1 + jnp.dot(att.astype(BF16), wo_ref[0], preferred_element_type=F32)
    if final:
        x2 = _rms(x2, nf_ref[...])
    o_ref[...] = x2


def _post(x2d, orw, os5, wout_b, nx3, wq_b, wo_b, mk, mv, nf2, l, nb, nseg, seg_rows, tq, final):
    tiles = x2d.shape[0] // nb // tq
    tok = lambda w: pl.BlockSpec((tq, w), lambda b, t: (b * tiles + t, 0))
    s5_rows = tq // S5_SEG * S5_SEG_PITCH if nseg == 1 else tq
    tok_s5 = pl.BlockSpec((S5_TILES, s5_rows, LANE), lambda b, t: (0, b * tiles + t, 0))
    if nseg == 1:
        mem = pl.BlockSpec((1, 1, MEM_LEN, D_MODEL), lambda b, t: (l, b, 0, 0))
    else:
        mem = pl.BlockSpec((1, nseg, MEM_LEN * _MEM_SUB, LANE), lambda b, t: (l, t, 0, 0))
    wsp = pl.BlockSpec((1, D_MODEL, D_MODEL), lambda b, t: (l, 0, 0))
    return pl.pallas_call(
        functools.partial(_post_kernel, nseg=nseg, seg_rows=seg_rows, final=final),
        grid=(nb, tiles),
        in_specs=[tok(D_MODEL), tok(RWKV_WIDTH), tok_s5, wsp,
                  pl.BlockSpec((1, 1, D_MODEL), lambda b, t: (l, 0, 0)), wsp, wsp, mem, mem,
                  pl.BlockSpec((1, D_MODEL), lambda b, t: (0, 0))],
        out_specs=tok(D_MODEL),
        out_shape=jax.ShapeDtypeStruct(x2d.shape, F32),
        compiler_params=_cparams(("arbitrary", "arbitrary")),
        name="post",
    )(x2d, orw, os5, wout_b, nx3, wq_b, wo_b, mk, mv, nf2)


def _trunk(x, states, mem_k, mem_v, W):
    batch, seq, _ = x.shape
    prompt = states is None
    x2d = x.reshape(batch * seq, D_MODEL)
    new_shift, new_wkv, new_re, new_im = [], [], [], []
    vfirst = None
    wkv_all = None if prompt else states[1]
    for l in range(DEPTH):
        rw_args = (W['mu'], W['vecs'], W['wla'], W['v1'], W['v2'], l)
        if prompt:
            orw, vfirst, us5, gs5, sh, wkv = _rwkv_prompt(x2d, W['norm_mix'], W['w_in'], vfirst, *rw_args,
                                                          batch, seq)
            os5, hr, hi = _s5_mix(us5, gs5, None, W['s5_prompt'], W['d_skip'], W['w_glu'], l, 8, batch)
            hr = hr.reshape(batch, S5_GROUPS, S5_STATE)
            hi = hi.reshape(batch, S5_GROUPS, S5_STATE)
            x2d = _post(x2d, orw, os5, W['w_out'], W['norm_x'], W['wq'], W['wo'], mem_k, mem_v,
                        W['norm_f'], l, batch, 1, seq, 1024, l == DEPTH - 1)
        else:
            shift0, _, re0, im0 = states
            psh, grw, us5, gs5 = _inproj(x2d, W['norm_mix'], W['w_in'], l)
            prev_rows = jnp.repeat(shift0[l], seq, axis=0)
            orw, vfirst, sh, wkv_all = _rwkv_sample(psh, grw, vfirst, prev_rows, wkv_all, *rw_args, seq)
            h0 = (re0[l].reshape(batch, S5_LANES), im0[l].reshape(batch, S5_LANES))
            os5, hr, hi = _s5_mix(us5, gs5, h0, W['s5_sample'], W['d_skip'], W['w_glu'], l, seq, 1)
            hr = hr.reshape(batch, S5_GROUPS, S5_STATE)
            hi = hi.reshape(batch, S5_GROUPS, S5_STATE)
            nseg = 32 // seq
            x2d = _post(x2d, orw, os5, W['w_out'], W['norm_x'], W['wq'], W['wo'], mem_k, mem_v,
                        W['norm_f'], l, 1, nseg, seq, nseg * seq, l == DEPTH - 1)
        new_shift.append(sh)
        if prompt:
            new_wkv.append(wkv)
        new_re.append(hr)
        new_im.append(hi)
    y = x2d.reshape(batch, seq, D_MODEL)
    wkv_out = jnp.stack(new_wkv) if prompt else wkv_all
    return y, jnp.stack(new_shift), wkv_out, jnp.stack(new_re), jnp.stack(new_im)


def kernel(x_prompt, x_sample, state_shift, state_wkv, state_s5_re, state_s5_im, cache_mem_k, cache_mem_v, mem_prompt, norm_mix, w_in, mu_shift, w0, w2, a0, a2, v0, v1, v2, k_k, k_a, r_k, gn_w, gn_b, lam_re, lam_im, log_dt, b_re, b_im, c_re, c_im, d_skip, w_glu, w_out, norm_x, norm_mem, wq, wk, wv, wo, norm_f):
    L = DEPTH
    bp, mp = mem_prompt.shape[0], mem_prompt.shape[1]
    zpad = jnp.zeros((1, RWKV_WIDTH), F32)
    vecs = jnp.stack([w0, a0, k_k, k_a, gn_w, gn_b, r_k.reshape(L, RWKV_WIDTH),
                      jnp.concatenate([zpad, v0], axis=0)], axis=1)
    zl = jnp.zeros((L, LORA_W, RWKV_WIDTH), F32)
    wla = jnp.concatenate([jnp.concatenate([w2, zl], axis=2),
                           jnp.concatenate([zl, a2], axis=2)], axis=1).astype(BF16)
    tr = lambda t: jnp.transpose(t, (0, 3, 1, 2)).reshape(L, S5_GROUP, S5_LANES)
    s5_in = (lam_re.reshape(L, 1, S5_LANES), lam_im.reshape(L, 1, S5_LANES),
             jnp.repeat(log_dt, S5_STATE, axis=1).reshape(L, 1, S5_LANES),
             tr(b_re), tr(b_im),
             jnp.transpose(c_re, (0, 2, 1, 3)).reshape(L, S5_GROUP, S5_LANES),
             jnp.transpose(c_im, (0, 2, 1, 3)).reshape(L, S5_GROUP, S5_LANES))
    W = dict(
        norm_mix=norm_mix.reshape(L, 1, D_MODEL), w_in=w_in.astype(BF16),
        mu=mu_shift.reshape(L, 1, SHIFT_COLS), vecs=vecs, wla=wla,
        v1=v1.astype(BF16), v2=v2.astype(BF16),
        s5_prompt=_s5_prep(*s5_in, 8), s5_sample=_s5_prep(*s5_in, x_sample.shape[1]),
        d_skip=d_skip.reshape(L, 1, S5_WIDTH), w_glu=w_glu.astype(BF16),
        w_out=w_out.astype(BF16), norm_x=norm_x.reshape(L, 1, D_MODEL),
        wq=wq.astype(BF16), wo=wo.astype(BF16), norm_f=norm_f.reshape(1, D_MODEL),
    )
    pk, pv, pk_t, pv_t = _memkv(mem_prompt.reshape(bp * mp, D_MODEL), norm_mem,
                                wk.astype(BF16), wv.astype(BF16))
    pk = pk.reshape(L, bp, mp, D_MODEL)
    pv = pv.reshape(L, bp, mp, D_MODEL)
    y_prompt, p_shift, p_wkv, p_re, p_im = _trunk(x_prompt, None, pk, pv, W)
    tiled = lambda c: jnp.transpose(c.reshape(L, -1, MEM_LEN, X_HEADS, _DT, LANE),
                                    (0, 1, 2, 4, 3, 5)).reshape(L, -1, MEM_LEN * _MEM_SUB, LANE)
    untiled = lambda c: jnp.transpose(c.reshape(L, -1, MEM_LEN, _DT, X_HEADS, LANE),
                                      (0, 1, 2, 4, 3, 5)).reshape(L, -1, MEM_LEN, X_HEADS, X_HEAD_DIM)
    y_sample, s_shift, s_wkv, s_re, s_im = _trunk(
        x_sample, (state_shift, state_wkv, state_s5_re, state_s5_im), tiled(cache_mem_k),
        tiled(cache_mem_v), W)
    return (y_prompt, y_sample, p_shift, p_wkv, p_re, p_im, untiled(pk_t), untiled(pv_t),
            s_shift, s_wkv, s_re, s_im)
```

```python
import functools
import math

import jax
import jax.numpy as jnp
from jax import lax
from jax.experimental import pallas as pl
from jax.experimental.pallas import tpu as pltpu

F32 = jnp.float32
BF16 = jnp.bfloat16

D_MODEL = 1024
DEPTH = 4
RWKV_WIDTH = 512
RWKV_HEAD = 64
RWKV_HEADS = 8
LORA_W = 64
LORA_A = 64
LORA_V = 32
S5_WIDTH = 512
S5_GROUP = 16
S5_GROUPS = 32
S5_STATE = 64
S5_LANES = S5_GROUPS * S5_STATE
MEM_LEN = 256
X_HEADS = 4
X_HEAD_DIM = 256
SHIFT_COLS = 3 * RWKV_WIDTH + LORA_W + LORA_A
IN_COLS = SHIFT_COLS + RWKV_WIDTH + 2 * S5_WIDTH
NORM_EPS = 1e-6
GN_EPS = 64e-5

LANE = 128
SUBLANES = 8
WKV_CHUNK = 64
HALF_HEADS = 4
WKV_UNROLL = 8
S5_TILES = S5_WIDTH // LANE
GROUPS_PER_TILE = LANE // S5_GROUP
TILE_STATES = GROUPS_PER_TILE * S5_STATE
S5_CHUNK = 8
S5_SCAN_ROWS = 8
S5_SEG = 128
S5_SEG_PITCH = S5_SEG + SUBLANES
VMEM_LIMIT = 56 * 1024 * 1024


def _cparams(sem):
    return pltpu.CompilerParams(dimension_semantics=sem, vmem_limit_bytes=VMEM_LIMIT)


def _mm(a, b):
    return jnp.dot(a.astype(BF16), b.astype(BF16), preferred_element_type=F32)


def _mm_nt(a, b):
    return lax.dot_general(a.astype(BF16), b.astype(BF16), (((1,), (1,)), ((), ())),
                           preferred_element_type=F32)


def _mm_tn(a, b):
    return lax.dot_general(a.astype(BF16), b.astype(BF16), (((0,), (0,)), ((), ())),
                           preferred_element_type=F32)


def _mm_f32(a, b):
    return jnp.dot(a, b, preferred_element_type=F32, precision=lax.Precision.HIGHEST)


def _mm_nt_x3(a, b):
    a_hi, b_hi = a.astype(BF16), b.astype(BF16)
    a_lo = (a - a_hi.astype(F32)).astype(BF16)
    b_lo = (b - b_hi.astype(F32)).astype(BF16)
    return _mm_nt(a_hi, b_hi) + _mm_nt(a_hi, b_lo) + _mm_nt(a_lo, b_hi)


def _rms(x, g):
    return x * lax.rsqrt(jnp.mean(x * x, axis=-1, keepdims=True) + NORM_EPS) * g


def _sigmoid(x):
    return 1.0 / (1.0 + jnp.exp(-x))


def _silu(x):
    return x * _sigmoid(x)


def _iota(shape, dim):
    return lax.broadcasted_iota(jnp.int32, shape, dim)


_DT = X_HEAD_DIM // LANE
_MEM_SUB = X_HEADS * _DT


def _memkv_kernel(m_ref, g_ref, wk_ref, wv_ref, k_ref, v_ref, kt_ref, vt_ref):
    mn = _rms(m_ref[...], g_ref[0]).astype(BF16)
    tm = m_ref.shape[0]
    for w_ref, o_ref, t_ref in ((wk_ref, k_ref, kt_ref), (wv_ref, v_ref, vt_ref)):
        kv = jnp.dot(mn, w_ref[0], preferred_element_type=F32)
        o_ref[0] = kv
        for h in range(X_HEADS):
            for dt in range(_DT):
                lo = h * X_HEAD_DIM + dt * LANE
                t_ref[0, pl.ds(dt * X_HEADS + h, tm, stride=_MEM_SUB), :] = kv[:, lo:lo + LANE]


def _memkv(mem2d, norm_mem, wk_b, wv_b):
    rows = mem2d.shape[0]
    tm = 512
    out = jax.ShapeDtypeStruct((DEPTH, rows, D_MODEL), F32)
    out_t = jax.ShapeDtypeStruct((DEPTH, rows * _MEM_SUB, LANE), F32)
    nat = pl.BlockSpec((1, tm, D_MODEL), lambda l, i: (l, i, 0))
    til = pl.BlockSpec((1, tm * _MEM_SUB, LANE), lambda l, i: (l, i, 0))
    return pl.pallas_call(
        _memkv_kernel,
        grid=(DEPTH, rows // tm),
        in_specs=[
            pl.BlockSpec((tm, D_MODEL), lambda l, i: (i, 0)),
            pl.BlockSpec((1, 1, D_MODEL), lambda l, i: (l, 0, 0)),
            pl.BlockSpec((1, D_MODEL, D_MODEL), lambda l, i: (l, 0, 0)),
            pl.BlockSpec((1, D_MODEL, D_MODEL), lambda l, i: (l, 0, 0)),
        ],
        out_specs=[nat, nat, til, til],
        out_shape=[out, out, out_t, out_t],
        compiler_params=_cparams(("arbitrary", "arbitrary")),
        name="memkv",
    )(mem2d, norm_mem.reshape(DEPTH, 1, D_MODEL), wk_b, wv_b)


_IN_SPLITS = (0, SHIFT_COLS, SHIFT_COLS + RWKV_WIDTH, SHIFT_COLS + RWKV_WIDTH + S5_WIDTH, IN_COLS)


def _inproj_kernel(x_ref, g_ref, w_ref, psh_ref, grw_ref, us5_ref, gs5_ref):
    xn = _rms(x_ref[...], g_ref[0]).astype(BF16)
    proj = lambda lo, hi: jnp.dot(xn, w_ref[0, :, lo:hi], preferred_element_type=F32)
    psh_ref[...] = proj(_IN_SPLITS[0], _IN_SPLITS[1])
    grw_ref[...] = proj(_IN_SPLITS[1], _IN_SPLITS[2])
    for o_ref, lo in ((us5_ref, _IN_SPLITS[2]), (gs5_ref, _IN_SPLITS[3])):
        cols = proj(lo, lo + S5_WIDTH)
        for j in range(S5_TILES):
            o_ref[j] = cols[:, j * LANE:(j + 1) * LANE]


def _inproj(x2d, norm_mix3, w_in_b, l):
    rows = x2d.shape[0]
    tm = 256
    s5_spec = pl.BlockSpec((S5_TILES, tm, LANE), lambda i: (0, i, 0))
    s5_shape = jax.ShapeDtypeStruct((S5_TILES, rows, LANE), F32)
    return pl.pallas_call(
        _inproj_kernel,
        grid=(rows // tm,),
        in_specs=[
            pl.BlockSpec((tm, D_MODEL), lambda i: (i, 0)),
            pl.BlockSpec((1, 1, D_MODEL), lambda i: (l, 0, 0)),
            pl.BlockSpec((1, D_MODEL, IN_COLS), lambda i: (l, 0, 0)),
        ],
        out_specs=[pl.BlockSpec((tm, SHIFT_COLS), lambda i: (i, 0)),
                   pl.BlockSpec((tm, RWKV_WIDTH), lambda i: (i, 0)), s5_spec, s5_spec],
        out_shape=[jax.ShapeDtypeStruct((rows, SHIFT_COLS), F32),
                   jax.ShapeDtypeStruct((rows, RWKV_WIDTH), F32), s5_shape, s5_shape],
        compiler_params=_cparams(("arbitrary",)),
        name="inproj",
    )(x2d, norm_mix3, w_in_b)


_V_W0, _V_A0, _V_KK, _V_KA, _V_GNW, _V_GNB, _V_RK, _V_V0 = range(8)


def _rwkv_prep(ps, vfirst, vecs, wla, v1, v2):
    r = ps[:, 0:RWKV_WIDTH]
    k = ps[:, RWKV_WIDTH:2 * RWKV_WIDTH]
    v = ps[:, 2 * RWKV_WIDTH:3 * RWKV_WIDTH]
    lora_in = ps[:, 3 * RWKV_WIDTH:SHIFT_COLS]
    lane = _iota(lora_in.shape, 1)
    lora_in = jnp.where(lane < LORA_W, jnp.tanh(lora_in), lora_in)
    lora = _mm(lora_in, wla)
    wpre = vecs[_V_W0:_V_W0 + 1] + lora[:, 0:RWKV_WIDTH]
    z = -wpre
    softplus = jnp.maximum(z, 0.0) + jnp.log(1.0 + jnp.exp(-jnp.abs(z)))
    logdecay = -jnp.exp(-softplus - 0.5)
    a = _sigmoid(vecs[_V_A0:_V_A0 + 1] + lora[:, RWKV_WIDTH:2 * RWKV_WIDTH])
    if vfirst is not None:
        gate = _sigmoid(vecs[_V_V0:_V_V0 + 1] + _mm(_mm(v, v1), v2))
        v = v + (vfirst - v) * gate
    kku = k * vecs[_V_KK:_V_KK + 1]
    kmod = k * (1.0 + (a - 1.0) * vecs[_V_KA:_V_KA + 1])
    return r, kku, kmod, v, a, logdecay


def _wkv_chunk(lw, load_head, states, seg_len, vecs):
    C = WKV_CHUNK
    shift = int(math.log2(seg_len))
    per_grp = SUBLANES // seg_len
    ngrp = C // SUBLANES
    row = _iota((C, C), 0)
    col = _iota((C, C), 1)
    same = (row >> shift) == (col >> shift)
    incl = jnp.logical_and(col <= row, same)
    strict = jnp.logical_and(col < row, same)
    cum = _mm_f32(incl.astype(F32), lw)
    tot = _mm_f32(same.astype(F32), lw)
    g_inc_all = jnp.exp(cum)
    g_exc_all = jnp.exp(cum - lw)
    g_inv_all = jnp.exp(-cum)
    g_end_all = jnp.exp(tot - cum)
    g_tot_all = jnp.exp(tot)
    eye = (row == col).astype(F32)
    nsq = max(shift - 1, 0)
    seg16 = (_iota((2 * SUBLANES, 1), 0) & (SUBLANES - 1)) >> shift

    hd = []
    for h in range(RWKV_HEADS):
        sl = slice(h * RWKV_HEAD, (h + 1) * RWKV_HEAD)
        r, kku, kmod, v, a = load_head(h)
        ss = jnp.sum(kku * kku, axis=-1, keepdims=True)
        kkn = kku * lax.rsqrt(jnp.maximum(ss, 1e-24))
        b = kkn * a
        qk = kkn * g_exc_all[:, sl]
        qr = r * g_inc_all[:, sl]
        hd.append(dict(sl=sl, r=r, kmod=kmod, v=v, qk=qk, qr=qr,
                       q2=jnp.concatenate([qk, qr], axis=0).astype(BF16),
                       kd=kmod * g_inv_all[:, sl], bd=b * g_inv_all[:, sl],
                       kc=kmod * g_end_all[:, sl], bc=b * g_end_all[:, sl]))
    for d in hd:
        d['ak'] = _mm_nt(d['q2'], d['kd'])
        d['ab'] = _mm_nt(d['q2'], d['bd'])
    for d in hd:
        d['akk'] = jnp.where(strict, d['ak'][:C], 0.0)
        d['ark'] = jnp.where(incl, d['ak'][C:], 0.0)
        d['arb'] = jnp.where(incl, d['ab'][C:], 0.0)
        d['n'] = -jnp.where(strict, d['ab'][:C], 0.0)
        d['tinv'] = eye + d['n']
    for _ in range(nsq):
        for d in hd:
            d['n'] = _mm(d['n'], d['n'])
        for d in hd:
            d['tinv'] = d['tinv'] + _mm(d['tinv'], d['n'])
    for d in hd:
        d['av'] = _mm(jnp.concatenate([d['akk'], d['ark']], axis=0), d['v'])
    for d in hd:
        d['u_ind'] = _mm(d['tinv'], d['av'][:C])
        d['tq'] = _mm(d['tinv'], d['qk'])
    for h, d in enumerate(hd):
        us, yq = [], []
        for g in range(ngrp):
            rs = slice(g * SUBLANES, (g + 1) * SUBLANES)
            lhs = jnp.concatenate([d['tq'][rs], d['qr'][rs]], axis=0).astype(BF16)
            qs = None
            for s in range(per_grp):
                o = _mm_nt(lhs, states[h][g * per_grp + s])
                qs = o if qs is None else jnp.where(seg16 == s, o, qs)
            us.append(d['u_ind'][rs] + qs[:SUBLANES])
            yq.append(qs[SUBLANES:])
        d['u'] = jnp.concatenate(us, axis=0)
        d['yq'] = jnp.concatenate(yq, axis=0)
    for d in hd:
        d['y'] = d['yq'] + d['av'][C:] - _mm(d['arb'], d['u'])
    new_states = []
    for h, d in enumerate(hd):
        hs = []
        for g in range(ngrp):
            rs = slice(g * SUBLANES, (g + 1) * SUBLANES)
            vu = jnp.concatenate([d['v'][rs], d['u'][rs]], axis=0)
            kb = jnp.concatenate([d['kc'][rs], -d['bc'][rs]], axis=0).astype(BF16)
            for s in range(per_grp):
                i = g * per_grp + s
                gt = g_tot_all[i * seg_len:i * seg_len + 1, d['sl']]
                hs.append(states[h][i] * gt + _mm_tn(jnp.where(seg16 == s, vu, 0.0), kb))
        new_states.append(hs)
    ys = []
    for d in hd:
        y, sl = d['y'], d['sl']
        mu = jnp.mean(y, axis=-1, keepdims=True)
        yc = y - mu
        var = jnp.mean(yc * yc, axis=-1, keepdims=True)
        yn = yc * lax.rsqrt(var + GN_EPS) * vecs[_V_GNW:_V_GNW + 1, sl] + vecs[_V_GNB:_V_GNB + 1, sl]
        bonus = jnp.sum(d['r'] * d['kmod'] * vecs[_V_RK:_V_RK + 1, sl], axis=-1, keepdims=True)
        ys.append(yn + bonus * d['v'])
    return jnp.concatenate(ys, axis=-1), new_states


def _rwkv_prompt_kernel(*refs, first_layer, tt):
    if first_layer:
        (x_ref, nm_ref, win_ref, mu_ref, vecs_ref, wla_ref, v1_ref, v2_ref,
         o_ref, vf_out_ref, us5_ref, gs5_ref, shift_out_ref, wkv_out_ref,
         carry_ref, s_ref, r_s, kkn_s, kmod_s, v_s, b_s, lw_s, y_s) = refs
        vf_ref = None
    else:
        (x_ref, nm_ref, win_ref, vf_ref, mu_ref, vecs_ref, wla_ref, v1_ref, v2_ref,
         o_ref, us5_ref, gs5_ref, shift_out_ref, wkv_out_ref,
         carry_ref, s_ref, r_s, kkn_s, kmod_s, v_s, b_s, lw_s, y_s) = refs
        vf_out_ref = None
    t = pl.program_id(1)

    @pl.when(t == 0)
    def _():
        carry_ref[...] = jnp.zeros_like(carry_ref)
        s_ref[...] = jnp.zeros_like(s_ref)

    xn = _rms(x_ref[0], nm_ref[0]).astype(BF16)
    proj = lambda lo, hi: jnp.dot(xn, win_ref[0, :, lo:hi], preferred_element_type=F32)
    p = proj(_IN_SPLITS[0], _IN_SPLITS[1])
    o_ref[0] = proj(_IN_SPLITS[1], _IN_SPLITS[2])
    pad = jnp.zeros((S5_SEG_PITCH - S5_SEG, LANE), F32)
    for s5_ref, lo in ((us5_ref, _IN_SPLITS[2]), (gs5_ref, _IN_SPLITS[3])):
        cols = proj(lo, lo + S5_WIDTH)
        for j in range(S5_TILES):
            for q in range(tt // S5_SEG):
                s5_ref[j, q * S5_SEG_PITCH:q * S5_SEG_PITCH + S5_SEG, :] = (
                    cols[q * S5_SEG:(q + 1) * S5_SEG, j * LANE:(j + 1) * LANE])
                s5_ref[j, q * S5_SEG_PITCH + S5_SEG:(q + 1) * S5_SEG_PITCH, :] = pad
    rowi = _iota(p.shape, 0)
    prev = jnp.where(rowi == 0, carry_ref[...], pltpu.roll(p, 1, 0))
    ps = p + (prev - p) * mu_ref[0]
    last = p[tt - 1:tt, :]
    carry_ref[...] = last
    shift_out_ref[0] = last

    vecs = vecs_ref[0]
    vfirst = None if first_layer else vf_ref[0]
    r, kku, kmod, v, a, lw = _rwkv_prep(ps, vfirst, vecs, wla_ref[0], v1_ref[0], v2_ref[0])
    if first_layer:
        vf_out_ref[0] = v

    hw = HALF_HEADS * RWKV_HEAD
    bd_bool = (_iota((hw, hw), 0) >> 6) == (_iota((hw, hw), 1) >> 6)
    bd_ones = bd_bool.astype(F32).astype(BF16)

    def head_sums(x):
        n = x.shape[0]
        hi = x.astype(BF16)
        lo = (x - hi.astype(F32)).astype(BF16)
        st = jnp.concatenate([hi, lo], axis=0)
        outs = []
        for hf in range(2):
            s2 = jnp.dot(st[:, hf * hw:(hf + 1) * hw], bd_ones, preferred_element_type=F32)
            outs.append(s2[:n] + s2[n:])
        return jnp.concatenate(outs, axis=1)

    kkn = kku * lax.rsqrt(jnp.maximum(head_sums(kku * kku), 1e-24))
    r_s[...] = r
    kkn_s[...] = kkn
    kmod_s[...] = kmod
    v_s[...] = v
    b_s[...] = kkn * a
    lw_s[...] = lw

    C = WKV_CHUNK
    trow = _iota((C, hw), 0)
    tcol = _iota((C, hw), 1) & (RWKV_HEAD - 1)
    strict = tcol < trow
    incl = tcol <= trow
    eye = (tcol == trow).astype(F32)
    crow = _iota((C, RWKV_WIDTH), 0)

    def bd(x):
        return jnp.concatenate([x.astype(BF16)] * HALF_HEADS, axis=0) * bd_ones

    def group(g, carry):
        chains = []
        for k in range(WKV_UNROLL):
            rows = pl.ds(pl.multiple_of((g * WKV_UNROLL + k) * C, C), C)
            lwc = lw_s[rows, :]
            cum = lwc
            for d in (1, 2, 4):
                cum = cum + jnp.where(crow >= d, pltpu.roll(cum, d, 0), 0.0)
            for d in (8, 16, 32):
                cum = jnp.concatenate([cum[:d], cum[d:] + cum[:C - d]], axis=0)
            tot = cum[C - 1:C, :]
            g_inc = jnp.exp(cum)
            g_exc = jnp.exp(cum - lwc)
            g_inv = jnp.exp(-cum)
            g_end = jnp.exp(tot - cum)
            g_tot = jnp.exp(tot)
            rr, kk, km, bb, vv = (r_s[rows, :], kkn_s[rows, :], kmod_s[rows, :], b_s[rows, :],
                                  v_s[rows, :])
            qk_a, qr_a = kk * g_exc, rr * g_inc
            kd_a, bd_a = km * g_inv, bb * g_inv
            kc_a, bc_a = km * g_end, bb * g_end
            for hf in range(2):
                ls = slice(hf * hw, (hf + 1) * hw)
                chains.append(dict(k=k, hf=hf, rows=rows, qk=qk_a[:, ls], qr=qr_a[:, ls], kd=kd_a[:, ls],
                                   bd=bd_a[:, ls], kc=kc_a[:, ls], bc=bc_a[:, ls], v=vv[:, ls],
                                   g_tot=g_tot[:, ls]))
        for ch in chains:
            ch['q2'] = jnp.concatenate([ch['qk'], ch['qr']], axis=0).astype(BF16)
            ch['ak'] = _mm_nt(ch['q2'], bd(ch['kd']))
        for ch in chains:
            ab = _mm_nt(ch['q2'], bd(ch['bd']))
            ch['akk'] = jnp.where(strict, ch['ak'][:C], 0.0)
            ch['ark'] = jnp.where(incl, ch['ak'][C:], 0.0)
            ch['arb'] = jnp.where(incl, ab[C:], 0.0)
            ch['n'] = -jnp.where(strict, ab[:C], 0.0)
            ch['tinv'] = eye + ch['n']
        for ch in chains:
            ch['npow'] = _mm(ch['n'], bd(ch['n']))
        for _ in range(4):
            for ch in chains:
                st = _mm(jnp.concatenate([ch['tinv'], ch['npow']], axis=0), bd(ch['npow']))
                ch['tinv'] = ch['tinv'] + st[:C]
                ch['npow'] = st[C:]
        for ch in chains:
            ch['av'] = _mm(jnp.concatenate([ch['akk'], ch['ark']], axis=0), bd(ch['v']))
        for ch in chains:
            ch['tinv'] = ch['tinv'] + _mm(ch['tinv'], bd(ch['npow']))
        for ch in chains:
            ch['u_ind'] = _mm(ch['tinv'], bd(ch['av'][:C]))
            ch['tq'] = _mm(ch['tinv'], bd(ch['qk']))
        for ch in chains:
            ch['gc'] = jnp.where(bd_bool, _mm_tn(ch['tq'], ch['bc']), 0.0).astype(BF16)
            ch['nc'] = jnp.where(bd_bool, _mm_tn(jnp.concatenate([ch['v'], ch['u_ind']], axis=0),
                                                 jnp.concatenate([ch['kc'], -ch['bc']], axis=0)), 0.0)
        state = [s_ref[0], s_ref[1]]
        for ch in chains:
            s_old = state[ch['hf']]
            ch['s_b'] = s_old.astype(BF16)
            state[ch['hf']] = s_old * ch['g_tot'] - _mm(ch['s_b'], ch['gc']) + ch['nc']
        s_ref[0] = state[0]
        s_ref[1] = state[1]
        for ch in chains:
            ch['qs'] = _mm_nt(jnp.concatenate([ch['tq'], ch['qr']], axis=0), ch['s_b'])
        for ch in chains:
            u = ch['u_ind'] + ch['qs'][:C]
            ch['y'] = ch['qs'][C:] + ch['av'][C:] - _mm(ch['arb'], bd(u))
        for k in range(WKV_UNROLL):
            y_s[chains[2 * k]['rows'], :] = jnp.concatenate([chains[2 * k]['y'], chains[2 * k + 1]['y']], axis=1)
        return carry

    lax.fori_loop(0, tt // (C * WKV_UNROLL), group, 0)

    y = y_s[...]
    inv_n = 1.0 / RWKV_HEAD
    mu = head_sums(y) * inv_n
    yc = y - mu
    var = head_sums(yc * yc) * inv_n
    yn = yc * lax.rsqrt(var + GN_EPS) * vecs[_V_GNW:_V_GNW + 1] + vecs[_V_GNB:_V_GNB + 1]
    rr, km, vv = r_s[...], kmod_s[...], v_s[...]
    bonus = head_sums(rr * km * vecs[_V_RK:_V_RK + 1])
    o_ref[0] = (yn + bonus * vv) * _silu(o_ref[0])

    @pl.when(t == pl.num_programs(1) - 1)
    def _():
        for h in range(RWKV_HEADS):
            o = (h % HALF_HEADS) * RWKV_HEAD
            wkv_out_ref[0, h] = s_ref[h // HALF_HEADS, o:o + RWKV_HEAD, o:o + RWKV_HEAD]


def _rwkv_prompt(x2d, norm_mix3, w_in_b, vfirst, mu3, vecs, wla, v1, v2, l, batch, seq):
    tt = 512
    tiles = seq // tt
    first = vfirst is None
    lv = max(l - 1, 0)
    tok = lambda w: pl.BlockSpec((1, tt, w), lambda b, t: (b, t, 0))
    s5_spec = pl.BlockSpec((S5_TILES, tt // S5_SEG * S5_SEG_PITCH, LANE), lambda b, t: (0, b * tiles + t, 0))
    s5_shape = jax.ShapeDtypeStruct((S5_TILES, batch * seq // S5_SEG * S5_SEG_PITCH, LANE), F32)
    in_specs = [tok(D_MODEL),
                pl.BlockSpec((1, 1, D_MODEL), lambda b, t: (l, 0, 0)),
                pl.BlockSpec((1, D_MODEL, IN_COLS), lambda b, t: (l, 0, 0))]
    args = [x2d.reshape(batch, seq, D_MODEL), norm_mix3, w_in_b]
    if not first:
        in_specs.append(tok(RWKV_WIDTH))
        args.append(vfirst.reshape(batch, seq, RWKV_WIDTH))
    in_specs += [
        pl.BlockSpec((1, 1, SHIFT_COLS), lambda b, t: (l, 0, 0)),
        pl.BlockSpec((1, 8, RWKV_WIDTH), lambda b, t: (l, 0, 0)),
        pl.BlockSpec((1, LORA_W + LORA_A, 2 * RWKV_WIDTH), lambda b, t: (l, 0, 0)),
        pl.BlockSpec((1, RWKV_WIDTH, LORA_V), lambda b, t: (lv, 0, 0)),
        pl.BlockSpec((1, LORA_V, RWKV_WIDTH), lambda b, t: (lv, 0, 0)),
    ]
    args += [mu3, vecs, wla, v1, v2]
    out_specs = [tok(RWKV_WIDTH)]
    out_shape = [jax.ShapeDtypeStruct((batch, seq, RWKV_WIDTH), F32)]
    if first:
        out_specs.append(tok(RWKV_WIDTH))
        out_shape.append(jax.ShapeDtypeStruct((batch, seq, RWKV_WIDTH), F32))
    out_specs += [
        s5_spec, s5_spec,
        pl.BlockSpec((1, 1, SHIFT_COLS), lambda b, t: (b, 0, 0)),
        pl.BlockSpec((1, RWKV_HEADS, RWKV_HEAD, RWKV_HEAD), lambda b, t: (b, 0, 0, 0)),
    ]
    out_shape += [
        s5_shape, s5_shape,
        jax.ShapeDtypeStruct((batch, 1, SHIFT_COLS), F32),
        jax.ShapeDtypeStruct((batch, RWKV_HEADS, RWKV_HEAD, RWKV_HEAD), F32),
    ]
    scratch = [
        pltpu.VMEM((1, SHIFT_COLS), F32),
        pltpu.VMEM((2, HALF_HEADS * RWKV_HEAD, HALF_HEADS * RWKV_HEAD), F32),
    ] + [pltpu.VMEM((tt, RWKV_WIDTH), F32) for _ in range(7)]
    outs = pl.pallas_call(
        functools.partial(_rwkv_prompt_kernel, first_layer=first, tt=tt),
        grid=(batch, seq // tt),
        in_specs=in_specs,
        out_specs=out_specs,
        out_shape=out_shape,
        scratch_shapes=scratch,
        compiler_params=_cparams(("arbitrary", "arbitrary")),
        name="rwkv_prompt",
    )(*args)
    if first:
        o, vf, us5, gs5, sh, wkv = outs
    else:
        (o, us5, gs5, sh, wkv), vf = outs, vfirst
    return (o.reshape(batch * seq, RWKV_WIDTH), vf.reshape(batch * seq, RWKV_WIDTH), us5, gs5,
            sh[:, 0], wkv)


def _rwkv_sample_kernel(*refs, first_layer, seq):
    if first_layer:
        (psh_ref, grw_ref, prev_ref, wkv_ref, mu_ref, vecs_ref, wla_ref, v1_ref, v2_ref,
         o_ref, vf_out_ref, wkv_out_ref) = refs
        vf_ref = None
    else:
        (psh_ref, grw_ref, vf_ref, prev_ref, wkv_ref, mu_ref, vecs_ref, wla_ref, v1_ref, v2_ref,
         o_ref, wkv_out_ref) = refs
        vf_out_ref = None
    nseq = WKV_CHUNK // seq
    p = psh_ref[...]
    rowi = _iota(p.shape, 0)
    prev = jnp.where((rowi & (seq - 1)) == 0, prev_ref[...], pltpu.roll(p, 1, 0))
    ps = p + (prev - p) * mu_ref[0]

    vecs = vecs_ref[0]
    vfirst = None if first_layer else vf_ref[...]
    r, kku, kmod, v, a, lw = _rwkv_prep(ps, vfirst, vecs, wla_ref[0], v1_ref[0], v2_ref[0])
    if first_layer:
        vf_out_ref[...] = v

    def load_head(h):
        sl = slice(h * RWKV_HEAD, (h + 1) * RWKV_HEAD)
        return (r[:, sl], kku[:, sl], kmod[:, sl], v[:, sl], a[:, sl])

    states = [[wkv_ref[0, s, h] for s in range(nseq)] for h in range(RWKV_HEADS)]
    y, new_states = _wkv_chunk(lw, load_head, states, seq, vecs)
    for h in range(RWKV_HEADS):
        for s in range(nseq):
            wkv_out_ref[0, s, h] = new_states[h][s]
    o_ref[...] = y * _silu(grw_ref[...])


def _rwkv_sample(psh, grw, vfirst, prev_rows, wkv_all, mu3, vecs, wla, v1, v2, l, seq):
    rows = psh.shape[0]
    nseq = WKV_CHUNK // seq
    first = vfirst is None
    lv = max(l - 1, 0)
    tok = lambda w: pl.BlockSpec((WKV_CHUNK, w), lambda i: (i, 0))
    st = pl.BlockSpec((1, nseq, RWKV_HEADS, RWKV_HEAD, RWKV_HEAD), lambda i: (l, i, 0, 0, 0))
    in_specs = [tok(SHIFT_COLS), tok(RWKV_WIDTH)]
    args = [psh, grw]
    if not first:
        in_specs.append(tok(RWKV_WIDTH))
        args.append(vfirst)
    in_specs += [
        tok(SHIFT_COLS), st,
        pl.BlockSpec((1, 1, SHIFT_COLS), lambda i: (l, 0, 0)),
        pl.BlockSpec((1, 8, RWKV_WIDTH), lambda i: (l, 0, 0)),
        pl.BlockSpec((1, LORA_W + LORA_A, 2 * RWKV_WIDTH), lambda i: (l, 0, 0)),
        pl.BlockSpec((1, RWKV_WIDTH, LORA_V), lambda i: (lv, 0, 0)),
        pl.BlockSpec((1, LORA_V, RWKV_WIDTH), lambda i: (lv, 0, 0)),
    ]
    wkv_arg = len(args) + 1
    args += [prev_rows, wkv_all, mu3, vecs, wla, v1, v2]
    out_specs = [tok(RWKV_WIDTH)]
    out_shape = [jax.ShapeDtypeStruct((rows, RWKV_WIDTH), F32)]
    if first:
        out_specs.append(tok(RWKV_WIDTH))
        out_shape.append(jax.ShapeDtypeStruct((rows, RWKV_WIDTH), F32))
    out_specs.append(st)
    out_shape.append(jax.ShapeDtypeStruct(wkv_all.shape, F32))
    outs = pl.pallas_call(
        functools.partial(_rwkv_sample_kernel, first_layer=first, seq=seq),
        grid=(rows // WKV_CHUNK,),
        in_specs=in_specs,
        out_specs=out_specs,
        out_shape=out_shape,
        input_output_aliases={wkv_arg: len(out_shape) - 1},
        compiler_params=_cparams(("arbitrary",)),
        name="rwkv_sample",
    )(*args)
    if first:
        o, vf, wkv = outs
    else:
        (o, wkv), vf = outs, vfirst
    sh = psh.reshape(rows // seq, seq, SHIFT_COLS)[:, seq - 1]
    return o, vf, sh, wkv


def _s5_prep_kernel(lam_re_ref, lam_im_ref, ldt_ref, bt_re_ref, bt_im_ref, ct_re_ref, ct_im_ref,
                    wz_ref, wy_ref, ap_re_ref, ap_im_ref, *, cs):
    n = TILE_STATES
    row_g = _iota((LANE, n), 0) >> 4
    lane_g = _iota((LANE, n), 1) >> 6
    emb_mask = row_g == lane_g
    bd_mask = (_iota((LANE, LANE), 0) >> 4) == (_iota((LANE, LANE), 1) >> 4)

    def rep(x):
        return jnp.concatenate([x] * GROUPS_PER_TILE, axis=0)

    def embed(x):
        return jnp.where(emb_mask, rep(x), 0.0)

    for j in range(S5_TILES):
        ls = slice(j * n, (j + 1) * n)
        lr = lam_re_ref[0, :, ls]
        li = lam_im_ref[0, :, ls]
        dt = jnp.exp(ldt_ref[0, :, ls])
        mag = jnp.exp(lr * dt)
        a_re = mag * jnp.cos(li * dt)
        a_im = mag * jnp.sin(li * dt)
        q_re = a_re - 1.0
        q_im = a_im
        den = lr * lr + li * li
        f_re = (q_re * lr + q_im * li) / den
        f_im = (q_im * lr - q_re * li) / den
        b_re = bt_re_ref[0, :, ls]
        b_im = bt_im_ref[0, :, ls]
        bb_re = f_re * b_re - f_im * b_im
        bb_im = f_re * b_im + f_im * b_re
        c_re = ct_re_ref[0, :, ls]
        c_im = ct_im_ref[0, :, ls]
        pw = [(jnp.ones_like(a_re), jnp.zeros_like(a_re))]
        for _ in range(cs):
            pr, pi = pw[-1]
            pw.append((pr * a_re - pi * a_im, pr * a_im + pi * a_re))
        e = [(pr * bb_re - pi * bb_im, pr * bb_im + pi * bb_re) for pr, pi in pw[:cs]]
        wz_ref[0, j, :, 0:n] = jnp.concatenate(
            [embed(e[cs - 1 - s][0]) for s in range(cs)], axis=0).astype(BF16)
        wz_ref[0, j, :, n:2 * n] = jnp.concatenate(
            [embed(e[cs - 1 - s][1]) for s in range(cs)], axis=0).astype(BF16)
        wc_re = jnp.concatenate(
            [embed(c_re * pw[s + 1][0] - c_im * pw[s + 1][1]) for s in range(cs)], axis=0)
        wc_im = jnp.concatenate(
            [embed(-(c_re * pw[s + 1][1] + c_im * pw[s + 1][0])) for s in range(cs)], axis=0)
        wy_ref[0, j, 0:n, :] = wc_re.T.astype(BF16)
        wy_ref[0, j, n:2 * n, :] = wc_im.T.astype(BF16)
        ce_re = embed(c_re)
        ce_im = embed(c_im)
        dt_blk = [jnp.where(bd_mask, _mm_nt_x3(rep(er), ce_re) - _mm_nt_x3(rep(ei), ce_im), 0.0)
                  for er, ei in e]
        zero = jnp.zeros((LANE, LANE), F32)
        wy_ref[0, j, 2 * n:, :] = jnp.concatenate(
            [jnp.concatenate([dt_blk[s - sp] if sp <= s else zero for s in range(cs)], axis=1)
             for sp in range(cs)], axis=0).astype(BF16)
        pr, pi = pw[cs // 2]
        for k in range(S5_SCAN_ROWS):
            ap_re_ref[0, k:k + 1, ls] = pr
            ap_im_ref[0, k:k + 1, ls] = pi
            pr, pi = pr * pr - pi * pi, 2.0 * pr * pi


def _s5_prep(lam_re, lam_im, ldt_rep, bt_re, bt_im, ct_re, ct_im, cs):
    vec = pl.BlockSpec((1, 1, S5_LANES), lambda l: (l, 0, 0))
    mat = pl.BlockSpec((1, S5_GROUP, S5_LANES), lambda l: (l, 0, 0))
    k = cs * LANE
    wz_dims = (S5_TILES, k, 2 * TILE_STATES)
    wy_dims = (S5_TILES, 2 * TILE_STATES + k, k)
    ap_spec = pl.BlockSpec((1, S5_SCAN_ROWS, S5_LANES), lambda l: (l, 0, 0))
    ap_shape = jax.ShapeDtypeStruct((DEPTH, S5_SCAN_ROWS, S5_LANES), F32)
    return pl.pallas_call(
        functools.partial(_s5_prep_kernel, cs=cs),
        grid=(DEPTH,),
        in_specs=[vec, vec, vec, mat, mat, mat, mat],
        out_specs=[pl.BlockSpec((1,) + wz_dims, lambda l: (l, 0, 0, 0)),
                   pl.BlockSpec((1,) + wy_dims, lambda l: (l, 0, 0, 0)), ap_spec, ap_spec],
        out_shape=[jax.ShapeDtypeStruct((DEPTH,) + wz_dims, BF16),
                   jax.ShapeDtypeStruct((DEPTH,) + wy_dims, BF16), ap_shape, ap_shape],
        compiler_params=_cparams(("arbitrary",)),
        name="s5_prep",
    )(lam_re, lam_im, ldt_rep, bt_re, bt_im, ct_re, ct_im)


def _gelu(x):
    return 0.5 * x * (1.0 + lax.erf(x * (1.0 / math.sqrt(2.0))))


def _s5_kernel(*refs, cs, rows, chain, ap0):
    if chain:
        (u_ref, g_ref, wz_ref, wy_ref, ap_re_ref, ap_im_ref,
         dsk_ref, wglu_ref, o_ref, hr_out_ref, hi_out_ref, hc_r_ref, hc_i_ref) = refs

        @pl.when(pl.program_id(1) == 0)
        def _():
            hc_r_ref[...] = jnp.zeros_like(hc_r_ref)
            hc_i_ref[...] = jnp.zeros_like(hc_i_ref)
    else:
        (u_ref, g_ref, h0r_ref, h0i_ref, wz_ref, wy_ref, ap_re_ref, ap_im_ref,
         dsk_ref, wglu_ref, o_ref, hr_out_ref, hi_out_ref) = refs
    n = TILE_STATES
    k = cs * LANE
    if chain:
        steps = rows // SUBLANES

        def load_rows(ref, j, s):
            return jnp.concatenate([ref[j, pl.ds(i * cs + s, SUBLANES, stride=S5_SEG_PITCH), :]
                                    for i in range(steps)], axis=0)

        def store_rows(ref, j, s, val):
            for i in range(steps):
                ref[j, pl.ds(i * cs + s, SUBLANES, stride=S5_SEG_PITCH), :] = (
                    val[i * SUBLANES:(i + 1) * SUBLANES])

        pad = jnp.zeros((S5_SEG_PITCH - S5_SEG, LANE), F32)
        for j in range(S5_TILES):
            for g in range(SUBLANES):
                o_ref[j, g * S5_SEG_PITCH + S5_SEG:(g + 1) * S5_SEG_PITCH, :] = pad
    else:
        def load_rows(ref, j, s):
            return ref[j, pl.ds(s, rows, stride=cs), :]

        def store_rows(ref, j, s, val):
            ref[j, pl.ds(s, rows, stride=cs), :] = val

    us = [[load_rows(u_ref, j, s) for s in range(cs)] for j in range(S5_TILES)]
    x8 = [jnp.concatenate(us[j], axis=1).astype(BF16) for j in range(S5_TILES)]
    z = [jnp.dot(x8[j], wz_ref[0, j], preferred_element_type=F32) for j in range(S5_TILES)]
    hin = []
    for j in range(S5_TILES):
        ls = slice(j * n, (j + 1) * n)
        zr, zi = z[j][:, :n], z[j][:, n:]
        ar, ai = ap_re_ref[0, ap0:ap0 + 1, ls], ap_im_ref[0, ap0:ap0 + 1, ls]
        if chain:
            hr = jnp.zeros((SUBLANES, n), F32)
            hi = jnp.zeros((SUBLANES, n), F32)
            loc = []
            for i in range(steps):
                loc.append((hr, hi))
                rs = slice(i * SUBLANES, (i + 1) * SUBLANES)
                hr, hi = ar * hr - ai * hi + zr[rs], ar * hi + ai * hr + zi[rs]
            lvl = ap0 + int(math.log2(steps))
            sr, si = ap_re_ref[0, lvl:lvl + 1, ls], ap_im_ref[0, lvl:lvl + 1, ls]
            cr, ci = hc_r_ref[:, ls], hc_i_ref[:, ls]
            seg_r, seg_i = [], []
            for g in range(SUBLANES):
                seg_r.append(cr)
                seg_i.append(ci)
                cr, ci = sr * cr - si * ci + hr[g:g + 1], sr * ci + si * cr + hi[g:g + 1]
            hc_r_ref[:, ls] = cr
            hc_i_ref[:, ls] = ci
            hr_out_ref[0, :, ls] = cr
            hi_out_ref[0, :, ls] = ci
            seg_r = jnp.concatenate(seg_r, axis=0)
            seg_i = jnp.concatenate(seg_i, axis=0)
            pr, pi = jnp.ones_like(ar), jnp.zeros_like(ar)
            hin_r, hin_i = [], []
            for i in range(steps):
                hin_r.append(loc[i][0] + pr * seg_r - pi * seg_i)
                hin_i.append(loc[i][1] + pr * seg_i + pi * seg_r)
                pr, pi = pr * ar - pi * ai, pr * ai + pi * ar
            hin_r = jnp.concatenate(hin_r, axis=0)
            hin_i = jnp.concatenate(hin_i, axis=0)
        else:
            hin_r, hin_i = h0r_ref[:, ls], h0i_ref[:, ls]
            hr_out_ref[:, ls] = ar * hin_r - ai * hin_i + zr
            hi_out_ref[:, ls] = ar * hin_i + ai * hin_r + zi
        hin.append((hin_r.astype(BF16), hin_i.astype(BF16)))
    y8 = []
    for j in range(S5_TILES):
        lhs = jnp.concatenate([hin[j][0], hin[j][1], x8[j]], axis=1)
        cols = []
        for b in range(k // 256):
            kk = 2 * n + (b + 1) * 256
            cols.append(jnp.dot(lhs[:, :kk], wy_ref[0, j, 0:kk, b * 256:(b + 1) * 256],
                                preferred_element_type=F32))
        y8.append(jnp.concatenate(cols, axis=1))
    dsk = dsk_ref[0]
    for s in range(cs):
        y = jnp.concatenate([y8[j][:, s * LANE:(s + 1) * LANE] for j in range(S5_TILES)], axis=1)
        u_s = jnp.concatenate([us[j][s] for j in range(S5_TILES)], axis=1)
        y = _gelu(y + dsk * u_s)
        y = y * _sigmoid(jnp.dot(y.astype(BF16), wglu_ref[0], preferred_element_type=F32))
        gate = jnp.concatenate([load_rows(g_ref, j, s) for j in range(S5_TILES)], axis=1)
        out = y * _silu(gate)
        for j in range(S5_TILES):
            store_rows(o_ref, j, s, out[:, j * LANE:(j + 1) * LANE])


def _s5_mix(u, g, h0, ops, dsk3, wglu_b, l, cs, nb):
    chain = h0 is None
    wz, wy, ap_re, ap_im = ops
    if chain:
        ntok = SUBLANES * S5_SEG
        assert S5_SEG % cs == 0 and (S5_SEG // cs) & (S5_SEG // cs - 1) == 0
        tiles = u.shape[1] // S5_SEG_PITCH // SUBLANES // nb
        tok = pl.BlockSpec((S5_TILES, SUBLANES * S5_SEG_PITCH, LANE), lambda b, t: (0, b * tiles + t, 0))
    else:
        ntok = u.shape[1]
        tiles = 1
        tok = pl.BlockSpec((S5_TILES, ntok, LANE), lambda b, t: (0, 0, 0))
    rows = ntok // cs
    ap_spec = pl.BlockSpec((1, S5_SCAN_ROWS, S5_LANES), lambda b, t: (l, 0, 0))
    in_specs = [tok, tok]
    args = [u, g]
    scratch = []
    if not chain:
        hs = pl.BlockSpec((rows, S5_LANES), lambda b, t: (0, 0))
        in_specs += [hs, hs]
        args += [h0[0], h0[1]]
        st_spec = hs
        st_shape = jax.ShapeDtypeStruct((rows, S5_LANES), F32)
    else:
        st_spec = pl.BlockSpec((1, 1, S5_LANES), lambda b, t: (b, 0, 0))
        st_shape = jax.ShapeDtypeStruct((nb, 1, S5_LANES), F32)
        scratch = [pltpu.VMEM((1, S5_LANES), F32), pltpu.VMEM((1, S5_LANES), F32)]
    k = cs * LANE
    cs_ops = wz.shape[2] // LANE
    ap0 = int(math.log2(2 * cs // cs_ops))
    in_specs += [
        pl.BlockSpec((1, S5_TILES, k, 2 * TILE_STATES), lambda b, t: (l, 0, cs_ops // cs - 1, 0)),
        pl.BlockSpec((1, S5_TILES, 2 * TILE_STATES + k, k), lambda b, t: (l, 0, 0, 0)),
        ap_spec, ap_spec,
        pl.BlockSpec((1, 1, S5_WIDTH), lambda b, t: (l, 0, 0)),
        pl.BlockSpec((1, S5_WIDTH, S5_WIDTH), lambda b, t: (l, 0, 0)),
    ]
    args += [wz, wy, ap_re, ap_im, dsk3, wglu_b]
    o, hr, hi = pl.pallas_call(
        functools.partial(_s5_kernel, cs=cs, rows=rows, chain=chain, ap0=ap0),
        grid=(nb, tiles),
        in_specs=in_specs,
        out_specs=[tok, st_spec, st_spec],
        out_shape=[jax.ShapeDtypeStruct(u.shape, F32), st_shape, st_shape],
        scratch_shapes=scratch,
        compiler_params=_cparams(("arbitrary", "arbitrary")),
        name="s5_mix",
    )(*args)
    return o, hr, hi


def _softmax(s):
    e = jnp.exp(s - jnp.max(s, axis=-1, keepdims=True))
    return (e / jnp.sum(e, axis=-1, keepdims=True)).astype(BF16)


def _attend(q, mk_ref, mv_ref):
    heads = [slice(h * X_HEAD_DIM, (h + 1) * X_HEAD_DIM) for h in range(X_HEADS)]
    s = [_mm_nt(q[:, hs], mk_ref[0, 0, :, hs]) for hs in heads]
    p = [_softmax(sh) for sh in s]
    return jnp.concatenate([_mm(ph, mv_ref[0, 0, :, hs]) for ph, hs in zip(p, heads)], axis=-1)


def _attend_tiled(q, mk_ref, mv_ref, nseg, row_seg):
    def head(ref, i, h):
        return jnp.concatenate([ref[0, i, pl.ds(dt * X_HEADS + h, MEM_LEN, stride=_MEM_SUB), :]
                                for dt in range(_DT)], axis=1)

    def pick(vals):
        out = vals[0]
        for i in range(1, nseg):
            out = jnp.where(row_seg == i, vals[i], out)
        return out

    qb = q.astype(BF16)
    heads = [slice(h * X_HEAD_DIM, (h + 1) * X_HEAD_DIM) for h in range(X_HEADS)]
    s = [[_mm_nt(qb[:, hs], head(mk_ref, i, h)) for i in range(nseg)] for h, hs in enumerate(heads)]
    p = [_softmax(pick(sh)) for sh in s]
    outs = [pick([_mm(p[h], head(mv_ref, i, h)) for i in range(nseg)]) for h in range(X_HEADS)]
    return jnp.concatenate(outs, axis=-1)


def _post_kernel(x_ref, orw_ref, os5_ref, wout_ref, nx_ref, wq_ref, wo_ref, mk_ref, mv_ref, nf_ref,
                 o_ref, *, nseg, seg_rows, final):
    x = x_ref[...]
    if nseg == 1:
        nq = x.shape[0] // S5_SEG
        os5 = jnp.concatenate(
            [jnp.concatenate([os5_ref[j, q * S5_SEG_PITCH:q * S5_SEG_PITCH + S5_SEG, :] for q in range(nq)], axis=0)
             for j in range(S5_TILES)], axis=1)
    else:
        os5 = jnp.concatenate([os5_ref[j] for j in range(S5_TILES)], axis=1)
    x1 = (x + jnp.dot(orw_ref[...].astype(BF16), wout_ref[0, 0:RWKV_WIDTH, :], preferred_element_type=F32)
          + jnp.dot(os5.astype(BF16), wout_ref[0, RWKV_WIDTH:, :], preferred_element_type=F32))
    xc = _rms(x1, nx_ref[0]).astype(BF16)
    q = jnp.dot(xc, wq_ref[0], preferred_element_type=F32) * (1.0 / math.sqrt(X_HEAD_DIM))
    if nseg == 1:
        att = _attend(q, mk_ref, mv_ref)
    else:
        row_seg = _iota((q.shape[0], 1), 0) >> int(math.log2(seg_rows))
        att = _attend_tiled(q, mk_ref, mv_ref, nseg, row_seg)
    x2 = x1 + jnp.dot(att.astype(BF16), wo_ref[0], preferred_element_type=F32)
    if final:
        x2 = _rms(x2, nf_ref[...])
    o_ref[...] = x2


def _post(x2d, orw, os5, wout_b, nx3, wq_b, wo_b, mk, mv, nf2, l, nb, nseg, seg_rows, tq, final):
    tiles = x2d.shape[0] // nb // tq
    tok = lambda w: pl.BlockSpec((tq, w), lambda b, t: (b * tiles + t, 0))
    s5_rows = tq // S5_SEG * S5_SEG_PITCH if nseg == 1 else tq
    tok_s5 = pl.BlockSpec((S5_TILES, s5_rows, LANE), lambda b, t: (0, b * tiles + t, 0))
    if nseg == 1:
        mem = pl.BlockSpec((1, 1, MEM_LEN, D_MODEL), lambda b, t: (l, b, 0, 0))
    else:
        mem = pl.BlockSpec((1, nseg, MEM_LEN * _MEM_SUB, LANE), lambda b, t: (l, t, 0, 0))
    wsp = pl.BlockSpec((1, D_MODEL, D_MODEL), lambda b, t: (l, 0, 0))
    return pl.pallas_call(
        functools.partial(_post_kernel, nseg=nseg, seg_rows=seg_rows, final=final),
        grid=(nb, tiles),
        in_specs=[tok(D_MODEL), tok(RWKV_WIDTH), tok_s5, wsp,
                  pl.BlockSpec((1, 1, D_MODEL), lambda b, t: (l, 0, 0)), wsp, wsp, mem, mem,
                  pl.BlockSpec((1, D_MODEL), lambda b, t: (0, 0))],
        out_specs=tok(D_MODEL),
        out_shape=jax.ShapeDtypeStruct(x2d.shape, F32),
        compiler_params=_cparams(("arbitrary", "arbitrary")),
        name="post",
    )(x2d, orw, os5, wout_b, nx3, wq_b, wo_b, mk, mv, nf2)


def _trunk(x, states, mem_k, mem_v, W):
    batch, seq, _ = x.shape
    prompt = states is None
    x2d = x.reshape(batch * seq, D_MODEL)
    new_shift, new_wkv, new_re, new_im = [], [], [], []
    vfirst = None
    wkv_all = None if prompt else states[1]
    for l in range(DEPTH):
        rw_args = (W['mu'], W['vecs'], W['wla'], W['v1'], W['v2'], l)
        if prompt:
            orw, vfirst, us5, gs5, sh, wkv = _rwkv_prompt(x2d, W['norm_mix'], W['w_in'], vfirst, *rw_args,
                                                          batch, seq)
            os5, hr, hi = _s5_mix(us5, gs5, None, W['s5_ops'], W['d_skip'], W['w_glu'], l, S5_CHUNK, batch)
            hr = hr.reshape(batch, S5_GROUPS, S5_STATE)
            hi = hi.reshape(batch, S5_GROUPS, S5_STATE)
            x2d = _post(x2d, orw, os5, W['w_out'], W['norm_x'], W['wq'], W['wo'], mem_k, mem_v,
                        W['norm_f'], l, batch, 1, seq, 1024, l == DEPTH - 1)
        else:
            shift0, _, re0, im0 = states
            psh, grw, us5, gs5 = _inproj(x2d, W['norm_mix'], W['w_in'], l)
            prev_rows = jnp.repeat(shift0[l], seq, axis=0)
            orw, vfirst, sh, wkv_all = _rwkv_sample(psh, grw, vfirst, prev_rows, wkv_all, *rw_args, seq)
            h0 = (re0[l].reshape(batch, S5_LANES), im0[l].reshape(batch, S5_LANES))
            os5, hr, hi = _s5_mix(us5, gs5, h0, W['s5_ops'], W['d_skip'], W['w_glu'], l, seq, 1)
            hr = hr.reshape(batch, S5_GROUPS, S5_STATE)
            hi = hi.reshape(batch, S5_GROUPS, S5_STATE)
            nseg = 32 // seq
            x2d = _post(x2d, orw, os5, W['w_out'], W['norm_x'], W['wq'], W['wo'], mem_k, mem_v,
                        W['norm_f'], l, 1, nseg, seq, nseg * seq, l == DEPTH - 1)
        new_shift.append(sh)
        if prompt:
            new_wkv.append(wkv)
        new_re.append(hr)
        new_im.append(hi)
    y = x2d.reshape(batch, seq, D_MODEL)
    wkv_out = jnp.stack(new_wkv) if prompt else wkv_all
    return y, jnp.stack(new_shift), wkv_out, jnp.stack(new_re), jnp.stack(new_im)


def kernel(x_prompt, x_sample, state_shift, state_wkv, state_s5_re, state_s5_im, cache_mem_k, cache_mem_v, mem_prompt, norm_mix, w_in, mu_shift, w0, w2, a0, a2, v0, v1, v2, k_k, k_a, r_k, gn_w, gn_b, lam_re, lam_im, log_dt, b_re, b_im, c_re, c_im, d_skip, w_glu, w_out, norm_x, norm_mem, wq, wk, wv, wo, norm_f):
    L = DEPTH
    bp, mp = mem_prompt.shape[0], mem_prompt.shape[1]
    zpad = jnp.zeros((1, RWKV_WIDTH), F32)
    vecs = jnp.stack([w0, a0, k_k, k_a, gn_w, gn_b, r_k.reshape(L, RWKV_WIDTH),
                      jnp.concatenate([zpad, v0], axis=0)], axis=1)
    zl = jnp.zeros((L, LORA_W, RWKV_WIDTH), F32)
    wla = jnp.concatenate([jnp.concatenate([w2, zl], axis=2),
                           jnp.concatenate([zl, a2], axis=2)], axis=1).astype(BF16)
    tr = lambda t: jnp.transpose(t, (0, 3, 1, 2)).reshape(L, S5_GROUP, S5_LANES)
    s5_in = (lam_re.reshape(L, 1, S5_LANES), lam_im.reshape(L, 1, S5_LANES),
             jnp.repeat(log_dt, S5_STATE, axis=1).reshape(L, 1, S5_LANES),
             tr(b_re), tr(b_im),
             jnp.transpose(c_re, (0, 2, 1, 3)).reshape(L, S5_GROUP, S5_LANES),
             jnp.transpose(c_im, (0, 2, 1, 3)).reshape(L, S5_GROUP, S5_LANES))
    W = dict(
        norm_mix=norm_mix.reshape(L, 1, D_MODEL), w_in=w_in.astype(BF16),
        mu=mu_shift.reshape(L, 1, SHIFT_COLS), vecs=vecs, wla=wla,
        v1=v1.astype(BF16), v2=v2.astype(BF16),
        s5_ops=_s5_prep(*s5_in, S5_CHUNK),
        d_skip=d_skip.reshape(L, 1, S5_WIDTH), w_glu=w_glu.astype(BF16),
        w_out=w_out.astype(BF16), norm_x=norm_x.reshape(L, 1, D_MODEL),
        wq=wq.astype(BF16), wo=wo.astype(BF16), norm_f=norm_f.reshape(1, D_MODEL),
    )
    pk, pv, pk_t, pv_t = _memkv(mem_prompt.reshape(bp * mp, D_MODEL), norm_mem,
                                wk.astype(BF16), wv.astype(BF16))
    pk = pk.reshape(L, bp, mp, D_MODEL)
    pv = pv.reshape(L, bp, mp, D_MODEL)
    y_prompt, p_shift, p_wkv, p_re, p_im = _trunk(x_prompt, None, pk, pv, W)
    tiled = lambda c: jnp.transpose(c.reshape(L, -1, MEM_LEN, X_HEADS, _DT, LANE),
                                    (0, 1, 2, 4, 3, 5)).reshape(L, -1, MEM_LEN * _MEM_SUB, LANE)
    untiled = lambda c: jnp.transpose(c.reshape(L, -1, MEM_LEN, _DT, X_HEADS, LANE),
                                      (0, 1, 2, 4, 3, 5)).reshape(L, -1, MEM_LEN, X_HEADS, X_HEAD_DIM)
    y_sample, s_shift, s_wkv, s_re, s_im = _trunk(
        x_sample, (state_shift, state_wkv, state_s5_re, state_s5_im), tiled(cache_mem_k),
        tiled(cache_mem_v), W)
    return (y_prompt, y_sample, p_shift, p_wkv, p_re, p_im, untiled(pk_t), untiled(pv_t),
            s_shift, s_wkv, s_re, s_im)
```

```python
import functools
import math

import jax
import jax.numpy as jnp
from jax import lax
from jax.experimental import pallas as pl
from jax.experimental.pallas import tpu as pltpu

F32 = jnp.float32
BF16 = jnp.bfloat16

D_MODEL = 1024
DEPTH = 4
RWKV_WIDTH = 512
RWKV_HEAD = 64
RWKV_HEADS = 8
LORA_W = 64
LORA_A = 64
LORA_V = 32
S5_WIDTH = 512
S5_GROUP = 16
S5_GROUPS = 32
S5_STATE = 64
S5_LANES = S5_GROUPS * S5_STATE
MEM_LEN = 256
X_HEADS = 4
X_HEAD_DIM = 256
SHIFT_COLS = 3 * RWKV_WIDTH + LORA_W + LORA_A
IN_COLS = SHIFT_COLS + RWKV_WIDTH + 2 * S5_WIDTH
NORM_EPS = 1e-6
GN_EPS = 64e-5

LANE = 128
SUBLANES = 8
WKV_CHUNK = 64
HALF_HEADS = 4
WKV_UNROLL = 8
S5_TILES = S5_WIDTH // LANE
GROUPS_PER_TILE = LANE // S5_GROUP
TILE_STATES = GROUPS_PER_TILE * S5_STATE
S5_CHUNK = 8
S5_SCAN_ROWS = 8
S5_SEG = 128
S5_SEG_PITCH = S5_SEG + SUBLANES
VMEM_LIMIT = 56 * 1024 * 1024


def _cparams(sem):
    return pltpu.CompilerParams(dimension_semantics=sem, vmem_limit_bytes=VMEM_LIMIT)


def _mm(a, b):
    return jnp.dot(a.astype(BF16), b.astype(BF16), preferred_element_type=F32)


def _mm_nt(a, b):
    return lax.dot_general(a.astype(BF16), b.astype(BF16), (((1,), (1,)), ((), ())),
                           preferred_element_type=F32)


def _mm_tn(a, b):
    return lax.dot_general(a.astype(BF16), b.astype(BF16), (((0,), (0,)), ((), ())),
                           preferred_element_type=F32)


def _mm_f32(a, b):
    return jnp.dot(a, b, preferred_element_type=F32, precision=lax.Precision.HIGHEST)


def _mm_nt_x3(a, b):
    a_hi, b_hi = a.astype(BF16), b.astype(BF16)
    a_lo = (a - a_hi.astype(F32)).astype(BF16)
    b_lo = (b - b_hi.astype(F32)).astype(BF16)
    return _mm_nt(a_hi, b_hi) + _mm_nt(a_hi, b_lo) + _mm_nt(a_lo, b_hi)


def _rms(x, g):
    return x * lax.rsqrt(jnp.mean(x * x, axis=-1, keepdims=True) + NORM_EPS) * g


def _sigmoid(x):
    return 1.0 / (1.0 + jnp.exp(-x))


def _silu(x):
    return x * _sigmoid(x)


def _iota(shape, dim):
    return lax.broadcasted_iota(jnp.int32, shape, dim)


_DT = X_HEAD_DIM // LANE
_MEM_SUB = X_HEADS * _DT


def _memkv_kernel(m_ref, g_ref, wk_ref, wv_ref, k_ref, v_ref, kt_ref, vt_ref):
    mn = _rms(m_ref[...], g_ref[0]).astype(BF16)
    tm = m_ref.shape[0]
    for w_ref, o_ref, t_ref in ((wk_ref, k_ref, kt_ref), (wv_ref, v_ref, vt_ref)):
        kv = jnp.dot(mn, w_ref[0], preferred_element_type=F32)
        o_ref[0] = kv
        for h in range(X_HEADS):
            for dt in range(_DT):
                lo = h * X_HEAD_DIM + dt * LANE
                t_ref[0, pl.ds(dt * X_HEADS + h, tm, stride=_MEM_SUB), :] = kv[:, lo:lo + LANE]


def _memkv(mem2d, norm_mem, wk_b, wv_b):
    rows = mem2d.shape[0]
    tm = 512
    out = jax.ShapeDtypeStruct((DEPTH, rows, D_MODEL), F32)
    out_t = jax.ShapeDtypeStruct((DEPTH, rows * _MEM_SUB, LANE), F32)
    nat = pl.BlockSpec((1, tm, D_MODEL), lambda l, i: (l, i, 0))
    til = pl.BlockSpec((1, tm * _MEM_SUB, LANE), lambda l, i: (l, i, 0))
    return pl.pallas_call(
        _memkv_kernel,
        grid=(DEPTH, rows // tm),
        in_specs=[
            pl.BlockSpec((tm, D_MODEL), lambda l, i: (i, 0)),
            pl.BlockSpec((1, 1, D_MODEL), lambda l, i: (l, 0, 0)),
            pl.BlockSpec((1, D_MODEL, D_MODEL), lambda l, i: (l, 0, 0)),
            pl.BlockSpec((1, D_MODEL, D_MODEL), lambda l, i: (l, 0, 0)),
        ],
        out_specs=[nat, nat, til, til],
        out_shape=[out, out, out_t, out_t],
        compiler_params=_cparams(("arbitrary", "arbitrary")),
        name="memkv",
    )(mem2d, norm_mem.reshape(DEPTH, 1, D_MODEL), wk_b, wv_b)


_IN_SPLITS = (0, SHIFT_COLS, SHIFT_COLS + RWKV_WIDTH, SHIFT_COLS + RWKV_WIDTH + S5_WIDTH, IN_COLS)


def _inproj_kernel(x_ref, g_ref, w_ref, psh_ref, grw_ref, us5_ref, gs5_ref):
    xn = _rms(x_ref[...], g_ref[0]).astype(BF16)
    proj = lambda lo, hi: jnp.dot(xn, w_ref[0, :, lo:hi], preferred_element_type=F32)
    psh_ref[...] = proj(_IN_SPLITS[0], _IN_SPLITS[1])
    grw_ref[...] = proj(_IN_SPLITS[1], _IN_SPLITS[2])
    for o_ref, lo in ((us5_ref, _IN_SPLITS[2]), (gs5_ref, _IN_SPLITS[3])):
        cols = proj(lo, lo + S5_WIDTH)
        for j in range(S5_TILES):
            o_ref[j] = cols[:, j * LANE:(j + 1) * LANE]


def _inproj(x2d, norm_mix3, w_in_b, l):
    rows = x2d.shape[0]
    tm = 256
    s5_spec = pl.BlockSpec((S5_TILES, tm, LANE), lambda i: (0, i, 0))
    s5_shape = jax.ShapeDtypeStruct((S5_TILES, rows, LANE), F32)
    return pl.pallas_call(
        _inproj_kernel,
        grid=(rows // tm,),
        in_specs=[
            pl.BlockSpec((tm, D_MODEL), lambda i: (i, 0)),
            pl.BlockSpec((1, 1, D_MODEL), lambda i: (l, 0, 0)),
            pl.BlockSpec((1, D_MODEL, IN_COLS), lambda i: (l, 0, 0)),
        ],
        out_specs=[pl.BlockSpec((tm, SHIFT_COLS), lambda i: (i, 0)),
                   pl.BlockSpec((tm, RWKV_WIDTH), lambda i: (i, 0)), s5_spec, s5_spec],
        out_shape=[jax.ShapeDtypeStruct((rows, SHIFT_COLS), F32),
                   jax.ShapeDtypeStruct((rows, RWKV_WIDTH), F32), s5_shape, s5_shape],
        compiler_params=_cparams(("arbitrary",)),
        name="inproj",
    )(x2d, norm_mix3, w_in_b)


_V_W0, _V_A0, _V_KK, _V_KA, _V_GNW, _V_GNB, _V_RK, _V_V0 = range(8)


def _rwkv_prep(ps, vfirst, vecs, wla, v1, v2):
    r = ps[:, 0:RWKV_WIDTH]
    k = ps[:, RWKV_WIDTH:2 * RWKV_WIDTH]
    v = ps[:, 2 * RWKV_WIDTH:3 * RWKV_WIDTH]
    lora_in = ps[:, 3 * RWKV_WIDTH:SHIFT_COLS]
    lane = _iota(lora_in.shape, 1)
    lora_in = jnp.where(lane < LORA_W, jnp.tanh(lora_in), lora_in)
    lora = _mm(lora_in, wla)
    wpre = vecs[_V_W0:_V_W0 + 1] + lora[:, 0:RWKV_WIDTH]
    z = -wpre
    softplus = jnp.maximum(z, 0.0) + jnp.log(1.0 + jnp.exp(-jnp.abs(z)))
    logdecay = -jnp.exp(-softplus - 0.5)
    a = _sigmoid(vecs[_V_A0:_V_A0 + 1] + lora[:, RWKV_WIDTH:2 * RWKV_WIDTH])
    if vfirst is not None:
        gate = _sigmoid(vecs[_V_V0:_V_V0 + 1] + _mm(_mm(v, v1), v2))
        v = v + (vfirst - v) * gate
    kku = k * vecs[_V_KK:_V_KK + 1]
    kmod = k * (1.0 + (a - 1.0) * vecs[_V_KA:_V_KA + 1])
    return r, kku, kmod, v, a, logdecay


def _wkv_chunk(lw, load_head, states, seg_len, vecs):
    C = WKV_CHUNK
    shift = int(math.log2(seg_len))
    per_grp = SUBLANES // seg_len
    ngrp = C // SUBLANES
    row = _iota((C, C), 0)
    col = _iota((C, C), 1)
    same = (row >> shift) == (col >> shift)
    incl = jnp.logical_and(col <= row, same)
    strict = jnp.logical_and(col < row, same)
    cum = _mm_f32(incl.astype(F32), lw)
    tot = _mm_f32(same.astype(F32), lw)
    g_inc_all = jnp.exp(cum)
    g_exc_all = jnp.exp(cum - lw)
    g_inv_all = jnp.exp(-cum)
    g_end_all = jnp.exp(tot - cum)
    g_tot_all = jnp.exp(tot)
    eye = (row == col).astype(F32)
    nsq = max(shift - 1, 0)
    seg16 = (_iota((2 * SUBLANES, 1), 0) & (SUBLANES - 1)) >> shift

    hd = []
    for h in range(RWKV_HEADS):
        sl = slice(h * RWKV_HEAD, (h + 1) * RWKV_HEAD)
        r, kku, kmod, v, a = load_head(h)
        ss = jnp.sum(kku * kku, axis=-1, keepdims=True)
        kkn = kku * lax.rsqrt(jnp.maximum(ss, 1e-24))
        b = kkn * a
        qk = kkn * g_exc_all[:, sl]
        qr = r * g_inc_all[:, sl]
        hd.append(dict(sl=sl, r=r, kmod=kmod, v=v, qk=qk, qr=qr,
                       q2=jnp.concatenate([qk, qr], axis=0).astype(BF16),
                       kd=kmod * g_inv_all[:, sl], bd=b * g_inv_all[:, sl],
                       kc=kmod * g_end_all[:, sl], bc=b * g_end_all[:, sl]))
    for d in hd:
        d['ak'] = _mm_nt(d['q2'], d['kd'])
        d['ab'] = _mm_nt(d['q2'], d['bd'])
    for d in hd:
        d['akk'] = jnp.where(strict, d['ak'][:C], 0.0)
        d['ark'] = jnp.where(incl, d['ak'][C:], 0.0)
        d['arb'] = jnp.where(incl, d['ab'][C:], 0.0)
        d['n'] = -jnp.where(strict, d['ab'][:C], 0.0)
        d['tinv'] = eye + d['n']
    for _ in range(nsq):
        for d in hd:
            d['n'] = _mm(d['n'], d['n'])
        for d in hd:
            d['tinv'] = d['tinv'] + _mm(d['tinv'], d['n'])
    for d in hd:
        d['av'] = _mm(jnp.concatenate([d['akk'], d['ark']], axis=0), d['v'])
    for d in hd:
        d['u_ind'] = _mm(d['tinv'], d['av'][:C])
        d['tq'] = _mm(d['tinv'], d['qk'])
    for h, d in enumerate(hd):
        us, yq = [], []
        for g in range(ngrp):
            rs = slice(g * SUBLANES, (g + 1) * SUBLANES)
            lhs = jnp.concatenate([d['tq'][rs], d['qr'][rs]], axis=0).astype(BF16)
            qs = None
            for s in range(per_grp):
                o = _mm_nt(lhs, states[h][g * per_grp + s])
                qs = o if qs is None else jnp.where(seg16 == s, o, qs)
            us.append(d['u_ind'][rs] + qs[:SUBLANES])
            yq.append(qs[SUBLANES:])
        d['u'] = jnp.concatenate(us, axis=0)
        d['yq'] = jnp.concatenate(yq, axis=0)
    for d in hd:
        d['y'] = d['yq'] + d['av'][C:] - _mm(d['arb'], d['u'])
    new_states = []
    for h, d in enumerate(hd):
        hs = []
        for g in range(ngrp):
            rs = slice(g * SUBLANES, (g + 1) * SUBLANES)
            vu = jnp.concatenate([d['v'][rs], d['u'][rs]], axis=0)
            kb = jnp.concatenate([d['kc'][rs], -d['bc'][rs]], axis=0).astype(BF16)
            for s in range(per_grp):
                i = g * per_grp + s
                gt = g_tot_all[i * seg_len:i * seg_len + 1, d['sl']]
                hs.append(states[h][i] * gt + _mm_tn(jnp.where(seg16 == s, vu, 0.0), kb))
        new_states.append(hs)
    ys = []
    for d in hd:
        y, sl = d['y'], d['sl']
        mu = jnp.mean(y, axis=-1, keepdims=True)
        yc = y - mu
        var = jnp.mean(yc * yc, axis=-1, keepdims=True)
        yn = yc * lax.rsqrt(var + GN_EPS) * vecs[_V_GNW:_V_GNW + 1, sl] + vecs[_V_GNB:_V_GNB + 1, sl]
        bonus = jnp.sum(d['r'] * d['kmod'] * vecs[_V_RK:_V_RK + 1, sl], axis=-1, keepdims=True)
        ys.append(yn + bonus * d['v'])
    return jnp.concatenate(ys, axis=-1), new_states


def _rwkv_prompt_kernel(*refs, first_layer, tt):
    if first_layer:
        (x_ref, nm_ref, win_ref, mu_ref, vecs_ref, wla_ref, v1_ref, v2_ref,
         o_ref, vf_out_ref, us5_ref, gs5_ref, shift_out_ref, wkv_out_ref,
         carry_ref, s_ref, r_s, kkn_s, kmod_s, v_s, b_s, lw_s, y_s) = refs
        vf_ref = None
    else:
        (x_ref, nm_ref, win_ref, vf_ref, mu_ref, vecs_ref, wla_ref, v1_ref, v2_ref,
         o_ref, us5_ref, gs5_ref, shift_out_ref, wkv_out_ref,
         carry_ref, s_ref, r_s, kkn_s, kmod_s, v_s, b_s, lw_s, y_s) = refs
        vf_out_ref = None
    t = pl.program_id(1)

    @pl.when(t == 0)
    def _():
        carry_ref[...] = jnp.zeros_like(carry_ref)
        s_ref[...] = jnp.zeros_like(s_ref)

    xn = _rms(x_ref[0], nm_ref[0]).astype(BF16)
    proj = lambda lo, hi: jnp.dot(xn, win_ref[0, :, lo:hi], preferred_element_type=F32)
    p = proj(_IN_SPLITS[0], _IN_SPLITS[1])
    o_ref[0] = proj(_IN_SPLITS[1], _IN_SPLITS[2])
    pad = jnp.zeros((S5_SEG_PITCH - S5_SEG, LANE), F32)
    for s5_ref, lo in ((us5_ref, _IN_SPLITS[2]), (gs5_ref, _IN_SPLITS[3])):
        cols = proj(lo, lo + S5_WIDTH)
        for j in range(S5_TILES):
            for q in range(tt // S5_SEG):
                s5_ref[j, q * S5_SEG_PITCH:q * S5_SEG_PITCH + S5_SEG, :] = (
                    cols[q * S5_SEG:(q + 1) * S5_SEG, j * LANE:(j + 1) * LANE])
                s5_ref[j, q * S5_SEG_PITCH + S5_SEG:(q + 1) * S5_SEG_PITCH, :] = pad
    rowi = _iota(p.shape, 0)
    prev = jnp.where(rowi == 0, carry_ref[...], pltpu.roll(p, 1, 0))
    ps = p + (prev - p) * mu_ref[0]
    last = p[tt - 1:tt, :]
    carry_ref[...] = last
    shift_out_ref[0] = last

    vecs = vecs_ref[0]
    vfirst = None if first_layer else vf_ref[0]
    r, kku, kmod, v, a, lw = _rwkv_prep(ps, vfirst, vecs, wla_ref[0], v1_ref[0], v2_ref[0])
    if first_layer:
        vf_out_ref[0] = v

    hw = HALF_HEADS * RWKV_HEAD
    bd_bool = (_iota((hw, hw), 0) >> 6) == (_iota((hw, hw), 1) >> 6)
    bd_ones = bd_bool.astype(F32).astype(BF16)

    def head_sums(x):
        n = x.shape[0]
        hi = x.astype(BF16)
        lo = (x - hi.astype(F32)).astype(BF16)
        st = jnp.concatenate([hi, lo], axis=0)
        outs = []
        for hf in range(2):
            s2 = jnp.dot(st[:, hf * hw:(hf + 1) * hw], bd_ones, preferred_element_type=F32)
            outs.append(s2[:n] + s2[n:])
        return jnp.concatenate(outs, axis=1)

    kkn = kku * lax.rsqrt(jnp.maximum(head_sums(kku * kku), 1e-24))
    r_s[...] = r
    kkn_s[...] = kkn
    kmod_s[...] = kmod
    v_s[...] = v
    b_s[...] = kkn * a
    lw_s[...] = lw

    C = WKV_CHUNK
    trow = _iota((C, hw), 0)
    tcol = _iota((C, hw), 1) & (RWKV_HEAD - 1)
    strict = tcol < trow
    incl = tcol <= trow
    eye = (tcol == trow).astype(F32)
    crow = _iota((C, RWKV_WIDTH), 0)

    def bd(x):
        return jnp.concatenate([x.astype(BF16)] * HALF_HEADS, axis=0) * bd_ones

    def group(g, carry):
        chains = []
        for k in range(WKV_UNROLL):
            rows = pl.ds(pl.multiple_of((g * WKV_UNROLL + k) * C, C), C)
            lwc = lw_s[rows, :]
            cum = lwc
            for d in (1, 2, 4):
                cum = cum + jnp.where(crow >= d, pltpu.roll(cum, d, 0), 0.0)
            for d in (8, 16, 32):
                cum = jnp.concatenate([cum[:d], cum[d:] + cum[:C - d]], axis=0)
            tot = cum[C - 1:C, :]
            g_inc = jnp.exp(cum)
            g_exc = jnp.exp(cum - lwc)
            g_inv = jnp.exp(-cum)
            g_end = jnp.exp(tot - cum)
            g_tot = jnp.exp(tot)
            rr, kk, km, bb, vv = (r_s[rows, :], kkn_s[rows, :], kmod_s[rows, :], b_s[rows, :],
                                  v_s[rows, :])
            qk_a, qr_a = kk * g_exc, rr * g_inc
            kd_a, bd_a = km * g_inv, bb * g_inv
            kc_a, bc_a = km * g_end, bb * g_end
            for hf in range(2):
                ls = slice(hf * hw, (hf + 1) * hw)
                chains.append(dict(k=k, hf=hf, rows=rows, qk=qk_a[:, ls], qr=qr_a[:, ls], kd=kd_a[:, ls],
                                   bd=bd_a[:, ls], kc=kc_a[:, ls], bc=bc_a[:, ls], v=vv[:, ls],
                                   g_tot=g_tot[:, ls]))
        for ch in chains:
            ch['q2'] = jnp.concatenate([ch['qk'], ch['qr']], axis=0).astype(BF16)
            ch['ak'] = _mm_nt(ch['q2'], bd(ch['kd']))
        for ch in chains:
            ab = _mm_nt(ch['q2'], bd(ch['bd']))
            ch['akk'] = jnp.where(strict, ch['ak'][:C], 0.0)
            ch['ark'] = jnp.where(incl, ch['ak'][C:], 0.0)
            ch['arb'] = jnp.where(incl, ab[C:], 0.0)
            ch['n'] = -jnp.where(strict, ab[:C], 0.0)
            ch['tinv'] = eye + ch['n']
        for ch in chains:
            ch['npow'] = _mm(ch['n'], bd(ch['n']))
        for _ in range(4):
            for ch in chains:
                st = _mm(jnp.concatenate([ch['tinv'], ch['npow']], axis=0), bd(ch['npow']))
                ch['tinv'] = ch['tinv'] + st[:C]
                ch['npow'] = st[C:]
        for ch in chains:
            ch['av'] = _mm(jnp.concatenate([ch['akk'], ch['ark']], axis=0), bd(ch['v']))
        for ch in chains:
            ch['tinv'] = ch['tinv'] + _mm(ch['tinv'], bd(ch['npow']))
        for ch in chains:
            ch['u_ind'] = _mm(ch['tinv'], bd(ch['av'][:C]))
            ch['tq'] = _mm(ch['tinv'], bd(ch['qk']))
        def state_terms(ch):
            ch['gc'] = jnp.where(bd_bool, _mm_tn(ch['tq'], ch['bc']), 0.0).astype(BF16)
            ch['nc'] = jnp.where(bd_bool, _mm_tn(jnp.concatenate([ch['v'], ch['u_ind']], axis=0),
                                                 jnp.concatenate([ch['kc'], -ch['bc']], axis=0)), 0.0)

        state = [s_ref[0], s_ref[1]]
        for ch in chains[:2]:
            state_terms(ch)
        for k in range(WKV_UNROLL):
            for ch in chains[2 * k + 2:2 * k + 4]:
                state_terms(ch)
            for ch in chains[2 * k:2 * k + 2]:
                s_old = state[ch['hf']]
                ch['s_b'] = s_old.astype(BF16)
                state[ch['hf']] = s_old * ch['g_tot'] - _mm(ch['s_b'], ch['gc']) + ch['nc']
            for ch in chains[2 * k:2 * k + 2]:
                ch['qs'] = _mm_nt(jnp.concatenate([ch['tq'], ch['qr']], axis=0), ch['s_b'])
        s_ref[0] = state[0]
        s_ref[1] = state[1]
        for ch in chains:
            u = ch['u_ind'] + ch['qs'][:C]
            ch['y'] = ch['qs'][C:] + ch['av'][C:] - _mm(ch['arb'], bd(u))
        for k in range(WKV_UNROLL):
            y_s[chains[2 * k]['rows'], :] = jnp.concatenate([chains[2 * k]['y'], chains[2 * k + 1]['y']], axis=1)
        return carry

    lax.fori_loop(0, tt // (C * WKV_UNROLL), group, 0)

    y = y_s[...]
    inv_n = 1.0 / RWKV_HEAD
    mu = head_sums(y) * inv_n
    yc = y - mu
    var = head_sums(yc * yc) * inv_n
    yn = yc * lax.rsqrt(var + GN_EPS) * vecs[_V_GNW:_V_GNW + 1] + vecs[_V_GNB:_V_GNB + 1]
    rr, km, vv = r_s[...], kmod_s[...], v_s[...]
    bonus = head_sums(rr * km * vecs[_V_RK:_V_RK + 1])
    o_ref[0] = (yn + bonus * vv) * _silu(o_ref[0])

    @pl.when(t == pl.num_programs(1) - 1)
    def _():
        for h in range(RWKV_HEADS):
            o = (h % HALF_HEADS) * RWKV_HEAD
            wkv_out_ref[0, h] = s_ref[h // HALF_HEADS, o:o + RWKV_HEAD, o:o + RWKV_HEAD]


def _rwkv_prompt(x2d, norm_mix3, w_in_b, vfirst, mu3, vecs, wla, v1, v2, l, batch, seq):
    tt = 512
    tiles = seq // tt
    first = vfirst is None
    lv = max(l - 1, 0)
    tok = lambda w: pl.BlockSpec((1, tt, w), lambda b, t: (b, t, 0))
    s5_spec = pl.BlockSpec((S5_TILES, tt // S5_SEG * S5_SEG_PITCH, LANE), lambda b, t: (0, b * tiles + t, 0))
    s5_shape = jax.ShapeDtypeStruct((S5_TILES, batch * seq // S5_SEG * S5_SEG_PITCH, LANE), F32)
    in_specs = [tok(D_MODEL),
                pl.BlockSpec((1, 1, D_MODEL), lambda b, t: (l, 0, 0)),
                pl.BlockSpec((1, D_MODEL, IN_COLS), lambda b, t: (l, 0, 0))]
    args = [x2d.reshape(batch, seq, D_MODEL), norm_mix3, w_in_b]
    if not first:
        in_specs.append(tok(RWKV_WIDTH))
        args.append(vfirst.reshape(batch, seq, RWKV_WIDTH))
    in_specs += [
        pl.BlockSpec((1, 1, SHIFT_COLS), lambda b, t: (l, 0, 0)),
        pl.BlockSpec((1, 8, RWKV_WIDTH), lambda b, t: (l, 0, 0)),
        pl.BlockSpec((1, LORA_W + LORA_A, 2 * RWKV_WIDTH), lambda b, t: (l, 0, 0)),
        pl.BlockSpec((1, RWKV_WIDTH, LORA_V), lambda b, t: (lv, 0, 0)),
        pl.BlockSpec((1, LORA_V, RWKV_WIDTH), lambda b, t: (lv, 0, 0)),
    ]
    args += [mu3, vecs, wla, v1, v2]
    out_specs = [tok(RWKV_WIDTH)]
    out_shape = [jax.ShapeDtypeStruct((batch, seq, RWKV_WIDTH), F32)]
    if first:
        out_specs.append(tok(RWKV_WIDTH))
        out_shape.append(jax.ShapeDtypeStruct((batch, seq, RWKV_WIDTH), F32))
    out_specs += [
        s5_spec, s5_spec,
        pl.BlockSpec((1, 1, SHIFT_COLS), lambda b, t: (b, 0, 0)),
        pl.BlockSpec((1, RWKV_HEADS, RWKV_HEAD, RWKV_HEAD), lambda b, t: (b, 0, 0, 0)),
    ]
    out_shape += [
        s5_shape, s5_shape,
        jax.ShapeDtypeStruct((batch, 1, SHIFT_COLS), F32),
        jax.ShapeDtypeStruct((batch, RWKV_HEADS, RWKV_HEAD, RWKV_HEAD), F32),
    ]
    scratch = [
        pltpu.VMEM((1, SHIFT_COLS), F32),
        pltpu.VMEM((2, HALF_HEADS * RWKV_HEAD, HALF_HEADS * RWKV_HEAD), F32),
    ] + [pltpu.VMEM((tt, RWKV_WIDTH), F32) for _ in range(7)]
    outs = pl.pallas_call(
        functools.partial(_rwkv_prompt_kernel, first_layer=first, tt=tt),
        grid=(batch, seq // tt),
        in_specs=in_specs,
        out_specs=out_specs,
        out_shape=out_shape,
        scratch_shapes=scratch,
        compiler_params=_cparams(("arbitrary", "arbitrary")),
        name="rwkv_prompt",
    )(*args)
    if first:
        o, vf, us5, gs5, sh, wkv = outs
    else:
        (o, us5, gs5, sh, wkv), vf = outs, vfirst
    return (o.reshape(batch * seq, RWKV_WIDTH), vf.reshape(batch * seq, RWKV_WIDTH), us5, gs5,
            sh[:, 0], wkv)


def _rwkv_sample_kernel(*refs, first_layer, seq):
    if first_layer:
        (psh_ref, grw_ref, prev_ref, wkv_ref, mu_ref, vecs_ref, wla_ref, v1_ref, v2_ref,
         o_ref, vf_out_ref, wkv_out_ref) = refs
        vf_ref = None
    else:
        (psh_ref, grw_ref, vf_ref, prev_ref, wkv_ref, mu_ref, vecs_ref, wla_ref, v1_ref, v2_ref,
         o_ref, wkv_out_ref) = refs
        vf_out_ref = None
    nseq = WKV_CHUNK // seq
    p = psh_ref[...]
    rowi = _iota(p.shape, 0)
    prev = jnp.where((rowi & (seq - 1)) == 0, prev_ref[...], pltpu.roll(p, 1, 0))
    ps = p + (prev - p) * mu_ref[0]

    vecs = vecs_ref[0]
    vfirst = None if first_layer else vf_ref[...]
    r, kku, kmod, v, a, lw = _rwkv_prep(ps, vfirst, vecs, wla_ref[0], v1_ref[0], v2_ref[0])
    if first_layer:
        vf_out_ref[...] = v

    def load_head(h):
        sl = slice(h * RWKV_HEAD, (h + 1) * RWKV_HEAD)
        return (r[:, sl], kku[:, sl], kmod[:, sl], v[:, sl], a[:, sl])

    states = [[wkv_ref[0, s, h] for s in range(nseq)] for h in range(RWKV_HEADS)]
    y, new_states = _wkv_chunk(lw, load_head, states, seq, vecs)
    for h in range(RWKV_HEADS):
        for s in range(nseq):
            wkv_out_ref[0, s, h] = new_states[h][s]
    o_ref[...] = y * _silu(grw_ref[...])


def _rwkv_sample(psh, grw, vfirst, prev_rows, wkv_all, mu3, vecs, wla, v1, v2, l, seq):
    rows = psh.shape[0]
    nseq = WKV_CHUNK // seq
    first = vfirst is None
    lv = max(l - 1, 0)
    tok = lambda w: pl.BlockSpec((WKV_CHUNK, w), lambda i: (i, 0))
    st = pl.BlockSpec((1, nseq, RWKV_HEADS, RWKV_HEAD, RWKV_HEAD), lambda i: (l, i, 0, 0, 0))
    in_specs = [tok(SHIFT_COLS), tok(RWKV_WIDTH)]
    args = [psh, grw]
    if not first:
        in_specs.append(tok(RWKV_WIDTH))
        args.append(vfirst)
    in_specs += [
        tok(SHIFT_COLS), st,
        pl.BlockSpec((1, 1, SHIFT_COLS), lambda i: (l, 0, 0)),
        pl.BlockSpec((1, 8, RWKV_WIDTH), lambda i: (l, 0, 0)),
        pl.BlockSpec((1, LORA_W + LORA_A, 2 * RWKV_WIDTH), lambda i: (l, 0, 0)),
        pl.BlockSpec((1, RWKV_WIDTH, LORA_V), lambda i: (lv, 0, 0)),
        pl.BlockSpec((1, LORA_V, RWKV_WIDTH), lambda i: (lv, 0, 0)),
    ]
    wkv_arg = len(args) + 1
    args += [prev_rows, wkv_all, mu3, vecs, wla, v1, v2]
    out_specs = [tok(RWKV_WIDTH)]
    out_shape = [jax.ShapeDtypeStruct((rows, RWKV_WIDTH), F32)]
    if first:
        out_specs.append(tok(RWKV_WIDTH))
        out_shape.append(jax.ShapeDtypeStruct((rows, RWKV_WIDTH), F32))
    out_specs.append(st)
    out_shape.append(jax.ShapeDtypeStruct(wkv_all.shape, F32))
    outs = pl.pallas_call(
        functools.partial(_rwkv_sample_kernel, first_layer=first, seq=seq),
        grid=(rows // WKV_CHUNK,),
        in_specs=in_specs,
        out_specs=out_specs,
        out_shape=out_shape,
        input_output_aliases={wkv_arg: len(out_shape) - 1},
        compiler_params=_cparams(("arbitrary",)),
        name="rwkv_sample",
    )(*args)
    if first:
        o, vf, wkv = outs
    else:
        (o, wkv), vf = outs, vfirst
    sh = psh.reshape(rows // seq, seq, SHIFT_COLS)[:, seq - 1]
    return o, vf, sh, wkv


def _s5_prep_kernel(lam_re_ref, lam_im_ref, ldt_ref, bt_re_ref, bt_im_ref, ct_re_ref, ct_im_ref,
                    wz_ref, wy_ref, ap_re_ref, ap_im_ref, *, cs):
    n = TILE_STATES
    row_g = _iota((LANE, n), 0) >> 4
    lane_g = _iota((LANE, n), 1) >> 6
    emb_mask = row_g == lane_g
    bd_mask = (_iota((LANE, LANE), 0) >> 4) == (_iota((LANE, LANE), 1) >> 4)

    def rep(x):
        return jnp.concatenate([x] * GROUPS_PER_TILE, axis=0)

    def embed(x):
        return jnp.where(emb_mask, rep(x), 0.0)

    for j in range(S5_TILES):
        ls = slice(j * n, (j + 1) * n)
        lr = lam_re_ref[0, :, ls]
        li = lam_im_ref[0, :, ls]
        dt = jnp.exp(ldt_ref[0, :, ls])
        mag = jnp.exp(lr * dt)
        a_re = mag * jnp.cos(li * dt)
        a_im = mag * jnp.sin(li * dt)
        q_re = a_re - 1.0
        q_im = a_im
        den = lr * lr + li * li
        f_re = (q_re * lr + q_im * li) / den
        f_im = (q_im * lr - q_re * li) / den
        b_re = bt_re_ref[0, :, ls]
        b_im = bt_im_ref[0, :, ls]
        bb_re = f_re * b_re - f_im * b_im
        bb_im = f_re * b_im + f_im * b_re
        c_re = ct_re_ref[0, :, ls]
        c_im = ct_im_ref[0, :, ls]
        pw = [(jnp.ones_like(a_re), jnp.zeros_like(a_re))]
        for _ in range(cs):
            pr, pi = pw[-1]
            pw.append((pr * a_re - pi * a_im, pr * a_im + pi * a_re))
        e = [(pr * bb_re - pi * bb_im, pr * bb_im + pi * bb_re) for pr, pi in pw[:cs]]
        wz_ref[0, j, :, 0:n] = jnp.concatenate(
            [embed(e[cs - 1 - s][0]) for s in range(cs)], axis=0).astype(BF16)
        wz_ref[0, j, :, n:2 * n] = jnp.concatenate(
            [embed(e[cs - 1 - s][1]) for s in range(cs)], axis=0).astype(BF16)
        wc_re = jnp.concatenate(
            [embed(c_re * pw[s + 1][0] - c_im * pw[s + 1][1]) for s in range(cs)], axis=0)
        wc_im = jnp.concatenate(
            [embed(-(c_re * pw[s + 1][1] + c_im * pw[s + 1][0])) for s in range(cs)], axis=0)
        wy_ref[0, j, 0:n, :] = wc_re.T.astype(BF16)
        wy_ref[0, j, n:2 * n, :] = wc_im.T.astype(BF16)
        ce_re = embed(c_re)
        ce_im = embed(c_im)
        dt_blk = [jnp.where(bd_mask, _mm_nt_x3(rep(er), ce_re) - _mm_nt_x3(rep(ei), ce_im), 0.0)
                  for er, ei in e]
        zero = jnp.zeros((LANE, LANE), F32)
        wy_ref[0, j, 2 * n:, :] = jnp.concatenate(
            [jnp.concatenate([dt_blk[s - sp] if sp <= s else zero for s in range(cs)], axis=1)
             for sp in range(cs)], axis=0).astype(BF16)
        pr, pi = pw[cs // 2]
        for k in range(S5_SCAN_ROWS):
            ap_re_ref[0, k:k + 1, ls] = pr
            ap_im_ref[0, k:k + 1, ls] = pi
            pr, pi = pr * pr - pi * pi, 2.0 * pr * pi


def _s5_prep(lam_re, lam_im, ldt_rep, bt_re, bt_im, ct_re, ct_im, cs):
    vec = pl.BlockSpec((1, 1, S5_LANES), lambda l: (l, 0, 0))
    mat = pl.BlockSpec((1, S5_GROUP, S5_LANES), lambda l: (l, 0, 0))
    k = cs * LANE
    wz_dims = (S5_TILES, k, 2 * TILE_STATES)
    wy_dims = (S5_TILES, 2 * TILE_STATES + k, k)
    ap_spec = pl.BlockSpec((1, S5_SCAN_ROWS, S5_LANES), lambda l: (l, 0, 0))
    ap_shape = jax.ShapeDtypeStruct((DEPTH, S5_SCAN_ROWS, S5_LANES), F32)
    return pl.pallas_call(
        functools.partial(_s5_prep_kernel, cs=cs),
        grid=(DEPTH,),
        in_specs=[vec, vec, vec, mat, mat, mat, mat],
        out_specs=[pl.BlockSpec((1,) + wz_dims, lambda l: (l, 0, 0, 0)),
                   pl.BlockSpec((1,) + wy_dims, lambda l: (l, 0, 0, 0)), ap_spec, ap_spec],
        out_shape=[jax.ShapeDtypeStruct((DEPTH,) + wz_dims, BF16),
                   jax.ShapeDtypeStruct((DEPTH,) + wy_dims, BF16), ap_shape, ap_shape],
        compiler_params=_cparams(("arbitrary",)),
        name="s5_prep",
    )(lam_re, lam_im, ldt_rep, bt_re, bt_im, ct_re, ct_im)


def _gelu(x):
    return 0.5 * x * (1.0 + lax.erf(x * (1.0 / math.sqrt(2.0))))


def _s5_kernel(*refs, cs, rows, chain, ap0):
    if chain:
        (u_ref, g_ref, wz_ref, wy_ref, ap_re_ref, ap_im_ref,
         dsk_ref, wglu_ref, o_ref, hr_out_ref, hi_out_ref, hc_r_ref, hc_i_ref) = refs

        @pl.when(pl.program_id(1) == 0)
        def _():
            hc_r_ref[...] = jnp.zeros_like(hc_r_ref)
            hc_i_ref[...] = jnp.zeros_like(hc_i_ref)
    else:
        (u_ref, g_ref, h0r_ref, h0i_ref, wz_ref, wy_ref, ap_re_ref, ap_im_ref,
         dsk_ref, wglu_ref, o_ref, hr_out_ref, hi_out_ref) = refs
    n = TILE_STATES
    k = cs * LANE
    if chain:
        steps = rows // SUBLANES

        def load_rows(ref, j, s):
            return jnp.concatenate([ref[j, pl.ds(i * cs + s, SUBLANES, stride=S5_SEG_PITCH), :]
                                    for i in range(steps)], axis=0)

        def store_rows(ref, j, s, val):
            for i in range(steps):
                ref[j, pl.ds(i * cs + s, SUBLANES, stride=S5_SEG_PITCH), :] = (
                    val[i * SUBLANES:(i + 1) * SUBLANES])

        pad = jnp.zeros((S5_SEG_PITCH - S5_SEG, LANE), F32)
        for j in range(S5_TILES):
            for g in range(SUBLANES):
                o_ref[j, g * S5_SEG_PITCH + S5_SEG:(g + 1) * S5_SEG_PITCH, :] = pad
    else:
        def load_rows(ref, j, s):
            return ref[j, pl.ds(s, rows, stride=cs), :]

        def store_rows(ref, j, s, val):
            ref[j, pl.ds(s, rows, stride=cs), :] = val

    us = [[load_rows(u_ref, j, s) for s in range(cs)] for j in range(S5_TILES)]
    x8 = [jnp.concatenate(us[j], axis=1).astype(BF16) for j in range(S5_TILES)]
    z = [jnp.dot(x8[j], wz_ref[0, j], preferred_element_type=F32) for j in range(S5_TILES)]
    hin = []
    for j in range(S5_TILES):
        ls = slice(j * n, (j + 1) * n)
        zr, zi = z[j][:, :n], z[j][:, n:]
        ar, ai = ap_re_ref[0, ap0:ap0 + 1, ls], ap_im_ref[0, ap0:ap0 + 1, ls]
        if chain:
            hr = jnp.zeros((SUBLANES, n), F32)
            hi = jnp.zeros((SUBLANES, n), F32)
            loc = []
            for i in range(steps):
                loc.append((hr, hi))
                rs = slice(i * SUBLANES, (i + 1) * SUBLANES)
                hr, hi = ar * hr - ai * hi + zr[rs], ar * hi + ai * hr + zi[rs]
            lvl = ap0 + int(math.log2(steps))
            sr, si = ap_re_ref[0, lvl:lvl + 1, ls], ap_im_ref[0, lvl:lvl + 1, ls]
            cr, ci = hc_r_ref[:, ls], hc_i_ref[:, ls]
            seg_r, seg_i = [], []
            for g in range(SUBLANES):
                seg_r.append(cr)
                seg_i.append(ci)
                cr, ci = sr * cr - si * ci + hr[g:g + 1], sr * ci + si * cr + hi[g:g + 1]
            hc_r_ref[:, ls] = cr
            hc_i_ref[:, ls] = ci
            hr_out_ref[0, :, ls] = cr
            hi_out_ref[0, :, ls] = ci
            seg_r = jnp.concatenate(seg_r, axis=0)
            seg_i = jnp.concatenate(seg_i, axis=0)
            pr, pi = jnp.ones_like(ar), jnp.zeros_like(ar)
            hin_r, hin_i = [], []
            for i in range(steps):
                hin_r.append(loc[i][0] + pr * seg_r - pi * seg_i)
                hin_i.append(loc[i][1] + pr * seg_i + pi * seg_r)
                pr, pi = pr * ar - pi * ai, pr * ai + pi * ar
            hin_r = jnp.concatenate(hin_r, axis=0)
            hin_i = jnp.concatenate(hin_i, axis=0)
        else:
            hin_r, hin_i = h0r_ref[:, ls], h0i_ref[:, ls]
            hr_out_ref[:, ls] = ar * hin_r - ai * hin_i + zr
            hi_out_ref[:, ls] = ar * hin_i + ai * hin_r + zi
        hin.append((hin_r.astype(BF16), hin_i.astype(BF16)))
    y8 = []
    for j in range(S5_TILES):
        lhs = jnp.concatenate([hin[j][0], hin[j][1], x8[j]], axis=1)
        cols = []
        for b in range(k // 256):
            kk = 2 * n + (b + 1) * 256
            cols.append(jnp.dot(lhs[:, :kk], wy_ref[0, j, 0:kk, b * 256:(b + 1) * 256],
                                preferred_element_type=F32))
        y8.append(jnp.concatenate(cols, axis=1))
    dsk = dsk_ref[0]
    for s in range(cs):
        y = jnp.concatenate([y8[j][:, s * LANE:(s + 1) * LANE] for j in range(S5_TILES)], axis=1)
        u_s = jnp.concatenate([us[j][s] for j in range(S5_TILES)], axis=1)
        y = _gelu(y + dsk * u_s)
        y = y * _sigmoid(jnp.dot(y.astype(BF16), wglu_ref[0], preferred_element_type=F32))
        gate = jnp.concatenate([load_rows(g_ref, j, s) for j in range(S5_TILES)], axis=1)
        out = y * _silu(gate)
        for j in range(S5_TILES):
            store_rows(o_ref, j, s, out[:, j * LANE:(j + 1) * LANE])


def _s5_mix(u, g, h0, ops, dsk3, wglu_b, l, cs, nb):
    chain = h0 is None
    wz, wy, ap_re, ap_im = ops
    if chain:
        ntok = SUBLANES * S5_SEG
        assert S5_SEG % cs == 0 and (S5_SEG // cs) & (S5_SEG // cs - 1) == 0
        tiles = u.shape[1] // S5_SEG_PITCH // SUBLANES // nb
        tok = pl.BlockSpec((S5_TILES, SUBLANES * S5_SEG_PITCH, LANE), lambda b, t: (0, b * tiles + t, 0))
    else:
        ntok = u.shape[1]
        tiles = 1
        tok = pl.BlockSpec((S5_TILES, ntok, LANE), lambda b, t: (0, 0, 0))
    rows = ntok // cs
    ap_spec = pl.BlockSpec((1, S5_SCAN_ROWS, S5_LANES), lambda b, t: (l, 0, 0))
    in_specs = [tok, tok]
    args = [u, g]
    scratch = []
    if not chain:
        hs = pl.BlockSpec((rows, S5_LANES), lambda b, t: (0, 0))
        in_specs += [hs, hs]
        args += [h0[0], h0[1]]
        st_spec = hs
        st_shape = jax.ShapeDtypeStruct((rows, S5_LANES), F32)
    else:
        st_spec = pl.BlockSpec((1, 1, S5_LANES), lambda b, t: (b, 0, 0))
        st_shape = jax.ShapeDtypeStruct((nb, 1, S5_LANES), F32)
        scratch = [pltpu.VMEM((1, S5_LANES), F32), pltpu.VMEM((1, S5_LANES), F32)]
    k = cs * LANE
    cs_ops = wz.shape[2] // LANE
    ap0 = int(math.log2(2 * cs // cs_ops))
    in_specs += [
        pl.BlockSpec((1, S5_TILES, k, 2 * TILE_STATES), lambda b, t: (l, 0, cs_ops // cs - 1, 0)),
        pl.BlockSpec((1, S5_TILES, 2 * TILE_STATES + k, k), lambda b, t: (l, 0, 0, 0)),
        ap_spec, ap_spec,
        pl.BlockSpec((1, 1, S5_WIDTH), lambda b, t: (l, 0, 0)),
        pl.BlockSpec((1, S5_WIDTH, S5_WIDTH), lambda b, t: (l, 0, 0)),
    ]
    args += [wz, wy, ap_re, ap_im, dsk3, wglu_b]
    o, hr, hi = pl.pallas_call(
        functools.partial(_s5_kernel, cs=cs, rows=rows, chain=chain, ap0=ap0),
        grid=(nb, tiles),
        in_specs=in_specs,
        out_specs=[tok, st_spec, st_spec],
        out_shape=[jax.ShapeDtypeStruct(u.shape, F32), st_shape, st_shape],
        scratch_shapes=scratch,
        compiler_params=_cparams(("arbitrary", "arbitrary")),
        name="s5_mix",
    )(*args)
    return o, hr, hi


def _softmax(s):
    e = jnp.exp(s - jnp.max(s, axis=-1, keepdims=True))
    return (e / jnp.sum(e, axis=-1, keepdims=True)).astype(BF16)


def _attend(q, mk_ref, mv_ref):
    heads = [slice(h * X_HEAD_DIM, (h + 1) * X_HEAD_DIM) for h in range(X_HEADS)]
    s = [_mm_nt(q[:, hs], mk_ref[0, 0, :, hs]) for hs in heads]
    p = [_softmax(sh) for sh in s]
    return jnp.concatenate([_mm(ph, mv_ref[0, 0, :, hs]) for ph, hs in zip(p, heads)], axis=-1)


def _attend_tiled(q, mk_ref, mv_ref, nseg, row_seg):
    def head(ref, i, h):
        return jnp.concatenate([ref[0, i, pl.ds(dt * X_HEADS + h, MEM_LEN, stride=_MEM_SUB), :]
                                for dt in range(_DT)], axis=1)

    def pick(vals):
        out = vals[0]
        for i in range(1, nseg):
            out = jnp.where(row_seg == i, vals[i], out)
        return out

    qb = q.astype(BF16)
    heads = [slice(h * X_HEAD_DIM, (h + 1) * X_HEAD_DIM) for h in range(X_HEADS)]
    s = [[_mm_nt(qb[:, hs], head(mk_ref, i, h)) for i in range(nseg)] for h, hs in enumerate(heads)]
    p = [_softmax(pick(sh)) for sh in s]
    outs = [pick([_mm(p[h], head(mv_ref, i, h)) for i in range(nseg)]) for h in range(X_HEADS)]
    return jnp.concatenate(outs, axis=-1)


def _post_kernel(x_ref, orw_ref, os5_ref, wout_ref, nx_ref, wq_ref, wo_ref, mk_ref, mv_ref, nf_ref,
                 o_ref, *, nseg, seg_rows, final):
    x = x_ref[...]
    if nseg == 1:
        nq = x.shape[0] // S5_SEG
        os5 = jnp.concatenate(
            [jnp.concatenate([os5_ref[j, q * S5_SEG_PITCH:q * S5_SEG_PITCH + S5_SEG, :] for q in range(nq)], axis=0)
             for j in range(S5_TILES)], axis=1)
    else:
        os5 = jnp.concatenate([os5_ref[j] for j in range(S5_TILES)], axis=1)
    x1 = (x + jnp.dot(orw_ref[...].astype(BF16), wout_ref[0, 0:RWKV_WIDTH, :], preferred_element_type=F32)
          + jnp.dot(os5.astype(BF16), wout_ref[0, RWKV_WIDTH:, :], preferred_element_type=F32))
    xc = _rms(x1, nx_ref[0]).astype(BF16)
    q = jnp.dot(xc, wq_ref[0], preferred_element_type=F32) * (1.0 / math.sqrt(X_HEAD_DIM))
    if nseg == 1:
        att = _attend(q, mk_ref, mv_ref)
    else:
        row_seg = _iota((q.shape[0], 1), 0) >> int(math.log2(seg_rows))
        att = _attend_tiled(q, mk_ref, mv_ref, nseg, row_seg)
    x2 = x1 + jnp.dot(att.astype(BF16), wo_ref[0], preferred_element_type=F32)
    if final:
        x2 = _rms(x2, nf_ref[...])
    o_ref[...] = x2


def _post(x2d, orw, os5, wout_b, nx3, wq_b, wo_b, mk, mv, nf2, l, nb, nseg, seg_rows, tq, final):
    tiles = x2d.shape[0] // nb // tq
    tok = lambda w: pl.BlockSpec((tq, w), lambda b, t: (b * tiles + t, 0))
    s5_rows = tq // S5_SEG * S5_SEG_PITCH if nseg == 1 else tq
    tok_s5 = pl.BlockSpec((S5_TILES, s5_rows, LANE), lambda b, t: (0, b * tiles + t, 0))
    if nseg == 1:
        mem = pl.BlockSpec((1, 1, MEM_LEN, D_MODEL), lambda b, t: (l, b, 0, 0))
    else:
        mem = pl.BlockSpec((1, nseg, MEM_LEN * _MEM_SUB, LANE), lambda b, t: (l, t, 0, 0))
    wsp = pl.BlockSpec((1, D_MODEL, D_MODEL), lambda b, t: (l, 0, 0))
    return pl.pallas_call(
        functools.partial(_post_kernel, nseg=nseg, seg_rows=seg_rows, final=final),
        grid=(nb, tiles),
        in_specs=[tok(D_MODEL), tok(RWKV_WIDTH), tok_s5, wsp,
                  pl.BlockSpec((1, 1, D_MODEL), lambda b, t: (l, 0, 0)), wsp, wsp, mem, mem,
                  pl.BlockSpec((1, D_MODEL), lambda b, t: (0, 0))],
        out_specs=tok(D_MODEL),
        out_shape=jax.ShapeDtypeStruct(x2d.shape, F32),
        compiler_params=_cparams(("arbitrary", "arbitrary")),
        name="post",
    )(x2d, orw, os5, wout_b, nx3, wq_b, wo_b, mk, mv, nf2)


def _trunk(x, states, mem_k, mem_v, W):
    batch, seq, _ = x.shape
    prompt = states is None
    x2d = x.reshape(batch * seq, D_MODEL)
    new_shift, new_wkv, new_re, new_im = [], [], [], []
    vfirst = None
    wkv_all = None if prompt else states[1]
    for l in range(DEPTH):
        rw_args = (W['mu'], W['vecs'], W['wla'], W['v1'], W['v2'], l)
        if prompt:
            orw, vfirst, us5, gs5, sh, wkv = _rwkv_prompt(x2d, W['norm_mix'], W['w_in'], vfirst, *rw_args,
                                                          batch, seq)
            os5, hr, hi = _s5_mix(us5, gs5, None, W['s5_ops'], W['d_skip'], W['w_glu'], l, S5_CHUNK, batch)
            hr = hr.reshape(batch, S5_GROUPS, S5_STATE)
            hi = hi.reshape(batch, S5_GROUPS, S5_STATE)
            x2d = _post(x2d, orw, os5, W['w_out'], W['norm_x'], W['wq'], W['wo'], mem_k, mem_v,
                        W['norm_f'], l, batch, 1, seq, 1024, l == DEPTH - 1)
        else:
            shift0, _, re0, im0 = states
            psh, grw, us5, gs5 = _inproj(x2d, W['norm_mix'], W['w_in'], l)
            prev_rows = jnp.repeat(shift0[l], seq, axis=0)
            orw, vfirst, sh, wkv_all = _rwkv_sample(psh, grw, vfirst, prev_rows, wkv_all, *rw_args, seq)
            h0 = (re0[l].reshape(batch, S5_LANES), im0[l].reshape(batch, S5_LANES))
            os5, hr, hi = _s5_mix(us5, gs5, h0, W['s5_ops'], W['d_skip'], W['w_glu'], l, seq, 1)
            hr = hr.reshape(batch, S5_GROUPS, S5_STATE)
            hi = hi.reshape(batch, S5_GROUPS, S5_STATE)
            nseg = 32 // seq
            x2d = _post(x2d, orw, os5, W['w_out'], W['norm_x'], W['wq'], W['wo'], mem_k, mem_v,
                        W['norm_f'], l, 1, nseg, seq, nseg * seq, l == DEPTH - 1)
        new_shift.append(sh)
        if prompt:
            new_wkv.append(wkv)
        new_re.append(hr)
        new_im.append(hi)
    y = x2d.reshape(batch, seq, D_MODEL)
    wkv_out = jnp.stack(new_wkv) if prompt else wkv_all
    return y, jnp.stack(new_shift), wkv_out, jnp.stack(new_re), jnp.stack(new_im)


def kernel(x_prompt, x_sample, state_shift, state_wkv, state_s5_re, state_s5_im, cache_mem_k, cache_mem_v, mem_prompt, norm_mix, w_in, mu_shift, w0, w2, a0, a2, v0, v1, v2, k_k, k_a, r_k, gn_w, gn_b, lam_re, lam_im, log_dt, b_re, b_im, c_re, c_im, d_skip, w_glu, w_out, norm_x, norm_mem, wq, wk, wv, wo, norm_f):
    L = DEPTH
    bp, mp = mem_prompt.shape[0], mem_prompt.shape[1]
    zpad = jnp.zeros((1, RWKV_WIDTH), F32)
    vecs = jnp.stack([w0, a0, k_k, k_a, gn_w, gn_b, r_k.reshape(L, RWKV_WIDTH),
                      jnp.concatenate([zpad, v0], axis=0)], axis=1)
    zl = jnp.zeros((L, LORA_W, RWKV_WIDTH), F32)
    wla = jnp.concatenate([jnp.concatenate([w2, zl], axis=2),
                           jnp.concatenate([zl, a2], axis=2)], axis=1).astype(BF16)
    tr = lambda t: jnp.transpose(t, (0, 3, 1, 2)).reshape(L, S5_GROUP, S5_LANES)
    s5_in = (lam_re.reshape(L, 1, S5_LANES), lam_im.reshape(L, 1, S5_LANES),
             jnp.repeat(log_dt, S5_STATE, axis=1).reshape(L, 1, S5_LANES),
             tr(b_re), tr(b_im),
             jnp.transpose(c_re, (0, 2, 1, 3)).reshape(L, S5_GROUP, S5_LANES),
             jnp.transpose(c_im, (0, 2, 1, 3)).reshape(L, S5_GROUP, S5_LANES))
    W = dict(
        norm_mix=norm_mix.reshape(L, 1, D_MODEL), w_in=w_in.astype(BF16),
        mu=mu_shift.reshape(L, 1, SHIFT_COLS), vecs=vecs, wla=wla,
        v1=v1.astype(BF16), v2=v2.astype(BF16),
        s5_ops=_s5_prep(*s5_in, S5_CHUNK),
        d_skip=d_skip.reshape(L, 1, S5_WIDTH), w_glu=w_glu.astype(BF16),
        w_out=w_out.astype(BF16), norm_x=norm_x.reshape(L, 1, D_MODEL),
        wq=wq.astype(BF16), wo=wo.astype(BF16), norm_f=norm_f.reshape(1, D_MODEL),
    )
    pk, pv, pk_t, pv_t = _memkv(mem_prompt.reshape(bp * mp, D_MODEL), norm_mem,
                                wk.astype(BF16), wv.astype(BF16))
    pk = pk.reshape(L, bp, mp, D_MODEL)
    pv = pv.reshape(L, bp, mp, D_MODEL)
    y_prompt, p_shift, p_wkv, p_re, p_im = _trunk(x_prompt, None, pk, pv, W)
    tiled = lambda c: jnp.transpose(c.reshape(L, -1, MEM_LEN, X_HEADS, _DT, LANE),
                                    (0, 1, 2, 4, 3, 5)).reshape(L, -1, MEM_LEN * _MEM_SUB, LANE)
    untiled = lambda c: jnp.transpose(c.reshape(L, -1, MEM_LEN, _DT, X_HEADS, LANE),
                                      (0, 1, 2, 4, 3, 5)).reshape(L, -1, MEM_LEN, X_HEADS, X_HEAD_DIM)
    y_sample, s_shift, s_wkv, s_re, s_im = _trunk(
        x_sample, (state_shift, state_wkv, state_s5_re, state_s5_im), tiled(cache_mem_k),
        tiled(cache_mem_v), W)
    return (y_prompt, y_sample, p_shift, p_wkv, p_re, p_im, untiled(pk_t), untiled(pv_t),
            s_shift, s_wkv, s_re, s_im)
```
